```python
import math
import jax, jax.numpy as jnp
from jax import lax
import numpy as np

D_MODEL = 2048
BATCH = 32
SEQ = 256
DEPTH = 2
DEC_BATCH = 2
DEC_SEQ = 1024
PAST_LEN = 256

GRID_W = 64
N_EVEN = (DEPTH + 1) // 2
N_ODD = DEPTH // 2
N_MOD = 9
D_FF = 5632
ROPE_THETA = 10000.0
Q_BLOCK = 128
EPS = 1e-6

MLA_HEADS = 8
MLA_Q_RANK = 512
MLA_KV_RANK = 512
MLA_NOPE = 128
MLA_ROPE = 64
MLA_V = 128

GQA_HEADS = 8
GQA_KV_HEADS = 2
GQA_GROUP = GQA_HEADS // GQA_KV_HEADS
HEAD_DIM = 128

EVEN_SPLITS = (MLA_Q_RANK, MLA_KV_RANK, MLA_ROPE, GQA_HEADS * HEAD_DIM, GQA_KV_HEADS * HEAD_DIM, GQA_KV_HEADS * HEAD_DIM)
EVEN_IN = MLA_Q_RANK + MLA_KV_RANK + MLA_ROPE + GQA_HEADS * HEAD_DIM + 2 * GQA_KV_HEADS * HEAD_DIM
EVEN_MIX = MLA_HEADS * MLA_V + GQA_HEADS * HEAD_DIM

RET_HEADS = 8
RET_DK = 128
RET_DV = 128
RET_CHUNK = 128
HY_CH = 1024
HY_ORDER = 2
HY_BANDS = 16
HY_EMB = 1 + 2 * HY_BANDS
HY_FHID = 64
ODD_SPLITS = (RET_HEADS * RET_DK, RET_HEADS * RET_DK, RET_HEADS * RET_DV, RET_HEADS * RET_DV, (HY_ORDER + 1) * HY_CH)
ODD_IN = 2 * RET_HEADS * RET_DK + 2 * RET_HEADS * RET_DV + (HY_ORDER + 1) * HY_CH
ODD_MIX = RET_HEADS * RET_DV + HY_CH

kernel_name = 'hybrid_diffusion_mla_gqa_retention_hyena'


def _split(u, sizes):
    return jnp.split(u, np.cumsum(np.array(sizes))[:-1].tolist(), axis=-1)


def _rms(x, g=None):
    xf = x.astype(jnp.float32)
    y = xf * lax.rsqrt(jnp.mean(xf * xf, axis=-1, keepdims=True) + EPS)
    if g is not None:
        y = y * g.astype(jnp.float32)
    return y.astype(x.dtype)


def _axial_rope(length, dim, dtype):
    f32 = jnp.float32
    n_rows = length // GRID_W
    row = jnp.repeat(jnp.arange(n_rows, dtype=f32), GRID_W)
    col = jnp.tile(jnp.arange(GRID_W, dtype=f32), n_rows)
    half = dim // 2
    freq = ROPE_THETA ** (-jnp.arange(0, half, 2, dtype=f32) / half)
    ang = jnp.concatenate([row[:, None] * freq[None], col[:, None] * freq[None]], axis=-1)
    return jnp.cos(ang).astype(dtype), jnp.sin(ang).astype(dtype)


def _rope(x, cos, sin):
    x1, x2 = jnp.split(x, 2, axis=-1)
    c = cos[:, None, :]
    s = sin[:, None, :]
    return jnp.concatenate([x1 * c - x2 * s, x1 * s + x2 * c], axis=-1)


def _attention(q, k, v):
    b, sq, hk, g, d = q.shape
    scale = d ** -0.5
    nb = sq // Q_BLOCK
    qb = jnp.moveaxis(q.reshape(b, nb, Q_BLOCK, hk, g, d), 1, 0)

    def block(qi):
        s = jnp.einsum('bqhgd,bkhd->bhgqk', qi, k, preferred_element_type=jnp.float32) * scale
        p = jax.nn.softmax(s, axis=-1).astype(v.dtype)
        return jnp.einsum('bhgqk,bkhv->bqhgv', p, v)

    o = lax.map(block, qb)
    return jnp.moveaxis(o, 0, 1).reshape(b, sq, hk * g * v.shape[-1])


def _even_mixer(h, w_in, q_norm, w_qb, kv_norm, w_kvb, nope_g, rope_g, gqa_g, w_out, cache):
    b, L, _ = h.shape
    q_a, kv_a, kr, qg, kg, vg = _split(h @ w_in, EVEN_SPLITS)
    q = (_rms(q_a, q_norm) @ w_qb).reshape(b, L, MLA_HEADS, MLA_NOPE + MLA_ROPE)
    q_nope = _rms(q[..., :MLA_NOPE], nope_g[0])
    q_rope = _rms(q[..., MLA_NOPE:], rope_g[0])
    ckv = _rms(kv_a, kv_norm)
    k_rope = _rms(kr, rope_g[1])
    qg = _rms(qg.reshape(b, L, GQA_HEADS, HEAD_DIM), gqa_g[0])
    kg = _rms(kg.reshape(b, L, GQA_KV_HEADS, HEAD_DIM), gqa_g[1])
    vg = vg.reshape(b, L, GQA_KV_HEADS, HEAD_DIM)
    if cache is None:
        state = (ckv, k_rope, kg, vg)
        ckv_all, krope_all, kg_all, vg_all = state
    else:
        cos_m, sin_m = _axial_rope(L, MLA_ROPE, h.dtype)
        cos_g, sin_g = _axial_rope(L, HEAD_DIM, h.dtype)
        q_rope = _rope(q_rope, cos_m, sin_m)
        k_rope = _rope(k_rope[:, :, None, :], cos_m, sin_m)[:, :, 0]
        qg = _rope(qg, cos_g, sin_g)
        kg = _rope(kg, cos_g, sin_g)
        c_ckv, c_krope, c_k, c_v = cache
        ckv_all = jnp.concatenate([c_ckv, ckv], axis=1)
        krope_all = jnp.concatenate([c_krope, k_rope], axis=1)
        kg_all = jnp.concatenate([c_k, kg], axis=1)
        vg_all = jnp.concatenate([c_v, vg], axis=1)
        state = None
    sk = ckv_all.shape[1]
    kv = (ckv_all @ w_kvb).reshape(b, sk, MLA_HEADS, MLA_NOPE + MLA_V)
    k_m = jnp.concatenate([_rms(kv[..., :MLA_NOPE], nope_g[1]),
                           jnp.broadcast_to(krope_all[:, :, None, :], (b, sk, MLA_HEADS, MLA_ROPE))], axis=-1)
    q_m = jnp.concatenate([q_nope, q_rope], axis=-1)[:, :, :, None, :]
    o_m = _attention(q_m, k_m, kv[..., MLA_NOPE:])
    o_g = _attention(qg.reshape(b, L, GQA_KV_HEADS, GQA_GROUP, HEAD_DIM), kg_all, vg_all)
    return jnp.concatenate([o_m, o_g], axis=-1) @ w_out, state


def _retention(q, k, v, log_g, s0):
    f32 = jnp.float32
    b, nh, L, _ = q.shape
    dv = v.shape[-1]
    n = L // RET_CHUNK
    idx = jnp.arange(RET_CHUNK, dtype=f32)
    lg = log_g.astype(f32)
    diff = idx[:, None] - idx[None, :]
    intra = jnp.where(diff >= 0, jnp.exp(jnp.maximum(diff, 0.0)[None] * lg[:, None, None]), 0.0)
    q_dec = jnp.exp((idx + 1.0)[None] * lg[:, None])
    k_dec = jnp.exp((RET_CHUNK - 1.0 - idx)[None] * lg[:, None])
    c_dec = jnp.exp(RET_CHUNK * lg)

    def chunks(x):
        return jnp.moveaxis(x.astype(f32).reshape(b, nh, n, RET_CHUNK, x.shape[-1]), 2, 0)

    def step(s, inp):
        qc, kc, vc = inp
        att = jnp.einsum('bhnd,bhmd->bhnm', qc, kc) * intra
        o = jnp.einsum('bhnm,bhmv->bhnv', att, vc) + jnp.einsum('bhnd,bhdv->bhnv', qc, s) * q_dec[..., None]
        s = s * c_dec[:, None, None] + jnp.einsum('bhmd,bhmv->bhdv', kc * k_dec[..., None], vc)
        return s, o

    s_fin, o = lax.scan(step, s0, (chunks(q), chunks(k), chunks(v)))
    return jnp.moveaxis(o, 0, 2).reshape(b, nh, L, dv), s_fin


def _bidir_retention(q, k, v, log_g, s0):
    o_f, s_f = _retention(q, k, v, log_g[0], s0[:, 0])
    flip = lambda x: jnp.flip(x, axis=2)
    o_b, s_b = _retention(flip(q), flip(k), flip(v), log_g[1], s0[:, 1])
    return o_f + flip(o_b), jnp.stack([s_f, s_b], axis=1)


def _short_conv(x, w, bias):
    xp = jnp.pad(x, ((0, 0), (1, 1), (0, 0)))
    return xp[:, :-2] * w[0] + xp[:, 1:-1] * w[1] + xp[:, 2:] * w[2] + bias


def _hyena_filters(L, w1, b1, w2, b2, freq, w3, decay):
    f32 = jnp.float32
    t = jnp.arange(L, dtype=f32)
    tn = t / L
    bands = jnp.arange(1, HY_BANDS + 1, dtype=f32)
    ang = 2.0 * math.pi * tn[:, None] * bands[None, :]
    feat = jnp.concatenate([tn[:, None], jnp.sin(ang), jnp.cos(ang)], axis=-1)
    z = jnp.sin(freq[0].astype(f32) * (feat @ w1.astype(f32) + b1.astype(f32)))
    z = jnp.sin(freq[1].astype(f32) * (z @ w2.astype(f32) + b2.astype(f32)))
    filt = z @ w3.astype(f32)
    r = jnp.abs(t - L // 2) / (L / 2)
    return filt * jnp.exp(-r[:, None] * jnp.abs(decay.astype(f32))[None, :])


def _fft_conv(z, h):
    L = z.shape[1]
    n = 2 * L
    zf = jnp.fft.rfft(z.astype(jnp.float32), n=n, axis=1)
    hf = jnp.fft.rfft(h, n=n, axis=0)
    y = jnp.fft.irfft(zf * hf[None], n=n, axis=1)
    return y[:, L // 2: L // 2 + L].astype(z.dtype)


def _hyena(u, conv_w, conv_b, w1, b1, w2, b2, freq, w3, decay, skip):
    L = u.shape[1]
    v, x1, x2 = jnp.split(_short_conv(u, conv_w, conv_b), HY_ORDER + 1, axis=-1)
    filt = _hyena_filters(L, w1, b1, w2, b2, freq, w3, decay)
    z = x1 * (_fft_conv(v, filt[:, :HY_CH]) + skip[:HY_CH] * v)
    return x2 * (_fft_conv(z, filt[:, HY_CH:]) + skip[HY_CH:] * z)


def _odd_mixer(h, w_in, decay_logit, ret_g, conv_w, conv_b, w1, b1, w2, b2, freq, w3, decay, skip, w_out, s0):
    b, L, _ = h.shape
    q, k, v, g, hy = _split(h @ w_in, ODD_SPLITS)
    q = q.reshape(b, L, RET_HEADS, RET_DK).transpose(0, 2, 1, 3)
    k = (k * (RET_DK ** -0.5)).reshape(b, L, RET_HEADS, RET_DK).transpose(0, 2, 1, 3)
    v = v.reshape(b, L, RET_HEADS, RET_DV).transpose(0, 2, 1, 3)
    log_g = jax.nn.log_sigmoid(decay_logit.astype(jnp.float32))
    o, state = _bidir_retention(q, k, v, log_g, s0.astype(jnp.float32))
    o = _rms(o.astype(h.dtype).transpose(0, 2, 1, 3)).reshape(b, L, RET_HEADS * RET_DV) * ret_g
    o_ret = o * jax.nn.silu(g)
    o_hy = _hyena(hy, conv_w, conv_b, w1, b1, w2, b2, freq, w3, decay, skip)
    return jnp.concatenate([o_ret, o_hy], axis=-1) @ w_out, state.astype(h.dtype)


def _modulate(x, mod, i):
    shift, scale, gate = mod[:, 3 * i], mod[:, 3 * i + 1], mod[:, 3 * i + 2]
    return _rms(x) * (1.0 + scale[:, None]) + shift[:, None], gate[:, None]


def _swiglu(h, wg, wu, wd):
    return (jax.nn.silu(h @ wg) * (h @ wu)) @ wd


def setup_inputs(seed: int = 0) -> dict:
    key = jax.random.key(seed)
    keys = jax.random.split(key, 64)
    counter = [0]
    f32 = jnp.float32

    def nrm(shape, scale=1.0):
        kk = keys[counter[0]]
        counter[0] += 1
        return jax.random.normal(kk, shape, f32) * scale

    def gain(shape):
        return 1.0 + nrm(shape, 0.02)

    base_logit = jnp.log(2.0 ** (5.0 + jnp.arange(RET_HEADS, dtype=f32)) - 1.0)
    decay_lo = -math.log(1e-2) / 1.5
    decay_hi = -math.log(1e-2) / 0.3
    return {
        'x_prompt': nrm((BATCH, SEQ, D_MODEL)),
        'x_sample': nrm((DEC_BATCH, DEC_SEQ, D_MODEL)),
        'cache_mla_ckv': nrm((DEC_BATCH, N_EVEN, PAST_LEN, MLA_KV_RANK)),
        'cache_mla_krope': nrm((DEC_BATCH, N_EVEN, PAST_LEN, MLA_ROPE)),
        'cache_gqa_k': nrm((DEC_BATCH, N_EVEN, PAST_LEN, GQA_KV_HEADS, HEAD_DIM)),
        'cache_gqa_v': nrm((DEC_BATCH, N_EVEN, PAST_LEN, GQA_KV_HEADS, HEAD_DIM)),
        'state_ret': nrm((DEC_BATCH, N_ODD, 2, RET_HEADS, RET_DK, RET_DV), 0.5),
        'c': nrm((DEC_BATCH, D_MODEL)),
        'c_ctx': nrm((D_MODEL,)),
        'mod_w': nrm((DEPTH, D_MODEL, N_MOD * D_MODEL), 0.5 * D_MODEL ** -0.5),
        'mod_b': nrm((DEPTH, N_MOD * D_MODEL), 0.02),
        'ffn_w_gate': nrm((DEPTH, 2, D_MODEL, D_FF), D_MODEL ** -0.5),
        'ffn_w_up': nrm((DEPTH, 2, D_MODEL, D_FF), D_MODEL ** -0.5),
        'ffn_w_down': nrm((DEPTH, 2, D_FF, D_MODEL), D_FF ** -0.5),
        'ev_w_in': nrm((N_EVEN, D_MODEL, EVEN_IN), D_MODEL ** -0.5),
        'mla_q_norm': gain((N_EVEN, MLA_Q_RANK)),
        'mla_w_qb': nrm((N_EVEN, MLA_Q_RANK, MLA_HEADS * (MLA_NOPE + MLA_ROPE)), MLA_Q_RANK ** -0.5),
        'mla_kv_norm': gain((N_EVEN, MLA_KV_RANK)),
        'mla_w_kvb': nrm((N_EVEN, MLA_KV_RANK, MLA_HEADS * (MLA_NOPE + MLA_V)), MLA_KV_RANK ** -0.5),
        'mla_nope_norm': gain((N_EVEN, 2, MLA_NOPE)),
        'mla_rope_norm': gain((N_EVEN, 2, MLA_ROPE)),
        'gqa_qk_norm': gain((N_EVEN, 2, HEAD_DIM)),
        'ev_w_out': nrm((N_EVEN, EVEN_MIX, D_MODEL), EVEN_MIX ** -0.5),
        'od_w_in': nrm((N_ODD, D_MODEL, ODD_IN), D_MODEL ** -0.5),
        'ret_decay_logit': base_logit[None, None, :] + nrm((N_ODD, 2, RET_HEADS), 0.1),
        'ret_norm': gain((N_ODD, RET_HEADS * RET_DV)),
        'hy_conv_w': nrm((N_ODD, 3, (HY_ORDER + 1) * HY_CH), 3 ** -0.5),
        'hy_conv_b': nrm((N_ODD, (HY_ORDER + 1) * HY_CH), 0.02),
        'hy_filt_w1': nrm((N_ODD, HY_EMB, HY_FHID), HY_EMB ** -0.5),
        'hy_filt_b1': nrm((N_ODD, HY_FHID), 0.02),
        'hy_filt_w2': nrm((N_ODD, HY_FHID, HY_FHID), HY_FHID ** -0.5),
        'hy_filt_b2': nrm((N_ODD, HY_FHID), 0.02),
        'hy_filt_freq': gain((N_ODD, 2, HY_FHID)),
        'hy_filt_w3': nrm((N_ODD, HY_FHID, HY_ORDER * HY_CH), 0.1 * HY_FHID ** -0.5),
        'hy_decay': jnp.linspace(decay_lo, decay_hi, HY_ORDER * HY_CH, dtype=f32)[None] + nrm((N_ODD, HY_ORDER * HY_CH), 0.1),
        'hy_skip': nrm((N_ODD, HY_ORDER * HY_CH), 0.5),
        'od_w_out': nrm((N_ODD, ODD_MIX, D_MODEL), ODD_MIX ** -0.5),
    }


def reference(x_prompt, x_sample, cache_mla_ckv, cache_mla_krope, cache_gqa_k, cache_gqa_v, state_ret,
              c, c_ctx, mod_w, mod_b, ffn_w_gate, ffn_w_up, ffn_w_down,
              ev_w_in, mla_q_norm, mla_w_qb, mla_kv_norm, mla_w_kvb, mla_nope_norm, mla_rope_norm, gqa_qk_norm, ev_w_out,
              od_w_in, ret_decay_logit, ret_norm, hy_conv_w, hy_conv_b, hy_filt_w1, hy_filt_b1, hy_filt_w2, hy_filt_b2,
              hy_filt_freq, hy_filt_w3, hy_decay, hy_skip, od_w_out):
    xc, xs = x_prompt, x_sample
    new_ckv, new_krope, new_k, new_v, new_ret = [], [], [], [], []
    for l in range(DEPTH):
        m_ctx = (jax.nn.silu(c_ctx)[None, :] @ mod_w[l] + mod_b[l]).reshape(1, N_MOD, D_MODEL)
        m_lat = (jax.nn.silu(c) @ mod_w[l] + mod_b[l]).reshape(c.shape[0], N_MOD, D_MODEL)

        def ffn(x, mod, i, j):
            h, g = _modulate(x, mod, i)
            return x + 0.5 * g * _swiglu(h, ffn_w_gate[l, j], ffn_w_up[l, j], ffn_w_down[l, j])

        xc = ffn(xc, m_ctx, 0, 0)
        xs = ffn(xs, m_lat, 0, 0)
        hc, gc = _modulate(xc, m_ctx, 1)
        hs, gs = _modulate(xs, m_lat, 1)
        if l % 2 == 0:
            e = l // 2
            w = (ev_w_in[e], mla_q_norm[e], mla_w_qb[e], mla_kv_norm[e], mla_w_kvb[e],
                 mla_nope_norm[e], mla_rope_norm[e], gqa_qk_norm[e], ev_w_out[e])
            oc, st = _even_mixer(hc, *w, None)
            os_, _ = _even_mixer(hs, *w, (cache_mla_ckv[:, e], cache_mla_krope[:, e], cache_gqa_k[:, e], cache_gqa_v[:, e]))
            new_ckv.append(st[0])
            new_krope.append(st[1])
            new_k.append(st[2])
            new_v.append(st[3])
        else:
            o = l // 2
            w = (od_w_in[o], ret_decay_logit[o], ret_norm[o], hy_conv_w[o], hy_conv_b[o], hy_filt_w1[o], hy_filt_b1[o],
                 hy_filt_w2[o], hy_filt_b2[o], hy_filt_freq[o], hy_filt_w3[o], hy_decay[o], hy_skip[o], od_w_out[o])
            s_zero = jnp.zeros((hc.shape[0], 2, RET_HEADS, RET_DK, RET_DV), hc.dtype)
            oc, st = _odd_mixer(hc, *w, s_zero)
            os_, _ = _odd_mixer(hs, *w, state_ret[:, o])
            new_ret.append(st)
        xc = xc + gc * oc
        xs = xs + gs * os_
        xc = ffn(xc, m_ctx, 2, 1)
        xs = ffn(xs, m_lat, 2, 1)
    return (xc, xs, jnp.stack(new_ckv, axis=1), jnp.stack(new_krope, axis=1), jnp.stack(new_k, axis=1),
            jnp.stack(new_v, axis=1), jnp.stack(new_ret, axis=1))
```

```python
import functools
import math

import numpy as np
import jax
import jax.numpy as jnp
from jax import lax
from jax.experimental import pallas as pl
from jax.experimental.pallas import tpu as pltpu

F32 = jnp.float32
BF16 = jnp.bfloat16

D_MODEL = 2048
BATCH = 32
SEQ = 256
DEPTH = 2
DEC_BATCH = 2
DEC_SEQ = 1024
PAST_LEN = 256
GRID_W = 64
N_MOD = 9
D_FF = 5632
ROPE_THETA = 10000.0
EPS = 1e-6

MLA_HEADS = 8
MLA_Q_RANK = 512
MLA_KV_RANK = 512
MLA_NOPE = 128
MLA_ROPE = 64
MLA_V = 128
GQA_HEADS = 8
GQA_KV_HEADS = 2
GQA_GROUP = GQA_HEADS // GQA_KV_HEADS
HEAD_DIM = 128
MLA_QK_PAD = 256

RET_HEADS = 8
RET_DK = 128
RET_DV = 128
HY_CH = 1024
HY_ORDER = 2
HY_BANDS = 16
HY_FHID = 64
ODD_IN = 4 * RET_HEADS * RET_DK + (HY_ORDER + 1) * HY_CH

V7X_VMEM_BYTES = 64 * 1024 * 1024
LANES = 128
VMEM_LIMIT = V7X_VMEM_BYTES - 6 * 1024 * 1024

TM_FFN = 512
TF_FFN = 512
TM_PROJ = 512
TN_MOD = 1024
TN_ODD = 1024
TQ_ATT = 256
TC_HY = 256


def _cparams(*sem):
    return pltpu.CompilerParams(dimension_semantics=sem, vmem_limit_bytes=VMEM_LIMIT)


def _const_spec(shape):
    nd = len(shape)
    return pl.BlockSpec(shape, lambda *_: (0,) * nd)


def _silu(x):
    return x * jax.nn.sigmoid(x)


def _rms_rows(x, width=None):
    ss = jnp.sum(x * x, axis=-1, keepdims=True)
    n = x.shape[-1] if width is None else width
    return x * lax.rsqrt(ss * (1.0 / n) + EPS)


def _mod_kernel(c_ref, w_ref, b_ref, o_ref):
    s = _silu(c_ref[...]).astype(BF16)
    o_ref[0] = jnp.dot(s, w_ref[0].astype(BF16), preferred_element_type=F32) + b_ref[0]


def _mod_vectors(cvec, mod_w, mod_b):
    depth, d, n = mod_w.shape
    rows = cvec.shape[0]
    return pl.pallas_call(
        _mod_kernel,
        out_shape=jax.ShapeDtypeStruct((depth, rows, n), F32),
        grid=(depth, n // TN_MOD),
        in_specs=[
            pl.BlockSpec((rows, d), lambda l, j: (0, 0)),
            pl.BlockSpec((1, d, TN_MOD), lambda l, j: (l, 0, j)),
            pl.BlockSpec((1, 1, TN_MOD), lambda l, j: (l, 0, j)),
        ],
        out_specs=pl.BlockSpec((1, rows, TN_MOD), lambda l, j: (l, 0, j)),
        compiler_params=_cparams("parallel", "parallel"),
        name="mod_vectors",
    )(cvec, mod_w, mod_b.reshape(depth, 1, n))


def _mod_specs(sub, group_of_tile, extra_axes=0):
    def spec(k):
        if extra_axes:
            return pl.BlockSpec((1, 1, D_MODEL), lambda i, j: (group_of_tile(i) * N_MOD + 3 * sub + k, 0, 0))
        return pl.BlockSpec((1, 1, D_MODEL), lambda i: (group_of_tile(i) * N_MOD + 3 * sub + k, 0, 0))
    return [spec(0), spec(1), spec(2)]


def _modulated(x, sh_ref, sc_ref):
    return (_rms_rows(x) * (1.0 + sc_ref[0]) + sh_ref[0]).astype(BF16)


def _ffn_kernel(x_ref, sh_ref, sc_ref, gt_ref, wg_ref, wu_ref, wd_ref, o_ref, h_ref):
    f = pl.program_id(1)

    @pl.when(f == 0)
    def _():
        h_ref[...] = _modulated(x_ref[...], sh_ref, sc_ref)
        o_ref[...] = jnp.zeros_like(o_ref)

    h = h_ref[...]
    g = jnp.dot(h, wg_ref[...].astype(BF16), preferred_element_type=F32)
    u = jnp.dot(h, wu_ref[...].astype(BF16), preferred_element_type=F32)
    a = (_silu(g) * u).astype(BF16)
    o_ref[...] += jnp.dot(a, wd_ref[...].astype(BF16), preferred_element_type=F32)

    @pl.when(f == pl.num_programs(1) - 1)
    def _():
        o_ref[...] = x_ref[...] + 0.5 * gt_ref[0] * o_ref[...]


def _ffn(x, mod_tab, group_of_tile, sub, w_gate, w_up, w_down, layer, j):
    m, d = x.shape
    dff = w_gate.shape[-1]
    return pl.pallas_call(
        _ffn_kernel,
        out_shape=jax.ShapeDtypeStruct((m, d), F32),
        grid=(m // TM_FFN, dff // TF_FFN),
        in_specs=[pl.BlockSpec((TM_FFN, d), lambda i, f: (i, 0))]
        + _mod_specs(sub, group_of_tile, extra_axes=1)
        + [
            pl.BlockSpec((None, None, d, TF_FFN), lambda i, f: (layer, j, 0, f)),
            pl.BlockSpec((None, None, d, TF_FFN), lambda i, f: (layer, j, 0, f)),
            pl.BlockSpec((None, None, TF_FFN, d), lambda i, f: (layer, j, f, 0)),
        ],
        out_specs=pl.BlockSpec((TM_FFN, d), lambda i, f: (i, 0)),
        scratch_shapes=[pltpu.VMEM((TM_FFN, d), BF16)],
        compiler_params=_cparams("parallel", "arbitrary"),
        name="ffn",
    )(x, mod_tab, mod_tab, mod_tab, w_gate, w_up, w_down)


def _out_proj_kernel(*refs, n_parts):
    x_ref, gt_ref = refs[0], refs[1]
    part_refs = refs[2:2 + n_parts]
    w_refs = refs[2 + n_parts:2 + 2 * n_parts]
    o_ref = refs[2 + 2 * n_parts]
    acc = None
    for p_ref, w_ref in zip(part_refs, w_refs):
        t = jnp.dot(p_ref[...], w_ref[...], preferred_element_type=F32)
        acc = t if acc is None else acc + t
    o_ref[...] = x_ref[...] + gt_ref[0] * acc


def _out_proj(x, mod_tab, group_of_tile, parts, w_out_bf16):
    m, d = x.shape
    widths = [p.shape[1] for p in parts]
    offs = np.cumsum([0] + widths)
    w_parts = [w_out_bf16[int(offs[k]):int(offs[k + 1])] for k in range(len(parts))]
    return pl.pallas_call(
        functools.partial(_out_proj_kernel, n_parts=len(parts)),
        out_shape=jax.ShapeDtypeStruct((m, d), F32),
        grid=(m // TM_PROJ,),
        in_specs=[pl.BlockSpec((TM_PROJ, d), lambda i: (i, 0)), _mod_specs(1, group_of_tile)[2]]
        + [pl.BlockSpec((TM_PROJ, w), lambda i: (i, 0)) for w in widths]
        + [_const_spec(wp.shape) for wp in w_parts],
        out_specs=pl.BlockSpec((TM_PROJ, d), lambda i: (i, 0)),
        compiler_params=_cparams("parallel"),
        name="mixer_out_proj",
    )(x, mod_tab, *parts, *w_parts)


EV_QA, EV_KVA, EV_QG, EV_KG, EV_VG, EV_KR, EV_END = 0, 512, 1024, 2048, 2304, 2560, 2688


def _swap_halves_64(x, lane):
    return jnp.where(lane < 32, pltpu.roll(x, 96, axis=1), pltpu.roll(x, 32, axis=1))


def _even_proj_kernel(x_ref, sh_ref, sc_ref, w_in_ref, w_qb_ref, qn_ref, kvn_ref, ng_ref, rg_ref, gg_ref,
                      cm_ref, sm_ref, cg_ref, sg_ref,
                      qm_ref, ckv_ref, kr_ref, qg_ref, kg_ref, vg_ref):
    h = _modulated(x_ref[...], sh_ref, sc_ref)
    u = jnp.dot(h, w_in_ref[...], preferred_element_type=F32)
    rows = u.shape[0]
    lane = lax.broadcasted_iota(jnp.int32, (rows, LANES), 1)
    cos_m, sin_m, cos_g, sin_g = cm_ref[...], sm_ref[...], cg_ref[...], sg_ref[...]

    qa = (_rms_rows(u[:, EV_QA:EV_KVA]) * qn_ref[...]).astype(BF16)
    q = jnp.dot(qa, w_qb_ref[...], preferred_element_type=F32)
    for hd in range(MLA_HEADS):
        c0 = hd * MLA_QK_PAD
        nope = _rms_rows(q[:, c0:c0 + MLA_NOPE]) * ng_ref[0:1, :]
        rope = _rms_rows(q[:, c0 + MLA_NOPE:c0 + MLA_QK_PAD], MLA_ROPE) * rg_ref[0:1, :]
        rope = rope * cos_m + _swap_halves_64(rope, lane) * sin_m
        qm_ref[:, c0:c0 + MLA_NOPE] = nope.astype(BF16)
        qm_ref[:, c0 + MLA_NOPE:c0 + MLA_QK_PAD] = rope.astype(BF16)

    ckv_ref[...] = _rms_rows(u[:, EV_KVA:EV_QG]) * kvn_ref[...]
    kr = _rms_rows(u[:, EV_KR:EV_END], MLA_ROPE) * rg_ref[1:2, :]
    kr_ref[...] = kr * cos_m + _swap_halves_64(kr, lane) * sin_m

    for hd in range(GQA_HEADS):
        c0 = EV_QG + hd * HEAD_DIM
        qh = _rms_rows(u[:, c0:c0 + HEAD_DIM]) * gg_ref[0:1, :]
        qh = qh * cos_g + pltpu.roll(qh, HEAD_DIM // 2, axis=1) * sin_g
        qg_ref[:, hd * HEAD_DIM:(hd + 1) * HEAD_DIM] = qh.astype(BF16)
    for hd in range(GQA_KV_HEADS):
        c0 = EV_KG + hd * HEAD_DIM
        kh = _rms_rows(u[:, c0:c0 + HEAD_DIM]) * gg_ref[1:2, :]
        kg_ref[:, hd * HEAD_DIM:(hd + 1) * HEAD_DIM] = kh * cos_g + pltpu.roll(kh, HEAD_DIM // 2, axis=1) * sin_g
    vg_ref[...] = u[:, EV_VG:EV_KR]


def _even_proj(x, mod_tab, group_of_tile, w, tables, table_index):
    m, d = x.shape
    tm = TM_PROJ
    tab_spec = pl.BlockSpec((None, tm, LANES), table_index)
    row = lambda w_: pl.BlockSpec((tm, w_), lambda i: (i, 0))
    return pl.pallas_call(
        _even_proj_kernel,
        out_shape=[
            jax.ShapeDtypeStruct((m, MLA_HEADS * MLA_QK_PAD), BF16),
            jax.ShapeDtypeStruct((m, MLA_KV_RANK), F32),
            jax.ShapeDtypeStruct((m, LANES), F32),
            jax.ShapeDtypeStruct((m, GQA_HEADS * HEAD_DIM), BF16),
            jax.ShapeDtypeStruct((m, GQA_KV_HEADS * HEAD_DIM), F32),
            jax.ShapeDtypeStruct((m, GQA_KV_HEADS * HEAD_DIM), F32),
        ],
        grid=(m // tm,),
        in_specs=[row(d)] + _mod_specs(1, group_of_tile)[:2]
        + [_const_spec(w["w_in"].shape), _const_spec(w["w_qb"].shape), _const_spec(w["q_norm"].shape),
           _const_spec(w["kv_norm"].shape), _const_spec(w["nope_g"].shape), _const_spec(w["rope_g"].shape),
           _const_spec(w["gqa_g"].shape), tab_spec, tab_spec, tab_spec, tab_spec],
        out_specs=[row(MLA_HEADS * MLA_QK_PAD), row(MLA_KV_RANK), row(LANES), row(GQA_HEADS * HEAD_DIM),
                   row(GQA_KV_HEADS * HEAD_DIM), row(GQA_KV_HEADS * HEAD_DIM)],
        compiler_params=_cparams("parallel"),
        name="even_proj",
    )(x, mod_tab, mod_tab, w["w_in"], w["w_qb"], w["q_norm"], w["kv_norm"], w["nope_g"], w["rope_g"], w["gqa_g"],
      tables["cos_m"], tables["sin_m"], tables["cos_g"], tables["sin_g"])


def _kv_expand_kernel(ckv_ref, kr_ref, w_ref, ng_ref, km_ref, v_ref):
    kv = jnp.dot(ckv_ref[...].astype(BF16), w_ref[...], preferred_element_type=F32)
    kr = kr_ref[...].astype(BF16)
    for hd in range(MLA_HEADS):
        c0 = hd * (MLA_NOPE + MLA_V)
        km_ref[:, hd * MLA_QK_PAD:hd * MLA_QK_PAD + MLA_NOPE] = (
            _rms_rows(kv[:, c0:c0 + MLA_NOPE]) * ng_ref[1:2, :]).astype(BF16)
        km_ref[:, hd * MLA_QK_PAD + MLA_NOPE:(hd + 1) * MLA_QK_PAD] = kr
        v_ref[:, hd * MLA_V:(hd + 1) * MLA_V] = kv[:, c0 + MLA_NOPE:c0 + MLA_NOPE + MLA_V].astype(BF16)


def _kv_expand(ckv, kr, w_kvb_bf16, nope_g, tr):
    r = ckv.shape[0]
    row = lambda w_: pl.BlockSpec((tr, w_), lambda i: (i, 0))
    return pl.pallas_call(
        _kv_expand_kernel,
        out_shape=[jax.ShapeDtypeStruct((r, MLA_HEADS * MLA_QK_PAD), BF16),
                   jax.ShapeDtypeStruct((r, MLA_HEADS * MLA_V), BF16)],
        grid=(r // tr,),
        in_specs=[row(MLA_KV_RANK), row(LANES), _const_spec(w_kvb_bf16.shape), _const_spec(nope_g.shape)],
        out_specs=[row(MLA_HEADS * MLA_QK_PAD), row(MLA_HEADS * MLA_V)],
        compiler_params=_cparams("parallel"),
        name="kv_expand",
    )(ckv, kr, w_kvb_bf16, nope_g)


def _softmax_pv(score_list, value_list):
    mx = None
    for s in score_list:
        m_ = jnp.max(s, axis=-1, keepdims=True)
        mx = m_ if mx is None else jnp.maximum(mx, m_)
    den, acc = None, None
    for s, v in zip(score_list, value_list):
        p = jnp.exp(s - mx)
        l_ = jnp.sum(p, axis=-1, keepdims=True)
        o_ = jnp.dot(p.astype(BF16), v, preferred_element_type=F32)
        den = l_ if den is None else den + l_
        acc = o_ if acc is None else acc + o_
    return acc / den


def _qk(q, k):
    return lax.dot_general(q, k, (((1,), (1,)), ((), ())), preferred_element_type=F32)


def _attn_kernel(*refs, has_cache):
    if has_cache:
        qm_ref, qg_ref, km_ref, v_ref, kg_ref, vg_ref, kmc_ref, vc_ref, kgc_ref, vgc_ref, o_ref = refs
    else:
        qm_ref, qg_ref, km_ref, v_ref, kg_ref, vg_ref, o_ref = refs
    scale_m = (MLA_NOPE + MLA_ROPE) ** -0.5
    scale_g = HEAD_DIM ** -0.5
    for hd in range(MLA_HEADS):
        qs = slice(hd * MLA_QK_PAD, (hd + 1) * MLA_QK_PAD)
        vs = slice(hd * MLA_V, (hd + 1) * MLA_V)
        q = qm_ref[:, qs]
        scores = [_qk(q, km_ref[:, qs]) * scale_m]
        values = [v_ref[:, vs]]
        if has_cache:
            scores.append(_qk(q, kmc_ref[:, qs]) * scale_m)
            values.append(vc_ref[:, vs])
        o_ref[:, vs] = _softmax_pv(scores, values).astype(o_ref.dtype)
    base = MLA_HEADS * MLA_V
    for kvh in range(GQA_KV_HEADS):
        ks = slice(kvh * HEAD_DIM, (kvh + 1) * HEAD_DIM)
        k = kg_ref[:, ks].astype(BF16)
        v = vg_ref[:, ks].astype(BF16)
        if has_cache:
            kc = kgc_ref[:, ks].astype(BF16)
            vc = vgc_ref[:, ks].astype(BF16)
        for g in range(GQA_GROUP):
            hd = kvh * GQA_GROUP + g
            hs = slice(hd * HEAD_DIM, (hd + 1) * HEAD_DIM)
            q = qg_ref[:, hs]
            scores, values = [_qk(q, k) * scale_g], [v]
            if has_cache:
                scores.append(_qk(q, kc) * scale_g)
                values.append(vc)
            o_ref[:, base + hd * HEAD_DIM:base + (hd + 1) * HEAD_DIM] = _softmax_pv(scores, values).astype(o_ref.dtype)


def _attention(qm, qg, km, v, kg, vg, seq, tq, cache=None):
    m = qm.shape[0]
    nq = seq // tq
    qrow = lambda w_: pl.BlockSpec((tq, w_), lambda i: (i, 0))
    krow = lambda w_: pl.BlockSpec((seq, w_), lambda i: (i // nq, 0))
    in_specs = [qrow(qm.shape[1]), qrow(qg.shape[1]), krow(km.shape[1]), krow(v.shape[1]),
                krow(kg.shape[1]), krow(vg.shape[1])]
    args = [qm, qg, km, v, kg, vg]
    if cache is not None:
        past = cache[0].shape[0] // (m // seq)
        crow = lambda w_: pl.BlockSpec((past, w_), lambda i: (i // nq, 0))
        in_specs += [crow(c.shape[1]) for c in cache]
        args += list(cache)
    width = MLA_HEADS * MLA_V + GQA_HEADS * HEAD_DIM
    return pl.pallas_call(
        functools.partial(_attn_kernel, has_cache=cache is not None),
        out_shape=jax.ShapeDtypeStruct((m, width), BF16),
        grid=(m // tq,),
        in_specs=in_specs,
        out_specs=qrow(width),
        compiler_params=_cparams("parallel"),
        name="attention",
    )(*args)


def _odd_proj_kernel(x_ref, sh_ref, sc_ref, w_ref, o_ref, h_ref):
    @pl.when(pl.program_id(1) == 0)
    def _():
        h_ref[...] = _modulated(x_ref[...], sh_ref, sc_ref)

    o_ref[...] = jnp.dot(h_ref[...], w_ref[...], preferred_element_type=F32).astype(o_ref.dtype)


def _odd_proj(x, mod_tab, group_of_tile, w_in_bf16):
    m, d = x.shape
    n = w_in_bf16.shape[1]
    return pl.pallas_call(
        _odd_proj_kernel,
        out_shape=jax.ShapeDtypeStruct((m, n), BF16),
        grid=(m // TM_PROJ, n // TN_ODD),
        in_specs=[pl.BlockSpec((TM_PROJ, d), lambda i, j: (i, 0))]
        + _mod_specs(1, group_of_tile, extra_axes=1)[:2]
        + [pl.BlockSpec((d, TN_ODD), lambda i, j: (0, j))],
        out_specs=pl.BlockSpec((TM_PROJ, TN_ODD), lambda i, j: (i, j)),
        scratch_shapes=[pltpu.VMEM((TM_PROJ, d), BF16)],
        compiler_params=_cparams("parallel", "arbitrary"),
        name="odd_proj",
    )(x, mod_tab, mod_tab, w_in_bf16)


def _retention_kernel(*refs, heads, has_init):
    if has_init:
        logit_ref, q_ref, k_ref, v_ref, g_ref, rg_ref, s0_ref, o_ref, st_ref = refs
    else:
        logit_ref, q_ref, k_ref, v_ref, g_ref, rg_ref, o_ref, st_ref = refs
    length = q_ref.shape[0]
    h0 = pl.program_id(1) * heads if heads < RET_HEADS else 0
    scale = RET_DK ** -0.5
    n_i = lax.broadcasted_iota(jnp.int32, (length, length), 0)
    m_i = lax.broadcasted_iota(jnp.int32, (length, length), 1)
    diff = (n_i - m_i).astype(F32)
    pos = lax.broadcasted_iota(jnp.int32, (length, 1), 0).astype(F32)
    logit = logit_ref[...]
    log_g = jnp.minimum(logit, 0.0) - jnp.log1p(jnp.exp(-jnp.abs(logit)))
    lane_h = lax.broadcasted_iota(jnp.int32, (2, RET_HEADS), 1)
    for hh in range(heads):
        cs = slice(hh * RET_DK, (hh + 1) * RET_DK)
        sel = jnp.where(lane_h == h0 + hh, log_g, 0.0)
        lg = jnp.sum(sel, axis=1, keepdims=True)
        lg_f, lg_b = lg[0:1, :], lg[1:2, :]
        q, k, v = q_ref[:, cs], k_ref[:, cs], v_ref[:, cs]
        decay = (jnp.where(diff >= 0, jnp.exp(jnp.maximum(diff, 0.0) * lg_f), 0.0)
                 + jnp.where(diff <= 0, jnp.exp(jnp.maximum(-diff, 0.0) * lg_b), 0.0))
        att = _qk(q, k) * (decay * scale)
        o = jnp.dot(att.astype(BF16), v, preferred_element_type=F32)
        kf = k.astype(F32)
        k_f = (kf * (jnp.exp((length - 1.0 - pos) * lg_f) * scale)).astype(BF16)
        k_b = (kf * (jnp.exp(pos * lg_b) * scale)).astype(BF16)
        tn = (((0,), (0,)), ((), ()))
        s_f = lax.dot_general(k_f, v, tn, preferred_element_type=F32)
        s_b = lax.dot_general(k_b, v, tn, preferred_element_type=F32)
        if has_init:
            s0_f, s0_b = s0_ref[0, 0, hh], s0_ref[0, 1, hh]
            o = o + jnp.dot(q, s0_f.astype(BF16), preferred_element_type=F32) * jnp.exp((pos + 1.0) * lg_f)
            o = o + jnp.dot(q, s0_b.astype(BF16), preferred_element_type=F32) * jnp.exp((length - pos) * lg_b)
            s_f = s_f + s0_f * jnp.exp(length * lg_f)
            s_b = s_b + s0_b * jnp.exp(length * lg_b)
        st_ref[0, 0, hh] = s_f
        st_ref[0, 1, hh] = s_b
        gate = g_ref[:, cs].astype(F32)
        o_ref[:, cs] = (_rms_rows(o) * rg_ref[:, cs] * _silu(gate)).astype(o_ref.dtype)


def _retention(u, decay_logit, ret_g, seq, heads, s0=None):
    m = u.shape[0]
    b = m // seq
    hw = heads * RET_DK
    nh = RET_HEADS // heads
    blocks_per_part = RET_HEADS * RET_DK // hw
    col = lambda part: pl.BlockSpec((seq, hw), lambda i, j, part=part: (i, part * blocks_per_part + j))
    st_spec = pl.BlockSpec((1, 2, heads, RET_DK, RET_DV), lambda i, j: (i, 0, j, 0, 0))
    in_specs = [_const_spec(decay_logit.shape), col(0), col(1), col(2), col(3),
                pl.BlockSpec((1, hw), lambda i, j: (0, j))]
    args = [decay_logit, u, u, u, u, ret_g]
    if s0 is not None:
        in_specs.append(st_spec)
        args.append(s0)
    return pl.pallas_call(
        functools.partial(_retention_kernel, heads=heads, has_init=s0 is not None),
        out_shape=[jax.ShapeDtypeStruct((m, RET_HEADS * RET_DV), BF16),
                   jax.ShapeDtypeStruct((b, 2, RET_HEADS, RET_DK, RET_DV), F32)],
        grid=(b, nh),
        in_specs=in_specs,
        out_specs=[pl.BlockSpec((seq, hw), lambda i, j: (i, j)), st_spec],
        compiler_params=_cparams("parallel", "parallel"),
        name="retention",
    )(*args)


@functools.lru_cache(maxsize=None)
def _dft_matrices(length):
    n = 2 * length
    f = np.arange(length, dtype=np.float64)[:, None]
    t = np.arange(length, dtype=np.float64)[None, :]
    ang = 2.0 * np.pi * f * t / n
    fwd_a = np.cos(ang)
    fwd_b = np.sin(ang)
    fwd_b[0, :] = np.cos(np.pi * t[0])
    fwd = np.concatenate([fwd_a, fwd_b], axis=0)
    tt = (np.arange(length, dtype=np.float64) + length // 2)[:, None]
    ff = np.arange(length, dtype=np.float64)[None, :]
    ang_i = 2.0 * np.pi * ff * tt / n
    inv_a = 2.0 * np.cos(ang_i) / n
    inv_b = 2.0 * np.sin(ang_i) / n
    inv_a[:, 0] = 1.0 / n
    inv_b[:, 0] = np.cos(np.pi * tt[:, 0]) / n
    inv = np.concatenate([inv_a, inv_b], axis=1)
    return fwd.astype(np.float32), inv.astype(np.float32)


@functools.lru_cache(maxsize=None)
def _filter_features(length):
    t = np.arange(length, dtype=np.float64)
    tn = t / length
    bands = np.arange(1, HY_BANDS + 1, dtype=np.float64)
    ang = 2.0 * np.pi * tn[:, None] * bands[None, :]
    feat = np.concatenate([tn[:, None], np.sin(ang), np.cos(ang)], axis=-1)
    feat = np.pad(feat, ((0, 0), (0, LANES - feat.shape[1])))
    r = (np.abs(t - length // 2) / (length / 2))[:, None]
    return feat.astype(np.float32), r.astype(np.float32)


def _filter_spec_kernel(feat_ref, r_ref, w1_ref, b1_ref, w2_ref, b2_ref, fr_ref, w3_ref, dec_ref, fwd_ref, o_ref):
    hp = lax.Precision.HIGHEST
    z = jnp.sin(fr_ref[0:1, :] * (jnp.dot(feat_ref[...], w1_ref[...], precision=hp, preferred_element_type=F32)
                                  + b1_ref[...]))
    z = jnp.sin(fr_ref[1:2, :] * (jnp.dot(z, w2_ref[...], precision=hp, preferred_element_type=F32) + b2_ref[...]))
    filt = jnp.dot(z, w3_ref[...], precision=hp, preferred_element_type=F32)
    filt = filt * jnp.exp(-r_ref[...] * jnp.abs(dec_ref[...]))
    o_ref[...] = jnp.dot(fwd_ref[...], filt.astype(BF16), preferred_element_type=F32)


def _filter_spectra(length, w1, b1, w2, b2, freq, w3, decay, fwd_bf16, tc):
    feat, r = _filter_features(length)
    nch = w3.shape[1]
    w1p = jnp.pad(w1, ((0, LANES - w1.shape[0]), (0, 0)))
    return pl.pallas_call(
        _filter_spec_kernel,
        out_shape=jax.ShapeDtypeStruct((2 * length, nch), F32),
        grid=(nch // tc,),
        in_specs=[_const_spec(feat.shape), _const_spec(r.shape), _const_spec(w1p.shape), _const_spec((1, HY_FHID)),
                  _const_spec(w2.shape), _const_spec((1, HY_FHID)), _const_spec(freq.shape),
                  pl.BlockSpec((HY_FHID, tc), lambda j: (0, j)), pl.BlockSpec((1, tc), lambda j: (0, j)),
                  _const_spec(fwd_bf16.shape)],
        out_specs=pl.BlockSpec((2 * length, tc), lambda j: (0, j)),
        compiler_params=_cparams("parallel"),
        name="hyena_filter_spectra",
    )(jnp.asarray(feat), jnp.asarray(r), w1p, b1.reshape(1, -1), w2, b2.reshape(1, -1), freq, w3,
      decay.reshape(1, -1), fwd_bf16)


def _hyena_kernel(v_ref, x1_ref, x2_ref, cwv_ref, cw1_ref, cw2_ref, cbv_ref, cb1_ref, cb2_ref,
                  h1_ref, h2_ref, sk1_ref, sk2_ref, fwd_ref, inv_ref, o_ref):
    length, tc = v_ref.shape
    row = lax.broadcasted_iota(jnp.int32, (length, tc), 0)
    first, last = row == 0, row == length - 1

    def short_conv(x_ref, w_ref, b_ref):
        x = x_ref[...].astype(F32)
        prev = jnp.where(first, 0.0, pltpu.roll(x, 1, axis=0))
        nxt = jnp.where(last, 0.0, pltpu.roll(x, length - 1, axis=0))
        return prev * w_ref[0:1, :] + x * w_ref[1:2, :] + nxt * w_ref[2:3, :] + b_ref[...]

    def long_conv(z, h_ref):
        zs = jnp.dot(fwd_ref[...], z.astype(BF16), preferred_element_type=F32)
        za, zb = zs[:length], zs[length:]
        ha, hb = h_ref[:length, :], h_ref[length:, :]
        bb = zb * hb
        ya = za * ha - jnp.where(first, 0.0, bb)
        yb = jnp.where(first, bb, za * hb + zb * ha)
        y = jnp.concatenate([ya, yb], axis=0).astype(BF16)
        return jnp.dot(inv_ref[...], y, preferred_element_type=F32)

    v = short_conv(v_ref, cwv_ref, cbv_ref)
    x1 = short_conv(x1_ref, cw1_ref, cb1_ref)
    x2 = short_conv(x2_ref, cw2_ref, cb2_ref)
    z = x1 * (long_conv(v, h1_ref) + sk1_ref[...] * v)
    o_ref[...] = (x2 * (long_conv(z, h2_ref) + sk2_ref[...] * z)).astype(o_ref.dtype)


def _hyena(u, seq, tc, conv_w, conv_b, spectra, skip, fwd_bf16, inv_bf16):
    m = u.shape[0]
    b = m // seq
    nc = HY_CH // tc
    hy0 = 4 * RET_HEADS * RET_DK // tc
    ucol = lambda part: pl.BlockSpec((seq, tc), lambda i, j, part=part: (i, hy0 + part * nc + j))
    wcol = lambda rows, part: pl.BlockSpec((rows, tc), lambda i, j, part=part: (0, part * nc + j))
    return pl.pallas_call(
        _hyena_kernel,
        out_shape=jax.ShapeDtypeStruct((m, HY_CH), BF16),
        grid=(b, nc),
        in_specs=[ucol(0), ucol(1), ucol(2), wcol(3, 0), wcol(3, 1), wcol(3, 2), wcol(1, 0), wcol(1, 1), wcol(1, 2),
                  wcol(2 * seq, 0), wcol(2 * seq, 1), wcol(1, 0), wcol(1, 1),
                  _const_spec(fwd_bf16.shape), _const_spec(inv_bf16.shape)],
        out_specs=pl.BlockSpec((seq, tc), lambda i, j: (i, j)),
        compiler_params=_cparams("parallel", "parallel"),
        name="hyena",
    )(u, u, u, conv_w, conv_w, conv_w, conv_b, conv_b, conv_b, spectra, spectra, skip, skip, fwd_bf16, inv_bf16)


@functools.lru_cache(maxsize=None)
def _rope_tables(length):
    rows = np.repeat(np.arange(length // GRID_W, dtype=np.float64), GRID_W)
    cols = np.tile(np.arange(GRID_W, dtype=np.float64), length // GRID_W)

    def angles(dim):
        half = dim // 2
        freq = ROPE_THETA ** (-np.arange(0, half, 2, dtype=np.float64) / half)
        return np.concatenate([rows[:, None] * freq[None], cols[:, None] * freq[None]], axis=-1)

    def pack(dim):
        ang = angles(dim)
        cos = np.concatenate([np.cos(ang), np.cos(ang)], axis=-1)
        sin = np.concatenate([-np.sin(ang), np.sin(ang)], axis=-1)
        pad = LANES - dim
        cos = np.pad(cos, ((0, 0), (0, pad)), constant_values=1.0)
        sin = np.pad(sin, ((0, 0), (0, pad)))
        ident_c, ident_s = np.ones_like(cos), np.zeros_like(sin)
        return (np.stack([ident_c, cos]).astype(np.float32), np.stack([ident_s, sin]).astype(np.float32))

    cos_m, sin_m = pack(MLA_ROPE)
    cos_g, sin_g = pack(HEAD_DIM)
    return cos_m, sin_m, cos_g, sin_g


def _even_weights(w_in, q_norm, w_qb, kv_norm, w_kvb, nope_g, rope_g, gqa_g, w_out):
    d = w_in.shape[0]
    q_a, kv_a, kr, qg, kg, vg = jnp.split(w_in, np.cumsum([512, 512, 64, 1024, 256])[:].tolist(), axis=1)
    w_in_p = jnp.concatenate([q_a, kv_a, qg, kg, vg, kr, jnp.zeros((d, LANES - MLA_ROPE), w_in.dtype)], axis=1)
    wq = w_qb.reshape(MLA_Q_RANK, MLA_HEADS, MLA_NOPE + MLA_ROPE)
    wq = jnp.pad(wq, ((0, 0), (0, 0), (0, MLA_QK_PAD - MLA_NOPE - MLA_ROPE)))
    return {
        "w_in": w_in_p.astype(BF16),
        "w_qb": wq.reshape(MLA_Q_RANK, MLA_HEADS * MLA_QK_PAD).astype(BF16),
        "w_kvb": w_kvb.astype(BF16),
        "w_out": w_out.astype(BF16),
        "q_norm": q_norm.reshape(1, -1),
        "kv_norm": kv_norm.reshape(1, -1),
        "nope_g": nope_g,
        "rope_g": jnp.pad(rope_g, ((0, 0), (0, LANES - MLA_ROPE))),
        "gqa_g": gqa_g,
    }


def _ctx_group(i):
    return 0


def _make_sample_group(tile):
    per_seq = DEC_SEQ // tile
    return lambda i: 1 + i // per_seq


def kernel(x_prompt, x_sample, cache_mla_ckv, cache_mla_krope, cache_gqa_k, cache_gqa_v, state_ret, c, c_ctx, mod_w, mod_b, ffn_w_gate, ffn_w_up, ffn_w_down, ev_w_in, mla_q_norm, mla_w_qb, mla_kv_norm, mla_w_kvb, mla_nope_norm, mla_rope_norm, gqa_qk_norm, ev_w_out, od_w_in, ret_decay_logit, ret_norm, hy_conv_w, hy_conv_b, hy_filt_w1, hy_filt_b1, hy_filt_w2, hy_filt_b2, hy_filt_freq, hy_filt_w3, hy_decay, hy_skip, od_w_out):
    xc = x_prompt.reshape(BATCH * SEQ, D_MODEL)
    xs = x_sample.reshape(DEC_BATCH * DEC_SEQ, D_MODEL)

    cvec = jnp.concatenate([c_ctx[None, :], c, jnp.zeros((8 - 1 - DEC_BATCH, D_MODEL), F32)], axis=0)
    mod_all = _mod_vectors(cvec, mod_w, mod_b)

    cos_m, sin_m, cos_g, sin_g = _rope_tables(DEC_SEQ)
    tables = {"cos_m": jnp.asarray(cos_m), "sin_m": jnp.asarray(sin_m),
              "cos_g": jnp.asarray(cos_g), "sin_g": jnp.asarray(sin_g)}
    proj_tiles_per_seq = DEC_SEQ // TM_PROJ
    ctx_tab_index = lambda i: (0, 0, 0)
    smp_tab_index = lambda i: (1, i % proj_tiles_per_seq, 0)
    g_ffn_s = _make_sample_group(TM_FFN)
    g_proj_s = _make_sample_group(TM_PROJ)

    outs = {}
    for l in range(DEPTH):
        mod_tab = mod_all[l, :1 + DEC_BATCH].reshape((1 + DEC_BATCH) * N_MOD, 1, D_MODEL)
        xc = _ffn(xc, mod_tab, _ctx_group, 0, ffn_w_gate, ffn_w_up, ffn_w_down, l, 0)
        xs = _ffn(xs, mod_tab, g_ffn_s, 0, ffn_w_gate, ffn_w_up, ffn_w_down, l, 0)
        if l % 2 == 0:
            e = l // 2
            w = _even_weights(ev_w_in[e], mla_q_norm[e], mla_w_qb[e], mla_kv_norm[e], mla_w_kvb[e],
                              mla_nope_norm[e], mla_rope_norm[e], gqa_qk_norm[e], ev_w_out[e])
            qm, ckv, kr, qg, kg, vg = _even_proj(xc, mod_tab, _ctx_group, w, tables, ctx_tab_index)
            km, v = _kv_expand(ckv, kr, w["w_kvb"], w["nope_g"], TM_PROJ)
            att = _attention(qm, qg, km, v, kg, vg, SEQ, SEQ)
            xc = _out_proj(xc, mod_tab, _ctx_group, [att], w["w_out"])
            outs["ckv"], outs["krope"], outs["k"], outs["v"] = ckv, kr[:, :MLA_ROPE], kg, vg
            qm, ckv, kr, qg, kg, vg = _even_proj(xs, mod_tab, g_proj_s, w, tables, smp_tab_index)
            km, v = _kv_expand(ckv, kr, w["w_kvb"], w["nope_g"], TM_PROJ)
            c_ckv = cache_mla_ckv[:, e].reshape(DEC_BATCH * PAST_LEN, MLA_KV_RANK)
            c_kr = jnp.pad(cache_mla_krope[:, e].reshape(DEC_BATCH * PAST_LEN, MLA_ROPE),
                           ((0, 0), (0, LANES - MLA_ROPE)))
            kmc, vc = _kv_expand(c_ckv, c_kr, w["w_kvb"], w["nope_g"], PAST_LEN)
            kgc = cache_gqa_k[:, e].reshape(DEC_BATCH * PAST_LEN, GQA_KV_HEADS * HEAD_DIM)
            vgc = cache_gqa_v[:, e].reshape(DEC_BATCH * PAST_LEN, GQA_KV_HEADS * HEAD_DIM)
            att = _attention(qm, qg, km, v, kg, vg, DEC_SEQ, TQ_ATT, cache=(kmc, vc, kgc, vgc))
            xs = _out_proj(xs, mod_tab, g_proj_s, [att], w["w_out"])
        else:
            o = l // 2
            w_in = od_w_in[o].astype(BF16)
            w_out = od_w_out[o].astype(BF16)
            logit = ret_decay_logit[o]
            ret_g = ret_norm[o].reshape(1, -1)
            skip = hy_skip[o].reshape(1, -1)
            conv_b = hy_conv_b[o].reshape(1, -1)
            filt_args = (hy_filt_w1[o], hy_filt_b1[o], hy_filt_w2[o], hy_filt_b2[o], hy_filt_freq[o],
                         hy_filt_w3[o], hy_decay[o])
            for stream in ("ctx", "smp"):
                x, seq, group = (xc, SEQ, _ctx_group) if stream == "ctx" else (xs, DEC_SEQ, g_proj_s)
                fwd, inv = _dft_matrices(seq)
                fwd_b, inv_b = jnp.asarray(fwd).astype(BF16), jnp.asarray(inv).astype(BF16)
                spectra = _filter_spectra(seq, *filt_args, fwd_b, TC_HY)
                u = _odd_proj(x, mod_tab, group, w_in)
                if stream == "ctx":
                    o_ret, st = _retention(u, logit, ret_g, seq, RET_HEADS)
                    outs["ret"] = st
                    o_hy = _hyena(u, seq, HY_CH, hy_conv_w[o], conv_b, spectra, skip, fwd_b, inv_b)
                else:
                    o_ret, _ = _retention(u, logit, ret_g, seq, 1, s0=state_ret[:, o])
                    o_hy = _hyena(u, seq, TC_HY, hy_conv_w[o], conv_b, spectra, skip, fwd_b, inv_b)
                x = _out_proj(x, mod_tab, group, [o_ret, o_hy], w_out)
                if stream == "ctx":
                    xc = x
                else:
                    xs = x
        xc = _ffn(xc, mod_tab, _ctx_group, 2, ffn_w_gate, ffn_w_up, ffn_w_down, l, 1)
        xs = _ffn(xs, mod_tab, g_ffn_s, 2, ffn_w_gate, ffn_w_up, ffn_w_down, l, 1)

    return (
        xc.reshape(BATCH, SEQ, D_MODEL),
        xs.reshape(DEC_BATCH, DEC_SEQ, D_MODEL),
        outs["ckv"].reshape(BATCH, 1, SEQ, MLA_KV_RANK),
        outs["krope"].reshape(BATCH, 1, SEQ, MLA_ROPE),
        outs["k"].reshape(BATCH, 1, SEQ, GQA_KV_HEADS, HEAD_DIM),
        outs["v"].reshape(BATCH, 1, SEQ, GQA_KV_HEADS, HEAD_DIM),
        outs["ret"].reshape(BATCH, 1, 2, RET_HEADS, RET_DK, RET_DV),
    )
```

```python
import functools
import math

import numpy as np
import jax
import jax.numpy as jnp
from jax import lax
from jax.experimental import pallas as pl
from jax.experimental.pallas import tpu as pltpu

F32 = jnp.float32
BF16 = jnp.bfloat16

D_MODEL = 2048
BATCH = 32
SEQ = 256
DEPTH = 2
DEC_BATCH = 2
DEC_SEQ = 1024
PAST_LEN = 256
GRID_W = 64
N_MOD = 9
D_FF = 5632
ROPE_THETA = 10000.0
EPS = 1e-6

MLA_HEADS = 8
MLA_Q_RANK = 512
MLA_KV_RANK = 512
MLA_NOPE = 128
MLA_ROPE = 64
MLA_V = 128
GQA_HEADS = 8
GQA_KV_HEADS = 2
GQA_GROUP = GQA_HEADS // GQA_KV_HEADS
HEAD_DIM = 128
MLA_QK_PAD = 256

RET_HEADS = 8
RET_DK = 128
RET_DV = 128
HY_CH = 1024
HY_ORDER = 2
HY_BANDS = 16
HY_FHID = 64
ODD_IN = 4 * RET_HEADS * RET_DK + (HY_ORDER + 1) * HY_CH

V7X_VMEM_BYTES = 64 * 1024 * 1024
LANES = 128
VMEM_LIMIT = V7X_VMEM_BYTES - 6 * 1024 * 1024

TM_FFN = 1024
TF_FFN = 256
TM_PROJ = 512
TM_ODD = 1024
TN_MOD = 1024
TN_ODD = 1024
TQ_ATT = 256
TC_HY = 256


def _cparams(*sem):
    return pltpu.CompilerParams(dimension_semantics=sem, vmem_limit_bytes=VMEM_LIMIT)


def _const_spec(shape):
    nd = len(shape)
    return pl.BlockSpec(shape, lambda *_: (0,) * nd)


def _silu(x):
    return x * jax.nn.sigmoid(x)


def _rms_rows(x, width=None):
    ss = jnp.sum(x * x, axis=-1, keepdims=True)
    n = x.shape[-1] if width is None else width
    return x * lax.rsqrt(ss * (1.0 / n) + EPS)


def _mod_kernel(c_ref, w_ref, b_ref, o_ref):
    s = _silu(c_ref[...]).astype(BF16)
    o_ref[0] = jnp.dot(s, w_ref[0].astype(BF16), preferred_element_type=F32) + b_ref[0]


def _mod_vectors(cvec, mod_w, mod_b):
    depth, d, n = mod_w.shape
    rows = cvec.shape[0]
    return pl.pallas_call(
        _mod_kernel,
        out_shape=jax.ShapeDtypeStruct((depth, rows, n), F32),
        grid=(depth, n // TN_MOD),
        in_specs=[
            pl.BlockSpec((rows, d), lambda l, j: (0, 0)),
            pl.BlockSpec((1, d, TN_MOD), lambda l, j: (l, 0, j)),
            pl.BlockSpec((1, 1, TN_MOD), lambda l, j: (l, 0, j)),
        ],
        out_specs=pl.BlockSpec((1, rows, TN_MOD), lambda l, j: (l, 0, j)),
        compiler_params=_cparams("parallel", "parallel"),
        name="mod_vectors",
    )(cvec, mod_w, mod_b.reshape(depth, 1, n))


def _mod_specs(sub, latent, tile, extra_axes=0):
    group_of_tile = (lambda i: 1 + (i * tile) // DEC_SEQ) if latent else (lambda i: 0)

    def spec(k):
        if extra_axes:
            return pl.BlockSpec((1, 1, D_MODEL), lambda i, j: (group_of_tile(i) * N_MOD + 3 * sub + k, 0, 0))
        return pl.BlockSpec((1, 1, D_MODEL), lambda i: (group_of_tile(i) * N_MOD + 3 * sub + k, 0, 0))
    return [spec(0), spec(1), spec(2)]


def _modulated(x, sh_ref, sc_ref):
    return (_rms_rows(x) * (1.0 + sc_ref[0]) + sh_ref[0]).astype(BF16)


def _ffn_kernel(x_ref, sh_ref, sc_ref, gt_ref, wg_ref, wu_ref, wd_ref, o_ref, h_ref):
    f = pl.program_id(1)

    @pl.when(f == 0)
    def _():
        h_ref[...] = _modulated(x_ref[...], sh_ref, sc_ref)
        o_ref[...] = jnp.zeros_like(o_ref)

    h = h_ref[...]
    g = jnp.dot(h, wg_ref[...].astype(BF16), preferred_element_type=F32)
    u = jnp.dot(h, wu_ref[...].astype(BF16), preferred_element_type=F32)
    a = (_silu(g) * u).astype(BF16)
    o_ref[...] += jnp.dot(a, wd_ref[...].astype(BF16), preferred_element_type=F32)

    @pl.when(f == pl.num_programs(1) - 1)
    def _():
        o_ref[...] = x_ref[...] + 0.5 * gt_ref[0] * o_ref[...]


def _ffn(x, mod_tab, latent, sub, w_gate, w_up, w_down, layer, j):
    m, d = x.shape
    dff = w_gate.shape[-1]
    return pl.pallas_call(
        _ffn_kernel,
        out_shape=jax.ShapeDtypeStruct((m, d), F32),
        grid=(m // TM_FFN, dff // TF_FFN),
        in_specs=[pl.BlockSpec((TM_FFN, d), lambda i, f: (i, 0))]
        + _mod_specs(sub, latent, TM_FFN, extra_axes=1)
        + [
            pl.BlockSpec((None, None, d, TF_FFN), lambda i, f: (layer, j, 0, f)),
            pl.BlockSpec((None, None, d, TF_FFN), lambda i, f: (layer, j, 0, f)),
            pl.BlockSpec((None, None, TF_FFN, d), lambda i, f: (layer, j, f, 0)),
        ],
        out_specs=pl.BlockSpec((TM_FFN, d), lambda i, f: (i, 0)),
        scratch_shapes=[pltpu.VMEM((TM_FFN, d), BF16)],
        compiler_params=_cparams("parallel", "arbitrary"),
        name="ffn",
    )(x, mod_tab, mod_tab, mod_tab, w_gate, w_up, w_down)


def _out_proj_kernel(*refs, n_parts):
    x_ref, gt_ref = refs[0], refs[1]
    part_refs = refs[2:2 + n_parts]
    w_refs = refs[2 + n_parts:2 + 2 * n_parts]
    o_ref = refs[2 + 2 * n_parts]
    acc = None
    for p_ref, w_ref in zip(part_refs, w_refs):
        t = jnp.dot(p_ref[...], w_ref[...], preferred_element_type=F32)
        acc = t if acc is None else acc + t
    o_ref[...] = x_ref[...] + gt_ref[0] * acc


def _out_proj(x, mod_tab, latent, parts, w_out_bf16):
    m, d = x.shape
    widths = [p.shape[1] for p in parts]
    offs = np.cumsum([0] + widths)
    w_parts = [w_out_bf16[int(offs[k]):int(offs[k + 1])] for k in range(len(parts))]
    return pl.pallas_call(
        functools.partial(_out_proj_kernel, n_parts=len(parts)),
        out_shape=jax.ShapeDtypeStruct((m, d), F32),
        grid=(m // TM_PROJ,),
        in_specs=[pl.BlockSpec((TM_PROJ, d), lambda i: (i, 0)), _mod_specs(1, latent, TM_PROJ)[2]]
        + [pl.BlockSpec((TM_PROJ, w), lambda i: (i, 0)) for w in widths]
        + [_const_spec(wp.shape) for wp in w_parts],
        out_specs=pl.BlockSpec((TM_PROJ, d), lambda i: (i, 0)),
        compiler_params=_cparams("parallel"),
        name="mixer_out_proj",
    )(x, mod_tab, *parts, *w_parts)


EV_QA, EV_KVA, EV_QG, EV_KG, EV_VG, EV_KR, EV_END = 0, 512, 1024, 2048, 2304, 2560, 2688


def _rotate_half(x, cos, sin):
    return x * cos + pltpu.roll(x, LANES // 2, axis=1) * sin


def _even_proj_kernel(*refs, rotary):
    (x_ref, sh_ref, sc_ref, w_in_ref, w_qb_ref, qn_ref, kvn_ref, ng_ref, rg_ref, gg_ref) = refs[:10]
    if rotary:
        cm_ref, sm_ref, cg_ref, sg_ref = refs[10:14]
        rot_m = lambda t: _rotate_half(t, cm_ref[...], sm_ref[...])
        rot_g = lambda t: _rotate_half(t, cg_ref[...], sg_ref[...])
    else:
        rot_m = rot_g = lambda t: t
    qm_ref, ckv_ref, kr_ref, qg_ref, kg_ref, vg_ref = refs[-6:]

    h = _modulated(x_ref[...], sh_ref, sc_ref)
    u = jnp.dot(h, w_in_ref[...], preferred_element_type=F32)

    qa = (_rms_rows(u[:, EV_QA:EV_KVA]) * qn_ref[...]).astype(BF16)
    q = jnp.dot(qa, w_qb_ref[...], preferred_element_type=F32)
    for hd in range(MLA_HEADS):
        c0 = hd * MLA_QK_PAD
        nope = _rms_rows(q[:, c0:c0 + MLA_NOPE]) * ng_ref[0:1, :]
        rope = _rms_rows(q[:, c0 + MLA_NOPE:c0 + MLA_QK_PAD], MLA_ROPE) * rg_ref[0:1, :]
        qm_ref[:, c0:c0 + MLA_NOPE] = nope.astype(BF16)
        qm_ref[:, c0 + MLA_NOPE:c0 + MLA_QK_PAD] = rot_m(rope).astype(BF16)

    ckv_ref[...] = _rms_rows(u[:, EV_KVA:EV_QG]) * kvn_ref[...]
    kr_ref[...] = rot_m(_rms_rows(u[:, EV_KR:EV_END], MLA_ROPE) * rg_ref[1:2, :])

    for hd in range(GQA_HEADS):
        c0 = EV_QG + hd * HEAD_DIM
        qh = _rms_rows(u[:, c0:c0 + HEAD_DIM]) * gg_ref[0:1, :]
        qg_ref[:, hd * HEAD_DIM:(hd + 1) * HEAD_DIM] = rot_g(qh).astype(BF16)
    for hd in range(GQA_KV_HEADS):
        c0 = EV_KG + hd * HEAD_DIM
        kh = _rms_rows(u[:, c0:c0 + HEAD_DIM]) * gg_ref[1:2, :]
        kg_ref[:, hd * HEAD_DIM:(hd + 1) * HEAD_DIM] = rot_g(kh)
    vg_ref[...] = u[:, EV_VG:EV_KR]


def _even_proj(x, mod_tab, w, tables=None):
    m, d = x.shape
    tm = TM_PROJ
    row = lambda w_: pl.BlockSpec((tm, w_), lambda i: (i, 0))
    tab_specs, tab_args = [], []
    if tables is not None:
        tiles_per_seq = tables[0].shape[0] // tm
        tab_specs = [pl.BlockSpec((tm, LANES), lambda i: (i % tiles_per_seq, 0))] * 4
        tab_args = list(tables)
    return pl.pallas_call(
        functools.partial(_even_proj_kernel, rotary=tables is not None),
        out_shape=[
            jax.ShapeDtypeStruct((m, MLA_HEADS * MLA_QK_PAD), BF16),
            jax.ShapeDtypeStruct((m, MLA_KV_RANK), F32),
            jax.ShapeDtypeStruct((m, LANES), F32),
            jax.ShapeDtypeStruct((m, GQA_HEADS * HEAD_DIM), BF16),
            jax.ShapeDtypeStruct((m, GQA_KV_HEADS * HEAD_DIM), F32),
            jax.ShapeDtypeStruct((m, GQA_KV_HEADS * HEAD_DIM), F32),
        ],
        grid=(m // tm,),
        in_specs=[row(d)] + _mod_specs(1, tables is not None, tm)[:2]
        + [_const_spec(w["w_in"].shape), _const_spec(w["w_qb"].shape), _const_spec(w["q_norm"].shape),
           _const_spec(w["kv_norm"].shape), _const_spec(w["nope_g"].shape), _const_spec(w["rope_g"].shape),
           _const_spec(w["gqa_g"].shape)] + tab_specs,
        out_specs=[row(MLA_HEADS * MLA_QK_PAD), row(MLA_KV_RANK), row(LANES), row(GQA_HEADS * HEAD_DIM),
                   row(GQA_KV_HEADS * HEAD_DIM), row(GQA_KV_HEADS * HEAD_DIM)],
        compiler_params=_cparams("parallel"),
        name="even_proj",
    )(x, mod_tab, mod_tab, w["w_in"], w["w_qb"], w["q_norm"], w["kv_norm"], w["nope_g"], w["rope_g"], w["gqa_g"],
      *tab_args)


def _kv_expand_kernel(ckv_ref, kr_ref, w_ref, ng_ref, km_ref, v_ref):
    kv = jnp.dot(ckv_ref[...].astype(BF16), w_ref[...], preferred_element_type=F32)
    kr = kr_ref[...].astype(BF16)
    for hd in range(MLA_HEADS):
        c0 = hd * (MLA_NOPE + MLA_V)
        km_ref[:, hd * MLA_QK_PAD:hd * MLA_QK_PAD + MLA_NOPE] = (
            _rms_rows(kv[:, c0:c0 + MLA_NOPE]) * ng_ref[1:2, :]).astype(BF16)
        km_ref[:, hd * MLA_QK_PAD + MLA_NOPE:(hd + 1) * MLA_QK_PAD] = kr
        v_ref[:, hd * MLA_V:(hd + 1) * MLA_V] = kv[:, c0 + MLA_NOPE:c0 + MLA_NOPE + MLA_V].astype(BF16)


def _kv_expand(ckv, kr, w_kvb_bf16, nope_g, tr):
    r = ckv.shape[0]
    row = lambda w_: pl.BlockSpec((tr, w_), lambda i: (i, 0))
    return pl.pallas_call(
        _kv_expand_kernel,
        out_shape=[jax.ShapeDtypeStruct((r, MLA_HEADS * MLA_QK_PAD), BF16),
                   jax.ShapeDtypeStruct((r, MLA_HEADS * MLA_V), BF16)],
        grid=(r // tr,),
        in_specs=[row(MLA_KV_RANK), row(LANES), _const_spec(w_kvb_bf16.shape), _const_spec(nope_g.shape)],
        out_specs=[row(MLA_HEADS * MLA_QK_PAD), row(MLA_HEADS * MLA_V)],
        compiler_params=_cparams("parallel"),
        name="kv_expand",
    )(ckv, kr, w_kvb_bf16, nope_g)


def _softmax_pv(score_list, value_list):
    mx = None
    for s in score_list:
        m_ = jnp.max(s, axis=-1, keepdims=True)
        mx = m_ if mx is None else jnp.maximum(mx, m_)
    den, acc = None, None
    for s, v in zip(score_list, value_list):
        p = jnp.exp(s - mx)
        l_ = jnp.sum(p, axis=-1, keepdims=True)
        o_ = jnp.dot(p.astype(BF16), v, preferred_element_type=F32)
        den = l_ if den is None else den + l_
        acc = o_ if acc is None else acc + o_
    return acc / den


def _qk(q, k):
    return lax.dot_general(q, k, (((1,), (1,)), ((), ())), preferred_element_type=F32)


def _attn_kernel(*refs, has_cache):
    if has_cache:
        qm_ref, qg_ref, km_ref, v_ref, kg_ref, vg_ref, kmc_ref, vc_ref, kgc_ref, vgc_ref, o_ref = refs
    else:
        qm_ref, qg_ref, km_ref, v_ref, kg_ref, vg_ref, o_ref = refs
    scale_m = (MLA_NOPE + MLA_ROPE) ** -0.5
    scale_g = HEAD_DIM ** -0.5
    for hd in range(MLA_HEADS):
        qs = slice(hd * MLA_QK_PAD, (hd + 1) * MLA_QK_PAD)
        vs = slice(hd * MLA_V, (hd + 1) * MLA_V)
        q = qm_ref[:, qs]
        scores = [_qk(q, km_ref[:, qs]) * scale_m]
        values = [v_ref[:, vs]]
        if has_cache:
            scores.append(_qk(q, kmc_ref[:, qs]) * scale_m)
            values.append(vc_ref[:, vs])
        o_ref[:, vs] = _softmax_pv(scores, values).astype(o_ref.dtype)
    base = MLA_HEADS * MLA_V
    for kvh in range(GQA_KV_HEADS):
        ks = slice(kvh * HEAD_DIM, (kvh + 1) * HEAD_DIM)
        k = kg_ref[:, ks].astype(BF16)
        v = vg_ref[:, ks].astype(BF16)
        if has_cache:
            kc = kgc_ref[:, ks].astype(BF16)
            vc = vgc_ref[:, ks].astype(BF16)
        for g in range(GQA_GROUP):
            hd = kvh * GQA_GROUP + g
            hs = slice(hd * HEAD_DIM, (hd + 1) * HEAD_DIM)
            q = qg_ref[:, hs]
            scores, values = [_qk(q, k) * scale_g], [v]
            if has_cache:
                scores.append(_qk(q, kc) * scale_g)
                values.append(vc)
            o_ref[:, base + hd * HEAD_DIM:base + (hd + 1) * HEAD_DIM] = _softmax_pv(scores, values).astype(o_ref.dtype)


def _attention(qm, qg, km, v, kg, vg, seq, tq, cache=None):
    m = qm.shape[0]
    nq = seq // tq
    qrow = lambda w_: pl.BlockSpec((tq, w_), lambda i: (i, 0))
    krow = lambda w_: pl.BlockSpec((seq, w_), lambda i: (i // nq, 0))
    in_specs = [qrow(qm.shape[1]), qrow(qg.shape[1]), krow(km.shape[1]), krow(v.shape[1]),
                krow(kg.shape[1]), krow(vg.shape[1])]
    args = [qm, qg, km, v, kg, vg]
    if cache is not None:
        past = cache[0].shape[0] // (m // seq)
        crow = lambda w_: pl.BlockSpec((past, w_), lambda i: (i // nq, 0))
        in_specs += [crow(c.shape[1]) for c in cache]
        args += list(cache)
    width = MLA_HEADS * MLA_V + GQA_HEADS * HEAD_DIM
    return pl.pallas_call(
        functools.partial(_attn_kernel, has_cache=cache is not None),
        out_shape=jax.ShapeDtypeStruct((m, width), BF16),
        grid=(m // tq,),
        in_specs=in_specs,
        out_specs=qrow(width),
        compiler_params=_cparams("parallel"),
        name="attention",
    )(*args)


def _odd_proj_kernel(x_ref, sh_ref, sc_ref, w_ref, o_ref, h_ref):
    @pl.when(pl.program_id(1) == 0)
    def _():
        h_ref[...] = _modulated(x_ref[...], sh_ref, sc_ref)

    o_ref[...] = jnp.dot(h_ref[...], w_ref[...], preferred_element_type=F32).astype(o_ref.dtype)


def _odd_proj(x, mod_tab, latent, w_in_bf16):
    m, d = x.shape
    n = w_in_bf16.shape[1]
    return pl.pallas_call(
        _odd_proj_kernel,
        out_shape=jax.ShapeDtypeStruct((m, n), BF16),
        grid=(m // TM_ODD, n // TN_ODD),
        in_specs=[pl.BlockSpec((TM_ODD, d), lambda i, j: (i, 0))]
        + _mod_specs(1, latent, TM_ODD, extra_axes=1)[:2]
        + [pl.BlockSpec((d, TN_ODD), lambda i, j: (0, j))],
        out_specs=pl.BlockSpec((TM_ODD, TN_ODD), lambda i, j: (i, j)),
        scratch_shapes=[pltpu.VMEM((TM_ODD, d), BF16)],
        compiler_params=_cparams("parallel", "arbitrary"),
        name="odd_proj",
    )(x, mod_tab, mod_tab, w_in_bf16)


def _log_sigmoid(x):
    return jnp.minimum(x, 0.0) - jnp.log1p(jnp.exp(-jnp.abs(x)))


def _retention_kernel(*refs, heads, has_init):
    if has_init:
        logit_ref, q_ref, k_ref, v_ref, g_ref, rg_ref, s0_ref, o_ref, st_ref, dec_ref, vec_ref = refs
    else:
        logit_ref, q_ref, k_ref, v_ref, g_ref, rg_ref, o_ref, st_ref, dec_ref, vec_ref = refs
    length = q_ref.shape[0]
    h0 = pl.program_id(0) * heads
    scale = RET_DK ** -0.5
    log_g = _log_sigmoid(logit_ref[...])
    lane_h = lax.broadcasted_iota(jnp.int32, (2, RET_HEADS), 1)

    def head_log_decays(hh):
        lg = jnp.sum(jnp.where(lane_h == h0 + hh, log_g, 0.0), axis=1, keepdims=True)
        return lg[0:1, :], lg[1:2, :]

    @pl.when(pl.program_id(1) == 0)
    def _():
        n_i = lax.broadcasted_iota(jnp.int32, (length, length), 0)
        m_i = lax.broadcasted_iota(jnp.int32, (length, length), 1)
        diff = (n_i - m_i).astype(F32)
        pos = lax.broadcasted_iota(jnp.int32, (length, 1), 0).astype(F32)
        for hh in range(heads):
            lg_f, lg_b = head_log_decays(hh)
            dec_ref[hh] = scale * (jnp.where(diff >= 0, jnp.exp(jnp.maximum(diff, 0.0) * lg_f), 0.0)
                                   + jnp.where(diff <= 0, jnp.exp(jnp.maximum(-diff, 0.0) * lg_b), 0.0))
            vec_ref[4 * hh + 0] = jnp.exp((length - 1.0 - pos) * lg_f) * scale
            vec_ref[4 * hh + 1] = jnp.exp(pos * lg_b) * scale
            vec_ref[4 * hh + 2] = jnp.exp((pos + 1.0) * lg_f)
            vec_ref[4 * hh + 3] = jnp.exp((length - pos) * lg_b)

    tn = (((0,), (0,)), ((), ()))
    for hh in range(heads):
        cs = slice(hh * RET_DK, (hh + 1) * RET_DK)
        q, k, v = q_ref[:, cs], k_ref[:, cs], v_ref[:, cs]
        att = _qk(q, k) * dec_ref[hh]
        o = jnp.dot(att.astype(BF16), v, preferred_element_type=F32)
        kf = k.astype(F32)
        s_f = lax.dot_general((kf * vec_ref[4 * hh + 0]).astype(BF16), v, tn, preferred_element_type=F32)
        s_b = lax.dot_general((kf * vec_ref[4 * hh + 1]).astype(BF16), v, tn, preferred_element_type=F32)
        if has_init:
            lg_f, lg_b = head_log_decays(hh)
            s0_f, s0_b = s0_ref[0, 0, hh], s0_ref[0, 1, hh]
            o = o + jnp.dot(q, s0_f.astype(BF16), preferred_element_type=F32) * vec_ref[4 * hh + 2]
            o = o + jnp.dot(q, s0_b.astype(BF16), preferred_element_type=F32) * vec_ref[4 * hh + 3]
            s_f = s_f + s0_f * jnp.exp(length * lg_f)
            s_b = s_b + s0_b * jnp.exp(length * lg_b)
        st_ref[0, 0, hh] = s_f
        st_ref[0, 1, hh] = s_b
        gate = g_ref[:, cs].astype(F32)
        o_ref[:, cs] = (_rms_rows(o) * rg_ref[:, cs] * _silu(gate)).astype(o_ref.dtype)


def _retention(u, decay_logit, ret_g, seq, heads, s0=None):
    m = u.shape[0]
    b = m // seq
    hw = heads * RET_DK
    nh = RET_HEADS // heads
    blocks_per_part = RET_HEADS * RET_DK // hw
    col = lambda part: pl.BlockSpec((seq, hw), lambda j, i, part=part: (i, part * blocks_per_part + j))
    st_spec = pl.BlockSpec((1, 2, heads, RET_DK, RET_DV), lambda j, i: (i, 0, j, 0, 0))
    in_specs = [_const_spec(decay_logit.shape), col(0), col(1), col(2), col(3),
                pl.BlockSpec((1, hw), lambda j, i: (0, j))]
    args = [decay_logit, u, u, u, u, ret_g]
    if s0 is not None:
        in_specs.append(st_spec)
        args.append(s0)
    return pl.pallas_call(
        functools.partial(_retention_kernel, heads=heads, has_init=s0 is not None),
        out_shape=[jax.ShapeDtypeStruct((m, RET_HEADS * RET_DV), BF16),
                   jax.ShapeDtypeStruct((b, 2, RET_HEADS, RET_DK, RET_DV), F32)],
        grid=(nh, b),
        in_specs=in_specs,
        out_specs=[pl.BlockSpec((seq, hw), lambda j, i: (i, j)), st_spec],
        scratch_shapes=[pltpu.VMEM((heads, seq, seq), F32), pltpu.VMEM((4 * heads, seq, 1), F32)],
        compiler_params=_cparams("parallel", "arbitrary"),
        name="retention",
    )(*args)


@functools.lru_cache(maxsize=None)
def _dft_matrices(length):
    n = 2 * length
    f = np.arange(length, dtype=np.float64)[:, None]
    t = np.arange(length, dtype=np.float64)[None, :]
    ang = 2.0 * np.pi * f * t / n
    fwd_a = np.cos(ang)
    fwd_b = np.sin(ang)
    fwd_b[0, :] = np.cos(np.pi * t[0])
    fwd = np.concatenate([fwd_a, fwd_b], axis=0)
    tt = (np.arange(length, dtype=np.float64) + length // 2)[:, None]
    ff = np.arange(length, dtype=np.float64)[None, :]
    ang_i = 2.0 * np.pi * ff * tt / n
    inv_a = 2.0 * np.cos(ang_i) / n
    inv_b = 2.0 * np.sin(ang_i) / n
    inv_a[:, 0] = 1.0 / n
    inv_b[:, 0] = np.cos(np.pi * tt[:, 0]) / n
    inv = np.concatenate([inv_a, inv_b], axis=1)
    return fwd.astype(np.float32), inv.astype(np.float32)


@functools.lru_cache(maxsize=None)
def _filter_features(length):
    t = np.arange(length, dtype=np.float64)
    tn = t / length
    bands = np.arange(1, HY_BANDS + 1, dtype=np.float64)
    ang = 2.0 * np.pi * tn[:, None] * bands[None, :]
    feat = np.concatenate([tn[:, None], np.sin(ang), np.cos(ang)], axis=-1)
    feat = np.pad(feat, ((0, 0), (0, LANES - feat.shape[1])))
    r = (np.abs(t - length // 2) / (length / 2))[:, None]
    return feat.astype(np.float32), r.astype(np.float32)


def _filter_spec_kernel(feat_ref, r_ref, w1_ref, b1_ref, w2_ref, b2_ref, fr_ref, w3_ref, dec_ref, fwd_ref, o_ref):
    hp = lax.Precision.HIGHEST
    z = jnp.sin(fr_ref[0:1, :] * (jnp.dot(feat_ref[...], w1_ref[...], precision=hp, preferred_element_type=F32)
                                  + b1_ref[...]))
    z = jnp.sin(fr_ref[1:2, :] * (jnp.dot(z, w2_ref[...], precision=hp, preferred_element_type=F32) + b2_ref[...]))
    filt = jnp.dot(z, w3_ref[...], precision=hp, preferred_element_type=F32)
    filt = filt * jnp.exp(-r_ref[...] * jnp.abs(dec_ref[...]))
    o_ref[...] = jnp.dot(fwd_ref[...], filt.astype(BF16), preferred_element_type=F32)


def _filter_spectra(length, w1, b1, w2, b2, freq, w3, decay, fwd_bf16, tc):
    feat, r = _filter_features(length)
    nch = w3.shape[1]
    w1p = jnp.pad(w1, ((0, LANES - w1.shape[0]), (0, 0)))
    return pl.pallas_call(
        _filter_spec_kernel,
        out_shape=jax.ShapeDtypeStruct((2 * length, nch), F32),
        grid=(nch // tc,),
        in_specs=[_const_spec(feat.shape), _const_spec(r.shape), _const_spec(w1p.shape), _const_spec((1, HY_FHID)),
                  _const_spec(w2.shape), _const_spec((1, HY_FHID)), _const_spec(freq.shape),
                  pl.BlockSpec((HY_FHID, tc), lambda j: (0, j)), pl.BlockSpec((1, tc), lambda j: (0, j)),
                  _const_spec(fwd_bf16.shape)],
        out_specs=pl.BlockSpec((2 * length, tc), lambda j: (0, j)),
        compiler_params=_cparams("parallel"),
        name="hyena_filter_spectra",
    )(jnp.asarray(feat), jnp.asarray(r), w1p, b1.reshape(1, -1), w2, b2.reshape(1, -1), freq, w3,
      decay.reshape(1, -1), fwd_bf16)


def _hyena_kernel(v_ref, x1_ref, x2_ref, cwv_ref, cw1_ref, cw2_ref, cbv_ref, cb1_ref, cb2_ref,
                  h1_ref, h2_ref, sk1_ref, sk2_ref, fwd_ref, inv_ref, o_ref):
    length, tc = v_ref.shape
    row = lax.broadcasted_iota(jnp.int32, (length, tc), 0)
    first, last = row == 0, row == length - 1

    def short_conv(x_ref, w_ref, b_ref):
        x = x_ref[...].astype(F32)
        prev = jnp.where(first, 0.0, pltpu.roll(x, 1, axis=0))
        nxt = jnp.where(last, 0.0, pltpu.roll(x, length - 1, axis=0))
        return prev * w_ref[0:1, :] + x * w_ref[1:2, :] + nxt * w_ref[2:3, :] + b_ref[...]

    def long_conv(z, h_ref):
        zs = jnp.dot(fwd_ref[...], z.astype(BF16), preferred_element_type=F32)
        za, zb = zs[:length], zs[length:]
        ha, hb = h_ref[:length, :], h_ref[length:, :]
        bb = zb * hb
        ya = za * ha - jnp.where(first, 0.0, bb)
        yb = jnp.where(first, bb, za * hb + zb * ha)
        y = jnp.concatenate([ya, yb], axis=0).astype(BF16)
        return jnp.dot(inv_ref[...], y, preferred_element_type=F32)

    v = short_conv(v_ref, cwv_ref, cbv_ref)
    x1 = short_conv(x1_ref, cw1_ref, cb1_ref)
    x2 = short_conv(x2_ref, cw2_ref, cb2_ref)
    z = x1 * (long_conv(v, h1_ref) + sk1_ref[...] * v)
    o_ref[...] = (x2 * (long_conv(z, h2_ref) + sk2_ref[...] * z)).astype(o_ref.dtype)


def _hyena(u, seq, tc, conv_w, conv_b, spectra, skip, fwd_bf16, inv_bf16):
    m = u.shape[0]
    b = m // seq
    nc = HY_CH // tc
    hy0 = 4 * RET_HEADS * RET_DK // tc
    ucol = lambda part: pl.BlockSpec((seq, tc), lambda i, j, part=part: (i, hy0 + part * nc + j))
    wcol = lambda rows, part: pl.BlockSpec((rows, tc), lambda i, j, part=part: (0, part * nc + j))
    return pl.pallas_call(
        _hyena_kernel,
        out_shape=jax.ShapeDtypeStruct((m, HY_CH), BF16),
        grid=(b, nc),
        in_specs=[ucol(0), ucol(1), ucol(2), wcol(3, 0), wcol(3, 1), wcol(3, 2), wcol(1, 0), wcol(1, 1), wcol(1, 2),
                  wcol(2 * seq, 0), wcol(2 * seq, 1), wcol(1, 0), wcol(1, 1),
                  _const_spec(fwd_bf16.shape), _const_spec(inv_bf16.shape)],
        out_specs=pl.BlockSpec((seq, tc), lambda i, j: (i, j)),
        compiler_params=_cparams("parallel", "parallel"),
        name="hyena",
    )(u, u, u, conv_w, conv_w, conv_w, conv_b, conv_b, conv_b, spectra, spectra, skip, skip, fwd_bf16, inv_bf16)


@functools.lru_cache(maxsize=None)
def _rope_tables(length):
    rows = np.repeat(np.arange(length // GRID_W, dtype=np.float64), GRID_W)
    cols = np.tile(np.arange(GRID_W, dtype=np.float64), length // GRID_W)

    def pack(dim):
        half = dim // 2
        freq = ROPE_THETA ** (-np.arange(0, half, 2, dtype=np.float64) / half)
        ang = np.concatenate([rows[:, None] * freq[None], cols[:, None] * freq[None]], axis=-1)
        pad = ((0, 0), (0, LANES // 2 - half))
        cos = np.pad(np.cos(ang), pad, constant_values=1.0)
        sin = np.pad(np.sin(ang), pad)
        return (np.concatenate([cos, cos], axis=-1).astype(np.float32),
                np.concatenate([-sin, sin], axis=-1).astype(np.float32))

    cos_m, sin_m = pack(MLA_ROPE)
    cos_g, sin_g = pack(HEAD_DIM)
    return cos_m, sin_m, cos_g, sin_g


def _spread_rope(a):
    half = MLA_ROPE // 2
    z = jnp.zeros(a.shape[:-1] + (LANES // 2 - half,), a.dtype)
    return jnp.concatenate([a[..., :half], z, a[..., half:], z], axis=-1)


def _unspread_rope(a):
    half = MLA_ROPE // 2
    return jnp.concatenate([a[..., :half], a[..., LANES // 2:LANES // 2 + half]], axis=-1)


def _even_weights(w_in, q_norm, w_qb, kv_norm, w_kvb, nope_g, rope_g, gqa_g, w_out):
    q_a, kv_a, kr, qg, kg, vg = jnp.split(w_in, np.cumsum([512, 512, 64, 1024, 256]).tolist(), axis=1)
    w_in_p = jnp.concatenate([q_a, kv_a, qg, kg, vg, _spread_rope(kr)], axis=1)
    wq = w_qb.reshape(MLA_Q_RANK, MLA_HEADS, MLA_NOPE + MLA_ROPE)
    wq = jnp.concatenate([wq[..., :MLA_NOPE], _spread_rope(wq[..., MLA_NOPE:])], axis=-1)
    return {
        "w_in": w_in_p.astype(BF16),
        "w_qb": wq.reshape(MLA_Q_RANK, MLA_HEADS * MLA_QK_PAD).astype(BF16),
        "w_kvb": w_kvb.astype(BF16),
        "w_out": w_out.astype(BF16),
        "q_norm": q_norm.reshape(1, -1),
        "kv_norm": kv_norm.reshape(1, -1),
        "nope_g": nope_g,
        "rope_g": _spread_rope(rope_g),
        "gqa_g": gqa_g,
    }


def kernel(x_prompt, x_sample, cache_mla_ckv, cache_mla_krope, cache_gqa_k, cache_gqa_v, state_ret, c, c_ctx, mod_w, mod_b, ffn_w_gate, ffn_w_up, ffn_w_down, ev_w_in, mla_q_norm, mla_w_qb, mla_kv_norm, mla_w_kvb, mla_nope_norm, mla_rope_norm, gqa_qk_norm, ev_w_out, od_w_in, ret_decay_logit, ret_norm, hy_conv_w, hy_conv_b, hy_filt_w1, hy_filt_b1, hy_filt_w2, hy_filt_b2, hy_filt_freq, hy_filt_w3, hy_decay, hy_skip, od_w_out):
    xc = x_prompt.reshape(BATCH * SEQ, D_MODEL)
    xs = x_sample.reshape(DEC_BATCH * DEC_SEQ, D_MODEL)

    cvec = jnp.concatenate([c_ctx[None, :], c, jnp.zeros((8 - 1 - DEC_BATCH, D_MODEL), F32)], axis=0)
    mod_all = _mod_vectors(cvec, mod_w, mod_b)

    tables = tuple(jnp.asarray(t) for t in _rope_tables(DEC_SEQ))

    outs = {}
    for l in range(DEPTH):
        mod_tab = mod_all[l, :1 + DEC_BATCH].reshape((1 + DEC_BATCH) * N_MOD, 1, D_MODEL)
        xc = _ffn(xc, mod_tab, False, 0, ffn_w_gate, ffn_w_up, ffn_w_down, l, 0)
        xs = _ffn(xs, mod_tab, True, 0, ffn_w_gate, ffn_w_up, ffn_w_down, l, 0)
        if l % 2 == 0:
            e = l // 2
            w = _even_weights(ev_w_in[e], mla_q_norm[e], mla_w_qb[e], mla_kv_norm[e], mla_w_kvb[e],
                              mla_nope_norm[e], mla_rope_norm[e], gqa_qk_norm[e], ev_w_out[e])
            qm, ckv, kr, qg, kg, vg = _even_proj(xc, mod_tab, w)
            km, v = _kv_expand(ckv, kr, w["w_kvb"], w["nope_g"], TM_PROJ)
            att = _attention(qm, qg, km, v, kg, vg, SEQ, SEQ)
            xc = _out_proj(xc, mod_tab, False, [att], w["w_out"])
            outs["ckv"], outs["krope"], outs["k"], outs["v"] = ckv, _unspread_rope(kr), kg, vg
            qm, ckv, kr, qg, kg, vg = _even_proj(xs, mod_tab, w, tables)
            km, v = _kv_expand(ckv, kr, w["w_kvb"], w["nope_g"], TM_PROJ)
            c_ckv = cache_mla_ckv[:, e].reshape(DEC_BATCH * PAST_LEN, MLA_KV_RANK)
            c_kr = _spread_rope(cache_mla_krope[:, e].reshape(DEC_BATCH * PAST_LEN, MLA_ROPE))
            kmc, vc = _kv_expand(c_ckv, c_kr, w["w_kvb"], w["nope_g"], PAST_LEN)
            kgc = cache_gqa_k[:, e].reshape(DEC_BATCH * PAST_LEN, GQA_KV_HEADS * HEAD_DIM)
            vgc = cache_gqa_v[:, e].reshape(DEC_BATCH * PAST_LEN, GQA_KV_HEADS * HEAD_DIM)
            att = _attention(qm, qg, km, v, kg, vg, DEC_SEQ, TQ_ATT, cache=(kmc, vc, kgc, vgc))
            xs = _out_proj(xs, mod_tab, True, [att], w["w_out"])
        else:
            o = l // 2
            w_in = od_w_in[o].astype(BF16)
            w_out = od_w_out[o].astype(BF16)
            logit = ret_decay_logit[o]
            ret_g = ret_norm[o].reshape(1, -1)
            skip = hy_skip[o].reshape(1, -1)
            conv_b = hy_conv_b[o].reshape(1, -1)
            filt_args = (hy_filt_w1[o], hy_filt_b1[o], hy_filt_w2[o], hy_filt_b2[o], hy_filt_freq[o],
                         hy_filt_w3[o], hy_decay[o])
            for stream in ("ctx", "smp"):
                x, seq, latent = (xc, SEQ, False) if stream == "ctx" else (xs, DEC_SEQ, True)
                fwd, inv = _dft_matrices(seq)
                fwd_b, inv_b = jnp.asarray(fwd).astype(BF16), jnp.asarray(inv).astype(BF16)
                spectra = _filter_spectra(seq, *filt_args, fwd_b, TC_HY)
                u = _odd_proj(x, mod_tab, latent, w_in)
                if stream == "ctx":
                    o_ret, st = _retention(u, logit, ret_g, seq, RET_HEADS)
                    outs["ret"] = st
                    o_hy = _hyena(u, seq, HY_CH, hy_conv_w[o], conv_b, spectra, skip, fwd_b, inv_b)
                else:
                    o_ret, _ = _retention(u, logit, ret_g, seq, 1, s0=state_ret[:, o])
                    o_hy = _hyena(u, seq, TC_HY, hy_conv_w[o], conv_b, spectra, skip, fwd_b, inv_b)
                x = _out_proj(x, mod_tab, latent, [o_ret, o_hy], w_out)
                if stream == "ctx":
                    xc = x
                else:
                    xs = x
        xc = _ffn(xc, mod_tab, False, 2, ffn_w_gate, ffn_w_up, ffn_w_down, l, 1)
        xs = _ffn(xs, mod_tab, True, 2, ffn_w_gate, ffn_w_up, ffn_w_down, l, 1)

    return (
        xc.reshape(BATCH, SEQ, D_MODEL),
        xs.reshape(DEC_BATCH, DEC_SEQ, D_MODEL),
        outs["ckv"].reshape(BATCH, 1, SEQ, MLA_KV_RANK),
        outs["krope"].reshape(BATCH, 1, SEQ, MLA_ROPE),
        outs["k"].reshape(BATCH, 1, SEQ, GQA_KV_HEADS, HEAD_DIM),
        outs["v"].reshape(BATCH, 1, SEQ, GQA_KV_HEADS, HEAD_DIM),
        outs["ret"].reshape(BATCH, 1, 2, RET_HEADS, RET_DK, RET_DV),
    )
```

```python
import functools
import math

import numpy as np
import jax
import jax.numpy as jnp
from jax import lax
from jax.experimental import pallas as pl
from jax.experimental.pallas import tpu as pltpu

F32 = jnp.float32
BF16 = jnp.bfloat16

D_MODEL = 2048
BATCH = 32
SEQ = 256
DEPTH = 2
DEC_BATCH = 2
DEC_SEQ = 1024
PAST_LEN = 256
GRID_W = 64
N_MOD = 9
D_FF = 5632
ROPE_THETA = 10000.0
EPS = 1e-6

MLA_HEADS = 8
MLA_Q_RANK = 512
MLA_KV_RANK = 512
MLA_NOPE = 128
MLA_ROPE = 64
MLA_V = 128
GQA_HEADS = 8
GQA_KV_HEADS = 2
GQA_GROUP = GQA_HEADS // GQA_KV_HEADS
HEAD_DIM = 128
MLA_SCALE = (MLA_NOPE + MLA_ROPE) ** -0.5
GQA_SCALE = HEAD_DIM ** -0.5
MLA_QK_PAD = 256

RET_HEADS = 8
RET_DK = 128
RET_DV = 128
HY_CH = 1024
HY_ORDER = 2
HY_BANDS = 16
HY_FHID = 64
ODD_IN = 4 * RET_HEADS * RET_DK + (HY_ORDER + 1) * HY_CH

V7X_VMEM_BYTES = 64 * 1024 * 1024
LANES = 128
VMEM_LIMIT = V7X_VMEM_BYTES - 6 * 1024 * 1024

TM_FFN = 1024
TF_FFN = 256
TM_PROJ = 512
TM_ODD = 1024
TN_MOD = 1024
TN_ODD = 1024
TQ_ATT = 256
TC_HY = 256


def _cparams(*sem):
    return pltpu.CompilerParams(dimension_semantics=sem, vmem_limit_bytes=VMEM_LIMIT)


def _const_spec(shape):
    nd = len(shape)
    return pl.BlockSpec(shape, lambda *_: (0,) * nd)


def _silu(x):
    return x * jax.nn.sigmoid(x)


def _rms_rows(x, width=None):
    ss = jnp.sum(x * x, axis=-1, keepdims=True)
    n = x.shape[-1] if width is None else width
    return x * lax.rsqrt(ss * (1.0 / n) + EPS)


def _mod_kernel(c_ref, w_ref, b_ref, o_ref):
    s = _silu(c_ref[...]).astype(BF16)
    o_ref[0] = jnp.dot(s, w_ref[0].astype(BF16), preferred_element_type=F32) + b_ref[0]


def _mod_vectors(cvec, mod_w, mod_b):
    depth, d, n = mod_w.shape
    rows = cvec.shape[0]
    return pl.pallas_call(
        _mod_kernel,
        out_shape=jax.ShapeDtypeStruct((depth, rows, n), F32),
        grid=(depth, n // TN_MOD),
        in_specs=[
            pl.BlockSpec((rows, d), lambda l, j: (0, 0)),
            pl.BlockSpec((1, d, TN_MOD), lambda l, j: (l, 0, j)),
            pl.BlockSpec((1, 1, TN_MOD), lambda l, j: (l, 0, j)),
        ],
        out_specs=pl.BlockSpec((1, rows, TN_MOD), lambda l, j: (l, 0, j)),
        compiler_params=_cparams("parallel", "parallel"),
        name="mod_vectors",
    )(cvec, mod_w, mod_b.reshape(depth, 1, n))


def _mod_specs(sub, latent, tile, extra_axes=0):
    group_of_tile = (lambda i: 1 + (i * tile) // DEC_SEQ) if latent else (lambda i: 0)

    def spec(k):
        if extra_axes:
            return pl.BlockSpec((1, 1, D_MODEL), lambda i, j: (group_of_tile(i) * N_MOD + 3 * sub + k, 0, 0))
        return pl.BlockSpec((1, 1, D_MODEL), lambda i: (group_of_tile(i) * N_MOD + 3 * sub + k, 0, 0))
    return [spec(0), spec(1), spec(2)]


def _modulated(x, sh_ref, sc_ref):
    return (_rms_rows(x) * (1.0 + sc_ref[0]) + sh_ref[0]).astype(BF16)


def _ffn_kernel(x_ref, sh_ref, sc_ref, gt_ref, wg_ref, wu_ref, wd_ref, o_ref, h_ref):
    f = pl.program_id(1)
    last = pl.num_programs(1) - 1

    def partial_out(h):
        g = jnp.dot(h, wg_ref[...].astype(BF16), preferred_element_type=F32)
        u = jnp.dot(h, wu_ref[...].astype(BF16), preferred_element_type=F32)
        a = (_silu(g) * u).astype(BF16)
        return jnp.dot(a, wd_ref[...].astype(BF16), preferred_element_type=F32)

    @pl.when(f == 0)
    def _():
        h = _modulated(x_ref[...], sh_ref, sc_ref)
        h_ref[...] = h
        o_ref[...] = partial_out(h)

    @pl.when(jnp.logical_and(f > 0, f < last))
    def _():
        o_ref[...] += partial_out(h_ref[...])

    @pl.when(f == last)
    def _():
        o_ref[...] = x_ref[...] + 0.5 * gt_ref[0] * (o_ref[...] + partial_out(h_ref[...]))


def _ffn(x, mod_tab, latent, sub, w_gate, w_up, w_down, layer, j):
    m, d = x.shape
    dff = w_gate.shape[-1]
    return pl.pallas_call(
        _ffn_kernel,
        out_shape=jax.ShapeDtypeStruct((m, d), F32),
        grid=(m // TM_FFN, dff // TF_FFN),
        in_specs=[pl.BlockSpec((TM_FFN, d), lambda i, f: (i, 0))]
        + _mod_specs(sub, latent, TM_FFN, extra_axes=1)
        + [
            pl.BlockSpec((None, None, d, TF_FFN), lambda i, f: (layer, j, 0, f)),
            pl.BlockSpec((None, None, d, TF_FFN), lambda i, f: (layer, j, 0, f)),
            pl.BlockSpec((None, None, TF_FFN, d), lambda i, f: (layer, j, f, 0)),
        ],
        out_specs=pl.BlockSpec((TM_FFN, d), lambda i, f: (i, 0)),
        scratch_shapes=[pltpu.VMEM((TM_FFN, d), BF16)],
        compiler_params=_cparams("parallel", "arbitrary"),
        name="ffn",
    )(x, mod_tab, mod_tab, mod_tab, w_gate, w_up, w_down)


def _out_proj_kernel(*refs, n_parts):
    x_ref, gt_ref, w_ref = refs[0], refs[1], refs[2]
    part_refs = refs[3:3 + n_parts]
    o_ref, wb_ref = refs[3 + n_parts], refs[4 + n_parts]

    @pl.when(pl.program_id(0) == 0)
    def _():
        wb_ref[...] = w_ref[...].astype(BF16)

    acc, off = None, 0
    for p_ref in part_refs:
        width = p_ref.shape[1]
        t = jnp.dot(p_ref[...], wb_ref[off:off + width, :], preferred_element_type=F32)
        acc = t if acc is None else acc + t
        off += width
    o_ref[...] = x_ref[...] + gt_ref[0] * acc


def _out_proj(x, mod_tab, latent, parts, w_out):
    m, d = x.shape
    return pl.pallas_call(
        functools.partial(_out_proj_kernel, n_parts=len(parts)),
        out_shape=jax.ShapeDtypeStruct((m, d), F32),
        grid=(m // TM_PROJ,),
        in_specs=[pl.BlockSpec((TM_PROJ, d), lambda i: (i, 0)), _mod_specs(1, latent, TM_PROJ)[2],
                  pl.BlockSpec(w_out.shape, lambda i: (0, 0), pipeline_mode=pl.Buffered(1))]
        + [pl.BlockSpec((TM_PROJ, p.shape[1]), lambda i: (i, 0)) for p in parts],
        out_specs=pl.BlockSpec((TM_PROJ, d), lambda i: (i, 0)),
        scratch_shapes=[pltpu.VMEM(w_out.shape, BF16)],
        compiler_params=_cparams("arbitrary"),
        name="mixer_out_proj",
    )(x, mod_tab, w_out, *parts)


EV_QA, EV_KVA, EV_QG, EV_KG, EV_VG, EV_KR, EV_END = 0, 512, 1024, 2048, 2304, 2560, 2688


def _rotate_half(x, cos, sin):
    return x * cos + pltpu.roll(x, LANES // 2, axis=1) * sin


def _even_proj_kernel(*refs, rotary):
    (x_ref, sh_ref, sc_ref, w_in_ref, w_qb_ref, qn_ref, kvn_ref, ng_ref, rg_ref, gg_ref) = refs[:10]
    if rotary:
        cm_ref, sm_ref, cg_ref, sg_ref = refs[10:14]
        rot_m = lambda t: _rotate_half(t, cm_ref[...], sm_ref[...])
        rot_g = lambda t: _rotate_half(t, cg_ref[...], sg_ref[...])
    else:
        rot_m = rot_g = lambda t: t
    qm_ref, ckv_ref, kr_ref, qg_ref, kg_ref, vg_ref = refs[-6:]

    h = _modulated(x_ref[...], sh_ref, sc_ref)
    u = jnp.dot(h, w_in_ref[...], preferred_element_type=F32)

    nope_gain = ng_ref[0:1, :] * MLA_SCALE
    rope_gain = rg_ref[0:1, :] * MLA_SCALE
    gqa_gain = gg_ref[0:1, :] * GQA_SCALE

    qa = (_rms_rows(u[:, EV_QA:EV_KVA]) * qn_ref[...]).astype(BF16)
    q = jnp.dot(qa, w_qb_ref[...], preferred_element_type=F32)
    for hd in range(MLA_HEADS):
        c0 = hd * MLA_QK_PAD
        nope = _rms_rows(q[:, c0:c0 + MLA_NOPE]) * nope_gain
        rope = _rms_rows(q[:, c0 + MLA_NOPE:c0 + MLA_QK_PAD], MLA_ROPE) * rope_gain
        qm_ref[:, c0:c0 + MLA_NOPE] = nope.astype(BF16)
        qm_ref[:, c0 + MLA_NOPE:c0 + MLA_QK_PAD] = rot_m(rope).astype(BF16)

    ckv_ref[...] = _rms_rows(u[:, EV_KVA:EV_QG]) * kvn_ref[...]
    kr_ref[...] = rot_m(_rms_rows(u[:, EV_KR:EV_END], MLA_ROPE) * rg_ref[1:2, :])

    for hd in range(GQA_HEADS):
        c0 = EV_QG + hd * HEAD_DIM
        qh = _rms_rows(u[:, c0:c0 + HEAD_DIM]) * gqa_gain
        qg_ref[:, hd * HEAD_DIM:(hd + 1) * HEAD_DIM] = rot_g(qh).astype(BF16)
    for hd in range(GQA_KV_HEADS):
        c0 = EV_KG + hd * HEAD_DIM
        kh = _rms_rows(u[:, c0:c0 + HEAD_DIM]) * gg_ref[1:2, :]
        kg_ref[:, hd * HEAD_DIM:(hd + 1) * HEAD_DIM] = rot_g(kh)
    vg_ref[...] = u[:, EV_VG:EV_KR]


def _even_proj(x, mod_tab, w, tables=None):
    m, d = x.shape
    tm = TM_PROJ
    row = lambda w_: pl.BlockSpec((tm, w_), lambda i: (i, 0))
    tab_specs, tab_args = [], []
    if tables is not None:
        tiles_per_seq = tables[0].shape[0] // tm
        tab_specs = [pl.BlockSpec((tm, LANES), lambda i: (i % tiles_per_seq, 0))] * 4
        tab_args = list(tables)
    return pl.pallas_call(
        functools.partial(_even_proj_kernel, rotary=tables is not None),
        out_shape=[
            jax.ShapeDtypeStruct((m, MLA_HEADS * MLA_QK_PAD), BF16),
            jax.ShapeDtypeStruct((m, MLA_KV_RANK), F32),
            jax.ShapeDtypeStruct((m, LANES), F32),
            jax.ShapeDtypeStruct((m, GQA_HEADS * HEAD_DIM), BF16),
            jax.ShapeDtypeStruct((m, GQA_KV_HEADS * HEAD_DIM), F32),
            jax.ShapeDtypeStruct((m, GQA_KV_HEADS * HEAD_DIM), F32),
        ],
        grid=(m // tm,),
        in_specs=[row(d)] + _mod_specs(1, tables is not None, tm)[:2]
        + [_const_spec(w["w_in"].shape), _const_spec(w["w_qb"].shape), _const_spec(w["q_norm"].shape),
           _const_spec(w["kv_norm"].shape), _const_spec(w["nope_g"].shape), _const_spec(w["rope_g"].shape),
           _const_spec(w["gqa_g"].shape)] + tab_specs,
        out_specs=[row(MLA_HEADS * MLA_QK_PAD), row(MLA_KV_RANK), row(LANES), row(GQA_HEADS * HEAD_DIM),
                   row(GQA_KV_HEADS * HEAD_DIM), row(GQA_KV_HEADS * HEAD_DIM)],
        compiler_params=_cparams("parallel"),
        name="even_proj",
    )(x, mod_tab, mod_tab, w["w_in"], w["w_qb"], w["q_norm"], w["kv_norm"], w["nope_g"], w["rope_g"], w["gqa_g"],
      *tab_args)


def _kv_expand_kernel(ckv_ref, kr_ref, w_ref, ng_ref, km_ref, v_ref):
    kv = jnp.dot(ckv_ref[...].astype(BF16), w_ref[...], preferred_element_type=F32)
    kr = kr_ref[...].astype(BF16)
    for hd in range(MLA_HEADS):
        c0 = hd * (MLA_NOPE + MLA_V)
        km_ref[:, hd * MLA_QK_PAD:hd * MLA_QK_PAD + MLA_NOPE] = (
            _rms_rows(kv[:, c0:c0 + MLA_NOPE]) * ng_ref[1:2, :]).astype(BF16)
        km_ref[:, hd * MLA_QK_PAD + MLA_NOPE:(hd + 1) * MLA_QK_PAD] = kr
        v_ref[:, hd * MLA_V:(hd + 1) * MLA_V] = kv[:, c0 + MLA_NOPE:c0 + MLA_NOPE + MLA_V].astype(BF16)


def _kv_expand(ckv, kr, w_kvb_bf16, nope_g, tr):
    r = ckv.shape[0]
    row = lambda w_: pl.BlockSpec((tr, w_), lambda i: (i, 0))
    return pl.pallas_call(
        _kv_expand_kernel,
        out_shape=[jax.ShapeDtypeStruct((r, MLA_HEADS * MLA_QK_PAD), BF16),
                   jax.ShapeDtypeStruct((r, MLA_HEADS * MLA_V), BF16)],
        grid=(r // tr,),
        in_specs=[row(MLA_KV_RANK), row(LANES), _const_spec(w_kvb_bf16.shape), _const_spec(nope_g.shape)],
        out_specs=[row(MLA_HEADS * MLA_QK_PAD), row(MLA_HEADS * MLA_V)],
        compiler_params=_cparams("parallel"),
        name="kv_expand",
    )(ckv, kr, w_kvb_bf16, nope_g)


def _softmax_pv(score_list, value_list):
    mx = None
    for s in score_list:
        m_ = jnp.max(s, axis=-1, keepdims=True)
        mx = m_ if mx is None else jnp.maximum(mx, m_)
    den, acc = None, None
    for s, v in zip(score_list, value_list):
        p = jnp.exp(s - mx)
        l_ = jnp.sum(p, axis=-1, keepdims=True)
        o_ = jnp.dot(p.astype(BF16), v, preferred_element_type=F32)
        den = l_ if den is None else den + l_
        acc = o_ if acc is None else acc + o_
    return acc / den


def _qk(q, k):
    return lax.dot_general(q, k, (((1,), (1,)), ((), ())), preferred_element_type=F32)


def _attn_kernel(*refs, has_cache):
    if has_cache:
        qm_ref, qg_ref, km_ref, v_ref, kg_ref, vg_ref, kmc_ref, vc_ref, kgc_ref, vgc_ref, o_ref = refs
    else:
        qm_ref, qg_ref, km_ref, v_ref, kg_ref, vg_ref, o_ref = refs
    for hd in range(MLA_HEADS):
        qs = slice(hd * MLA_QK_PAD, (hd + 1) * MLA_QK_PAD)
        vs = slice(hd * MLA_V, (hd + 1) * MLA_V)
        q = qm_ref[:, qs]
        scores = [_qk(q, km_ref[:, qs])]
        values = [v_ref[:, vs]]
        if has_cache:
            scores.append(_qk(q, kmc_ref[:, qs]))
            values.append(vc_ref[:, vs])
        o_ref[:, vs] = _softmax_pv(scores, values).astype(o_ref.dtype)
    base = MLA_HEADS * MLA_V
    for kvh in range(GQA_KV_HEADS):
        ks = slice(kvh * HEAD_DIM, (kvh + 1) * HEAD_DIM)
        k = kg_ref[:, ks].astype(BF16)
        v = vg_ref[:, ks].astype(BF16)
        if has_cache:
            kc = kgc_ref[:, ks].astype(BF16)
            vc = vgc_ref[:, ks].astype(BF16)
        for g in range(GQA_GROUP):
            hd = kvh * GQA_GROUP + g
            hs = slice(hd * HEAD_DIM, (hd + 1) * HEAD_DIM)
            q = qg_ref[:, hs]
            scores, values = [_qk(q, k)], [v]
            if has_cache:
                scores.append(_qk(q, kc))
                values.append(vc)
            o_ref[:, base + hd * HEAD_DIM:base + (hd + 1) * HEAD_DIM] = _softmax_pv(scores, values).astype(o_ref.dtype)


def _attention(qm, qg, km, v, kg, vg, seq, tq, cache=None):
    m = qm.shape[0]
    nq = seq // tq
    qrow = lambda w_: pl.BlockSpec((tq, w_), lambda i: (i, 0))
    krow = lambda w_: pl.BlockSpec((seq, w_), lambda i: (i // nq, 0))
    in_specs = [qrow(qm.shape[1]), qrow(qg.shape[1]), krow(km.shape[1]), krow(v.shape[1]),
                krow(kg.shape[1]), krow(vg.shape[1])]
    args = [qm, qg, km, v, kg, vg]
    if cache is not None:
        past = cache[0].shape[0] // (m // seq)
        crow = lambda w_: pl.BlockSpec((past, w_), lambda i: (i // nq, 0))
        in_specs += [crow(c.shape[1]) for c in cache]
        args += list(cache)
    width = MLA_HEADS * MLA_V + GQA_HEADS * HEAD_DIM
    return pl.pallas_call(
        functools.partial(_attn_kernel, has_cache=cache is not None),
        out_shape=jax.ShapeDtypeStruct((m, width), BF16),
        grid=(m // tq,),
        in_specs=in_specs,
        out_specs=qrow(width),
        compiler_params=_cparams("parallel"),
        name="attention",
    )(*args)


def _odd_proj_kernel(x_ref, sh_ref, sc_ref, w_ref, o_ref, h_ref):
    def project(h):
        return jnp.dot(h, w_ref[...].astype(BF16), preferred_element_type=F32).astype(o_ref.dtype)

    @pl.when(pl.program_id(1) == 0)
    def _():
        h = _modulated(x_ref[...], sh_ref, sc_ref)
        h_ref[...] = h
        o_ref[...] = project(h)

    @pl.when(pl.program_id(1) > 0)
    def _():
        o_ref[...] = project(h_ref[...])


def _odd_proj(x, mod_tab, latent, w_in, layer):
    m, d = x.shape
    n = w_in.shape[-1]
    return pl.pallas_call(
        _odd_proj_kernel,
        out_shape=jax.ShapeDtypeStruct((m, n), BF16),
        grid=(m // TM_ODD, n // TN_ODD),
        in_specs=[pl.BlockSpec((TM_ODD, d), lambda i, j: (i, 0))]
        + _mod_specs(1, latent, TM_ODD, extra_axes=1)[:2]
        + [pl.BlockSpec((None, d, TN_ODD), lambda i, j: (layer, 0, j))],
        out_specs=pl.BlockSpec((TM_ODD, TN_ODD), lambda i, j: (i, j)),
        scratch_shapes=[pltpu.VMEM((TM_ODD, d), BF16)],
        compiler_params=_cparams("parallel", "arbitrary"),
        name="odd_proj",
    )(x, mod_tab, mod_tab, w_in)


def _log_sigmoid(x):
    return jnp.minimum(x, 0.0) - jnp.log1p(jnp.exp(-jnp.abs(x)))


def _retention_kernel(*refs, heads, has_init):
    if has_init:
        logit_ref, q_ref, k_ref, v_ref, g_ref, rg_ref, s0_ref, o_ref, st_ref, dec_ref, vec_ref = refs
    else:
        logit_ref, q_ref, k_ref, v_ref, g_ref, rg_ref, o_ref, st_ref, dec_ref, vec_ref = refs
    length = q_ref.shape[0]
    h0 = pl.program_id(0) * heads
    scale = RET_DK ** -0.5
    log_g = _log_sigmoid(logit_ref[...])
    lane_h = lax.broadcasted_iota(jnp.int32, (2, RET_HEADS), 1)

    def head_log_decays(hh):
        lg = jnp.sum(jnp.where(lane_h == h0 + hh, log_g, 0.0), axis=1, keepdims=True)
        return lg[0:1, :], lg[1:2, :]

    @pl.when(pl.program_id(1) == 0)
    def _():
        n_i = lax.broadcasted_iota(jnp.int32, (length, length), 0)
        m_i = lax.broadcasted_iota(jnp.int32, (length, length), 1)
        diff = (n_i - m_i).astype(F32)
        pos = lax.broadcasted_iota(jnp.int32, (length, RET_DK), 0).astype(F32)
        for hh in range(heads):
            lg_f, lg_b = head_log_decays(hh)
            dec_ref[hh] = scale * (jnp.where(diff >= 0, jnp.exp(jnp.maximum(diff, 0.0) * lg_f), 0.0)
                                   + jnp.where(diff <= 0, jnp.exp(jnp.maximum(-diff, 0.0) * lg_b), 0.0))
            vec_ref[4 * hh + 0] = jnp.exp((length - 1.0 - pos) * lg_f) * scale
            vec_ref[4 * hh + 1] = jnp.exp(pos * lg_b) * scale
            vec_ref[4 * hh + 2] = jnp.exp((pos + 1.0) * lg_f)
            vec_ref[4 * hh + 3] = jnp.exp((length - pos) * lg_b)

    tn = (((0,), (0,)), ((), ()))
    for hh in range(heads):
        cs = slice(hh * RET_DK, (hh + 1) * RET_DK)
        q, k, v = q_ref[:, cs], k_ref[:, cs], v_ref[:, cs]
        att = _qk(q, k) * dec_ref[hh]
        o = jnp.dot(att.astype(BF16), v, preferred_element_type=F32)
        kf = k.astype(F32)
        s_f = lax.dot_general((kf * vec_ref[4 * hh + 0]).astype(BF16), v, tn, preferred_element_type=F32)
        s_b = lax.dot_general((kf * vec_ref[4 * hh + 1]).astype(BF16), v, tn, preferred_element_type=F32)
        if has_init:
            lg_f, lg_b = head_log_decays(hh)
            s0_f, s0_b = s0_ref[0, 0, hh], s0_ref[0, 1, hh]
            o = o + jnp.dot(q, s0_f.astype(BF16), preferred_element_type=F32) * vec_ref[4 * hh + 2]
            o = o + jnp.dot(q, s0_b.astype(BF16), preferred_element_type=F32) * vec_ref[4 * hh + 3]
            s_f = s_f + s0_f * jnp.exp(length * lg_f)
            s_b = s_b + s0_b * jnp.exp(length * lg_b)
        st_ref[0, 0, hh] = s_f
        st_ref[0, 1, hh] = s_b
        gate = g_ref[:, cs].astype(F32)
        o_ref[:, cs] = (_rms_rows(o) * rg_ref[:, cs] * _silu(gate)).astype(o_ref.dtype)


def _retention(u, decay_logit, ret_g, seq, heads, s0=None):
    m = u.shape[0]
    b = m // seq
    hw = heads * RET_DK
    nh = RET_HEADS // heads
    blocks_per_part = RET_HEADS * RET_DK // hw
    col = lambda part: pl.BlockSpec((seq, hw), lambda j, i, part=part: (i, part * blocks_per_part + j))
    st_spec = pl.BlockSpec((1, 2, heads, RET_DK, RET_DV), lambda j, i: (i, 0, j, 0, 0))
    in_specs = [_const_spec(decay_logit.shape), col(0), col(1), col(2), col(3),
                pl.BlockSpec((1, hw), lambda j, i: (0, j))]
    args = [decay_logit, u, u, u, u, ret_g]
    if s0 is not None:
        in_specs.append(st_spec)
        args.append(s0)
    return pl.pallas_call(
        functools.partial(_retention_kernel, heads=heads, has_init=s0 is not None),
        out_shape=[jax.ShapeDtypeStruct((m, RET_HEADS * RET_DV), BF16),
                   jax.ShapeDtypeStruct((b, 2, RET_HEADS, RET_DK, RET_DV), F32)],
        grid=(nh, b),
        in_specs=in_specs,
        out_specs=[pl.BlockSpec((seq, hw), lambda j, i: (i, j)), st_spec],
        scratch_shapes=[pltpu.VMEM((heads, seq, seq), F32), pltpu.VMEM((4 * heads, seq, RET_DK), F32)],
        compiler_params=_cparams("parallel", "arbitrary"),
        name="retention",
    )(*args)


@functools.lru_cache(maxsize=None)
def _dft_matrices(length):
    n = 2 * length
    f = np.arange(length, dtype=np.float64)[:, None]
    t = np.arange(length, dtype=np.float64)[None, :]
    ang = 2.0 * np.pi * f * t / n
    fwd_a = np.cos(ang)
    fwd_b = np.sin(ang)
    fwd_b[0, :] = np.cos(np.pi * t[0])
    fwd = np.concatenate([fwd_a, fwd_b], axis=0)
    tt = (np.arange(length, dtype=np.float64) + length // 2)[:, None]
    ff = np.arange(length, dtype=np.float64)[None, :]
    ang_i = 2.0 * np.pi * ff * tt / n
    inv_a = 2.0 * np.cos(ang_i) / n
    inv_b = 2.0 * np.sin(ang_i) / n
    inv_a[:, 0] = 1.0 / n
    inv_b[:, 0] = np.cos(np.pi * tt[:, 0]) / n
    inv = np.concatenate([inv_a, inv_b], axis=1)
    return fwd.astype(np.float32), inv.astype(np.float32)


@functools.lru_cache(maxsize=None)
def _filter_features(length):
    t = np.arange(length, dtype=np.float64)
    tn = t / length
    bands = np.arange(1, HY_BANDS + 1, dtype=np.float64)
    ang = 2.0 * np.pi * tn[:, None] * bands[None, :]
    feat = np.concatenate([tn[:, None], np.sin(ang), np.cos(ang)], axis=-1)
    feat = np.pad(feat, ((0, 0), (0, LANES - feat.shape[1])))
    r = (np.abs(t - length // 2) / (length / 2))[:, None]
    return feat.astype(np.float32), r.astype(np.float32)


def _filter_spec_kernel(feat_ref, r_ref, w1_ref, b1_ref, w2_ref, b2_ref, fr_ref, w3_ref, dec_ref, fwd_ref, o_ref):
    hp = lax.Precision.HIGHEST
    z = jnp.sin(fr_ref[0:1, :] * (jnp.dot(feat_ref[...], w1_ref[...], precision=hp, preferred_element_type=F32)
                                  + b1_ref[...]))
    z = jnp.sin(fr_ref[1:2, :] * (jnp.dot(z, w2_ref[...], precision=hp, preferred_element_type=F32) + b2_ref[...]))
    filt = jnp.dot(z, w3_ref[...], precision=hp, preferred_element_type=F32)
    filt = filt * jnp.exp(-r_ref[...] * jnp.abs(dec_ref[...]))
    o_ref[...] = jnp.dot(fwd_ref[...], filt.astype(BF16), preferred_element_type=F32)


def _filter_spectra(length, w1, b1, w2, b2, freq, w3, decay, fwd_bf16, tc):
    feat, r = _filter_features(length)
    nch = w3.shape[1]
    w1p = jnp.pad(w1, ((0, LANES - w1.shape[0]), (0, 0)))
    return pl.pallas_call(
        _filter_spec_kernel,
        out_shape=jax.ShapeDtypeStruct((2 * length, nch), F32),
        grid=(nch // tc,),
        in_specs=[_const_spec(feat.shape), _const_spec(r.shape), _const_spec(w1p.shape), _const_spec((1, HY_FHID)),
                  _const_spec(w2.shape), _const_spec((1, HY_FHID)), _const_spec(freq.shape),
                  pl.BlockSpec((HY_FHID, tc), lambda j: (0, j)), pl.BlockSpec((1, tc), lambda j: (0, j)),
                  _const_spec(fwd_bf16.shape)],
        out_specs=pl.BlockSpec((2 * length, tc), lambda j: (0, j)),
        compiler_params=_cparams("parallel"),
        name="hyena_filter_spectra",
    )(jnp.asarray(feat), jnp.asarray(r), w1p, b1.reshape(1, -1), w2, b2.reshape(1, -1), freq, w3,
      decay.reshape(1, -1), fwd_bf16)


def _hyena_kernel(v_ref, x1_ref, x2_ref, cwv_ref, cw1_ref, cw2_ref, cbv_ref, cb1_ref, cb2_ref,
                  h1_ref, h2_ref, sk1_ref, sk2_ref, fwd_ref, inv_ref, o_ref):
    length, tc = v_ref.shape
    row = lax.broadcasted_iota(jnp.int32, (length, tc), 0)
    first, last = row == 0, row == length - 1

    def short_conv(x_ref, w_ref, b_ref):
        x = x_ref[...].astype(F32)
        prev = jnp.where(first, 0.0, pltpu.roll(x, 1, axis=0))
        nxt = jnp.where(last, 0.0, pltpu.roll(x, length - 1, axis=0))
        return prev * w_ref[0:1, :] + x * w_ref[1:2, :] + nxt * w_ref[2:3, :] + b_ref[...]

    def long_conv(z, h_ref):
        zs = jnp.dot(fwd_ref[...], z.astype(BF16), preferred_element_type=F32)
        za, zb = zs[:length], zs[length:]
        ha, hb = h_ref[:length, :], h_ref[length:, :]
        bb = zb * hb
        ya = za * ha - jnp.where(first, 0.0, bb)
        yb = jnp.where(first, bb, za * hb + zb * ha)
        y = jnp.concatenate([ya, yb], axis=0).astype(BF16)
        return jnp.dot(inv_ref[...], y, preferred_element_type=F32)

    v = short_conv(v_ref, cwv_ref, cbv_ref)
    x1 = short_conv(x1_ref, cw1_ref, cb1_ref)
    x2 = short_conv(x2_ref, cw2_ref, cb2_ref)
    z = x1 * (long_conv(v, h1_ref) + sk1_ref[...] * v)
    o_ref[...] = (x2 * (long_conv(z, h2_ref) + sk2_ref[...] * z)).astype(o_ref.dtype)


def _hyena(u, seq, tc, conv_w, conv_b, spectra, skip, fwd_bf16, inv_bf16):
    m = u.shape[0]
    b = m // seq
    nc = HY_CH // tc
    hy0 = 4 * RET_HEADS * RET_DK // tc
    ucol = lambda part: pl.BlockSpec((seq, tc), lambda i, j, part=part: (i, hy0 + part * nc + j))
    wcol = lambda rows, part: pl.BlockSpec((rows, tc), lambda i, j, part=part: (0, part * nc + j))
    return pl.pallas_call(
        _hyena_kernel,
        out_shape=jax.ShapeDtypeStruct((m, HY_CH), BF16),
        grid=(b, nc),
        in_specs=[ucol(0), ucol(1), ucol(2), wcol(3, 0), wcol(3, 1), wcol(3, 2), wcol(1, 0), wcol(1, 1), wcol(1, 2),
                  wcol(2 * seq, 0), wcol(2 * seq, 1), wcol(1, 0), wcol(1, 1),
                  _const_spec(fwd_bf16.shape), _const_spec(inv_bf16.shape)],
        out_specs=pl.BlockSpec((seq, tc), lambda i, j: (i, j)),
        compiler_params=_cparams("parallel", "parallel"),
        name="hyena",
    )(u, u, u, conv_w, conv_w, conv_w, conv_b, conv_b, conv_b, spectra, spectra, skip, skip, fwd_bf16, inv_bf16)


@functools.lru_cache(maxsize=None)
def _rope_tables(length):
    rows = np.repeat(np.arange(length // GRID_W, dtype=np.float64), GRID_W)
    cols = np.tile(np.arange(GRID_W, dtype=np.float64), length // GRID_W)

    def pack(dim):
        half = dim // 2
        freq = ROPE_THETA ** (-np.arange(0, half, 2, dtype=np.float64) / half)
        ang = np.concatenate([rows[:, None] * freq[None], cols[:, None] * freq[None]], axis=-1)
        pad = ((0, 0), (0, LANES // 2 - half))
        cos = np.pad(np.cos(ang), pad, constant_values=1.0)
        sin = np.pad(np.sin(ang), pad)
        return (np.concatenate([cos, cos], axis=-1).astype(np.float32),
                np.concatenate([-sin, sin], axis=-1).astype(np.float32))

    cos_m, sin_m = pack(MLA_ROPE)
    cos_g, sin_g = pack(HEAD_DIM)
    return cos_m, sin_m, cos_g, sin_g


def _spread_rope(a):
    half = MLA_ROPE // 2
    z = jnp.zeros(a.shape[:-1] + (LANES // 2 - half,), a.dtype)
    return jnp.concatenate([a[..., :half], z, a[..., half:], z], axis=-1)


def _unspread_rope(a):
    half = MLA_ROPE // 2
    return jnp.concatenate([a[..., :half], a[..., LANES // 2:LANES // 2 + half]], axis=-1)


def _even_weights(w_in, q_norm, w_qb, kv_norm, w_kvb, nope_g, rope_g, gqa_g, w_out):
    q_a, kv_a, kr, qg, kg, vg = jnp.split(w_in, np.cumsum([512, 512, 64, 1024, 256]).tolist(), axis=1)
    w_in_p = jnp.concatenate([q_a, kv_a, qg, kg, vg, _spread_rope(kr)], axis=1)
    wq = w_qb.reshape(MLA_Q_RANK, MLA_HEADS, MLA_NOPE + MLA_ROPE)
    wq = jnp.concatenate([wq[..., :MLA_NOPE], _spread_rope(wq[..., MLA_NOPE:])], axis=-1)
    return {
        "w_in": w_in_p.astype(BF16),
        "w_qb": wq.reshape(MLA_Q_RANK, MLA_HEADS * MLA_QK_PAD).astype(BF16),
        "w_kvb": w_kvb.astype(BF16),
        "w_out": w_out,
        "q_norm": q_norm.reshape(1, -1),
        "kv_norm": kv_norm.reshape(1, -1),
        "nope_g": nope_g,
        "rope_g": _spread_rope(rope_g),
        "gqa_g": gqa_g,
    }


def kernel(x_prompt, x_sample, cache_mla_ckv, cache_mla_krope, cache_gqa_k, cache_gqa_v, state_ret, c, c_ctx, mod_w, mod_b, ffn_w_gate, ffn_w_up, ffn_w_down, ev_w_in, mla_q_norm, mla_w_qb, mla_kv_norm, mla_w_kvb, mla_nope_norm, mla_rope_norm, gqa_qk_norm, ev_w_out, od_w_in, ret_decay_logit, ret_norm, hy_conv_w, hy_conv_b, hy_filt_w1, hy_filt_b1, hy_filt_w2, hy_filt_b2, hy_filt_freq, hy_filt_w3, hy_decay, hy_skip, od_w_out):
    xc = x_prompt.reshape(BATCH * SEQ, D_MODEL)
    xs = x_sample.reshape(DEC_BATCH * DEC_SEQ, D_MODEL)

    cvec = jnp.concatenate([c_ctx[None, :], c, jnp.zeros((8 - 1 - DEC_BATCH, D_MODEL), F32)], axis=0)
    mod_all = _mod_vectors(cvec, mod_w, mod_b)

    tables = tuple(jnp.asarray(t) for t in _rope_tables(DEC_SEQ))

    outs = {}
    for l in range(DEPTH):
        mod_tab = mod_all[l, :1 + DEC_BATCH].reshape((1 + DEC_BATCH) * N_MOD, 1, D_MODEL)
        xc = _ffn(xc, mod_tab, False, 0, ffn_w_gate, ffn_w_up, ffn_w_down, l, 0)
        xs = _ffn(xs, mod_tab, True, 0, ffn_w_gate, ffn_w_up, ffn_w_down, l, 0)
        if l % 2 == 0:
            e = l // 2
            w = _even_weights(ev_w_in[e], mla_q_norm[e], mla_w_qb[e], mla_kv_norm[e], mla_w_kvb[e],
                              mla_nope_norm[e], mla_rope_norm[e], gqa_qk_norm[e], ev_w_out[e])
            qm, ckv, kr, qg, kg, vg = _even_proj(xc, mod_tab, w)
            km, v = _kv_expand(ckv, kr, w["w_kvb"], w["nope_g"], TM_PROJ)
            att = _attention(qm, qg, km, v, kg, vg, SEQ, SEQ)
            xc = _out_proj(xc, mod_tab, False, [att], w["w_out"])
            outs["ckv"], outs["krope"], outs["k"], outs["v"] = ckv, _unspread_rope(kr), kg, vg
            qm, ckv, kr, qg, kg, vg = _even_proj(xs, mod_tab, w, tables)
            km, v = _kv_expand(ckv, kr, w["w_kvb"], w["nope_g"], TM_PROJ)
            c_ckv = cache_mla_ckv[:, e].reshape(DEC_BATCH * PAST_LEN, MLA_KV_RANK)
            c_kr = _spread_rope(cache_mla_krope[:, e].reshape(DEC_BATCH * PAST_LEN, MLA_ROPE))
            kmc, vc = _kv_expand(c_ckv, c_kr, w["w_kvb"], w["nope_g"], PAST_LEN)
            kgc = cache_gqa_k[:, e].reshape(DEC_BATCH * PAST_LEN, GQA_KV_HEADS * HEAD_DIM)
            vgc = cache_gqa_v[:, e].reshape(DEC_BATCH * PAST_LEN, GQA_KV_HEADS * HEAD_DIM)
            att = _attention(qm, qg, km, v, kg, vg, DEC_SEQ, TQ_ATT, cache=(kmc, vc, kgc, vgc))
            xs = _out_proj(xs, mod_tab, True, [att], w["w_out"])
        else:
            o = l // 2
            w_out = od_w_out[o]
            logit = ret_decay_logit[o]
            ret_g = ret_norm[o].reshape(1, -1)
            skip = hy_skip[o].reshape(1, -1)
            conv_b = hy_conv_b[o].reshape(1, -1)
            filt_args = (hy_filt_w1[o], hy_filt_b1[o], hy_filt_w2[o], hy_filt_b2[o], hy_filt_freq[o],
                         hy_filt_w3[o], hy_decay[o])
            for stream in ("ctx", "smp"):
                x, seq, latent = (xc, SEQ, False) if stream == "ctx" else (xs, DEC_SEQ, True)
                fwd, inv = _dft_matrices(seq)
                fwd_b, inv_b = jnp.asarray(fwd).astype(BF16), jnp.asarray(inv).astype(BF16)
                spectra = _filter_spectra(seq, *filt_args, fwd_b, TC_HY)
                u = _odd_proj(x, mod_tab, latent, od_w_in, o)
                if stream == "ctx":
                    o_ret, st = _retention(u, logit, ret_g, seq, RET_HEADS)
                    outs["ret"] = st
                    o_hy = _hyena(u, seq, HY_CH, hy_conv_w[o], conv_b, spectra, skip, fwd_b, inv_b)
                else:
                    o_ret, _ = _retention(u, logit, ret_g, seq, 1, s0=state_ret[:, o])
                    o_hy = _hyena(u, seq, TC_HY, hy_conv_w[o], conv_b, spectra, skip, fwd_b, inv_b)
                x = _out_proj(x, mod_tab, latent, [o_ret, o_hy], w_out)
                if stream == "ctx":
                    xc = x
                else:
                    xs = x
        xc = _ffn(xc, mod_tab, False, 2, ffn_w_gate, ffn_w_up, ffn_w_down, l, 1)
        xs = _ffn(xs, mod_tab, True, 2, ffn_w_gate, ffn_w_up, ffn_w_down, l, 1)

    return (
        xc.reshape(BATCH, SEQ, D_MODEL),
        xs.reshape(DEC_BATCH, DEC_SEQ, D_MODEL),
        outs["ckv"].reshape(BATCH, 1, SEQ, MLA_KV_RANK),
        outs["krope"].reshape(BATCH, 1, SEQ, MLA_ROPE),
        outs["k"].reshape(BATCH, 1, SEQ, GQA_KV_HEADS, HEAD_DIM),
        outs["v"].reshape(BATCH, 1, SEQ, GQA_KV_HEADS, HEAD_DIM),
        outs["ret"].reshape(BATCH, 1, 2, RET_HEADS, RET_DK, RET_DV),
    )
```

```python
import functools
import math

import numpy as np
import jax
import jax.numpy as jnp
from jax import lax
from jax.experimental import pallas as pl
from jax.experimental.pallas import tpu as pltpu

F32 = jnp.float32
BF16 = jnp.bfloat16

D_MODEL = 2048
BATCH = 32
SEQ = 256
DEPTH = 2
DEC_BATCH = 2
DEC_SEQ = 1024
PAST_LEN = 256
GRID_W = 64
N_MOD = 9
D_FF = 5632
ROPE_THETA = 10000.0
EPS = 1e-6

MLA_HEADS = 8
MLA_Q_RANK = 512
MLA_KV_RANK = 512
MLA_NOPE = 128
MLA_ROPE = 64
MLA_V = 128
GQA_HEADS = 8
GQA_KV_HEADS = 2
GQA_GROUP = GQA_HEADS // GQA_KV_HEADS
HEAD_DIM = 128
MLA_SCALE = (MLA_NOPE + MLA_ROPE) ** -0.5
GQA_SCALE = HEAD_DIM ** -0.5
MLA_QK_PAD = 256

RET_HEADS = 8
RET_DK = 128
RET_DV = 128
HY_CH = 1024
HY_ORDER = 2
HY_BANDS = 16
HY_FHID = 64
ODD_IN = 4 * RET_HEADS * RET_DK + (HY_ORDER + 1) * HY_CH

V7X_VMEM_BYTES = 64 * 1024 * 1024
LANES = 128
VMEM_LIMIT = V7X_VMEM_BYTES - 3 * 1024 * 1024

TM_FFN = 1024
TF_FFN = 512
TF_SUB = 256
TM_PROJ = 512
TM_ODD = 1024
TN_MOD = 1024
TN_ODD = 1024
TQ_ATT = 256
TC_HY = 256


def _cparams(*sem):
    return pltpu.CompilerParams(dimension_semantics=sem, vmem_limit_bytes=VMEM_LIMIT)


def _const_spec(shape):
    nd = len(shape)
    return pl.BlockSpec(shape, lambda *_: (0,) * nd)


def _silu(x):
    return x * jax.nn.sigmoid(x)


def _rms_rows(x, width=None):
    ss = jnp.sum(x * x, axis=-1, keepdims=True)
    n = x.shape[-1] if width is None else width
    return x * lax.rsqrt(ss * (1.0 / n) + EPS)


def _mod_kernel(c_ref, w_ref, b_ref, o_ref):
    s = _silu(c_ref[...]).astype(BF16)
    o_ref[0] = jnp.dot(s, w_ref[0].astype(BF16), preferred_element_type=F32) + b_ref[0]


def _mod_vectors(cvec, mod_w, mod_b):
    depth, d, n = mod_w.shape
    rows = cvec.shape[0]
    return pl.pallas_call(
        _mod_kernel,
        out_shape=jax.ShapeDtypeStruct((depth, rows, n), F32),
        grid=(depth, n // TN_MOD),
        in_specs=[
            pl.BlockSpec((rows, d), lambda l, j: (0, 0)),
            pl.BlockSpec((1, d, TN_MOD), lambda l, j: (l, 0, j)),
            pl.BlockSpec((1, 1, TN_MOD), lambda l, j: (l, 0, j)),
        ],
        out_specs=pl.BlockSpec((1, rows, TN_MOD), lambda l, j: (l, 0, j)),
        compiler_params=_cparams("parallel", "parallel"),
        name="mod_vectors",
    )(cvec, mod_w, mod_b.reshape(depth, 1, n))


def _mod_specs(sub, latent, tile, extra_axes=0):
    group_of_tile = (lambda i: 1 + (i * tile) // DEC_SEQ) if latent else (lambda i: 0)

    def spec(k):
        if extra_axes:
            return pl.BlockSpec((1, 1, D_MODEL), lambda i, j: (group_of_tile(i) * N_MOD + 3 * sub + k, 0, 0))
        return pl.BlockSpec((1, 1, D_MODEL), lambda i: (group_of_tile(i) * N_MOD + 3 * sub + k, 0, 0))
    return [spec(0), spec(1), spec(2)]


def _modulated(x, sh_ref, sc_ref):
    return (_rms_rows(x) * (1.0 + sc_ref[0]) + sh_ref[0]).astype(BF16)


def _ffn_kernel(x_hbm, sh_ref, sc_ref, gt_ref, wg_ref, wu_ref, wd_ref, o_ref, x_buf, h_ref, x_sem):
    i, f = pl.program_id(0), pl.program_id(1)
    tm = x_buf.shape[0]
    half_gate = 0.5 * gt_ref[0]

    def x_copy(tile):
        return pltpu.make_async_copy(x_hbm.at[pl.ds(tile * tm, tm), :], x_buf, x_sem)

    def partial_out(h):
        out = None
        for c in range(TF_FFN // TF_SUB):
            cs = slice(c * TF_SUB, (c + 1) * TF_SUB)
            g = jnp.dot(h, wg_ref[:, cs].astype(BF16), preferred_element_type=F32)
            u = jnp.dot(h, wu_ref[:, cs].astype(BF16), preferred_element_type=F32)
            a = (_silu(g) * u).astype(BF16)
            t = jnp.dot(a, wd_ref[cs, :].astype(BF16), preferred_element_type=F32)
            out = t if out is None else out + t
        return half_gate * out

    @pl.when(f == 0)
    def _():
        @pl.when(i == 0)
        def _():
            x_copy(0).start()

        x_copy(i).wait()
        x = x_buf[...]
        h = _modulated(x, sh_ref, sc_ref)
        h_ref[...] = h
        o_ref[...] = x + partial_out(h)

    @pl.when(f > 0)
    def _():
        @pl.when(jnp.logical_and(f == 1, i + 1 < pl.num_programs(0)))
        def _():
            x_copy(i + 1).start()

        o_ref[...] += partial_out(h_ref[...])


def _ffn(x, mod_tab, latent, sub, w_gate, w_up, w_down, layer, j):
    m, d = x.shape
    dff = w_gate.shape[-1]
    assert dff // TF_FFN >= 2
    return pl.pallas_call(
        _ffn_kernel,
        out_shape=jax.ShapeDtypeStruct((m, d), F32),
        grid=(m // TM_FFN, dff // TF_FFN),
        in_specs=[pl.BlockSpec(memory_space=pl.ANY)]
        + _mod_specs(sub, latent, TM_FFN, extra_axes=1)
        + [
            pl.BlockSpec((None, None, d, TF_FFN), lambda i, f: (layer, j, 0, f)),
            pl.BlockSpec((None, None, d, TF_FFN), lambda i, f: (layer, j, 0, f)),
            pl.BlockSpec((None, None, TF_FFN, d), lambda i, f: (layer, j, f, 0)),
        ],
        out_specs=pl.BlockSpec((TM_FFN, d), lambda i, f: (i, 0)),
        scratch_shapes=[pltpu.VMEM((TM_FFN, d), F32), pltpu.VMEM((TM_FFN, d), BF16), pltpu.SemaphoreType.DMA(())],
        compiler_params=_cparams("arbitrary", "arbitrary"),
        name="ffn",
    )(x, mod_tab, mod_tab, mod_tab, w_gate, w_up, w_down)


def _out_proj_kernel(*refs, n_parts):
    x_ref, gt_ref, w_ref = refs[0], refs[1], refs[2]
    part_refs = refs[3:3 + n_parts]
    o_ref, wb_ref = refs[3 + n_parts], refs[4 + n_parts]

    @pl.when(pl.program_id(0) == 0)
    def _():
        wb_ref[...] = w_ref[...].astype(BF16)

    acc, off = None, 0
    for p_ref in part_refs:
        width = p_ref.shape[1]
        t = jnp.dot(p_ref[...], wb_ref[off:off + width, :], preferred_element_type=F32)
        acc = t if acc is None else acc + t
        off += width
    o_ref[...] = x_ref[...] + gt_ref[0] * acc


def _out_proj(x, mod_tab, latent, parts, w_out):
    m, d = x.shape
    return pl.pallas_call(
        functools.partial(_out_proj_kernel, n_parts=len(parts)),
        out_shape=jax.ShapeDtypeStruct((m, d), F32),
        grid=(m // TM_PROJ,),
        in_specs=[pl.BlockSpec((TM_PROJ, d), lambda i: (i, 0)), _mod_specs(1, latent, TM_PROJ)[2],
                  pl.BlockSpec(w_out.shape, lambda i: (0, 0), pipeline_mode=pl.Buffered(1))]
        + [pl.BlockSpec((TM_PROJ, p.shape[1]), lambda i: (i, 0)) for p in parts],
        out_specs=pl.BlockSpec((TM_PROJ, d), lambda i: (i, 0)),
        scratch_shapes=[pltpu.VMEM(w_out.shape, BF16)],
        compiler_params=_cparams("arbitrary"),
        name="mixer_out_proj",
    )(x, mod_tab, w_out, *parts)


EV_QA, EV_KVA, EV_QG, EV_KG, EV_VG, EV_KR, EV_END = 0, 512, 1024, 2048, 2304, 2560, 2688


def _rotate_half(x, cos, sin):
    return x * cos + pltpu.roll(x, LANES // 2, axis=1) * sin


def _even_proj_kernel(*refs, rotary):
    (x_ref, sh_ref, sc_ref, w_in_ref, w_qb_ref, qn_ref, kvn_ref, ng_ref, rg_ref, gg_ref) = refs[:10]
    if rotary:
        cm_ref, sm_ref, cg_ref, sg_ref = refs[10:14]
        rot_m = lambda t: _rotate_half(t, cm_ref[...], sm_ref[...])
        rot_g = lambda t: _rotate_half(t, cg_ref[...], sg_ref[...])
    else:
        rot_m = rot_g = lambda t: t
    qm_ref, ckv_ref, kr_ref, qg_ref, kg_ref, vg_ref = refs[-6:]

    h = _modulated(x_ref[...], sh_ref, sc_ref)
    u = jnp.dot(h, w_in_ref[...], preferred_element_type=F32)

    nope_gain = ng_ref[0:1, :] * MLA_SCALE
    rope_gain = rg_ref[0:1, :] * MLA_SCALE
    gqa_gain = gg_ref[0:1, :] * GQA_SCALE

    qa = (_rms_rows(u[:, EV_QA:EV_KVA]) * qn_ref[...]).astype(BF16)
    q = jnp.dot(qa, w_qb_ref[...], preferred_element_type=F32)
    for hd in range(MLA_HEADS):
        c0 = hd * MLA_QK_PAD
        nope = _rms_rows(q[:, c0:c0 + MLA_NOPE]) * nope_gain
        rope = _rms_rows(q[:, c0 + MLA_NOPE:c0 + MLA_QK_PAD], MLA_ROPE) * rope_gain
        qm_ref[:, c0:c0 + MLA_NOPE] = nope.astype(BF16)
        qm_ref[:, c0 + MLA_NOPE:c0 + MLA_QK_PAD] = rot_m(rope).astype(BF16)

    ckv_ref[...] = _rms_rows(u[:, EV_KVA:EV_QG]) * kvn_ref[...]
    kr_ref[...] = rot_m(_rms_rows(u[:, EV_KR:EV_END], MLA_ROPE) * rg_ref[1:2, :])

    for hd in range(GQA_HEADS):
        c0 = EV_QG + hd * HEAD_DIM
        qh = _rms_rows(u[:, c0:c0 + HEAD_DIM]) * gqa_gain
        qg_ref[:, hd * HEAD_DIM:(hd + 1) * HEAD_DIM] = rot_g(qh).astype(BF16)
    for hd in range(GQA_KV_HEADS):
        c0 = EV_KG + hd * HEAD_DIM
        kh = _rms_rows(u[:, c0:c0 + HEAD_DIM]) * gg_ref[1:2, :]
        kg_ref[:, hd * HEAD_DIM:(hd + 1) * HEAD_DIM] = rot_g(kh)
    vg_ref[...] = u[:, EV_VG:EV_KR]


def _even_proj(x, mod_tab, w, tables=None):
    m, d = x.shape
    tm = TM_PROJ
    row = lambda w_: pl.BlockSpec((tm, w_), lambda i: (i, 0))
    tab_specs, tab_args = [], []
    if tables is not None:
        tiles_per_seq = tables[0].shape[0] // tm
        tab_specs = [pl.BlockSpec((tm, LANES), lambda i: (i % tiles_per_seq, 0))] * 4
        tab_args = list(tables)
    return pl.pallas_call(
        functools.partial(_even_proj_kernel, rotary=tables is not None),
        out_shape=[
            jax.ShapeDtypeStruct((m, MLA_HEADS * MLA_QK_PAD), BF16),
            jax.ShapeDtypeStruct((m, MLA_KV_RANK), F32),
            jax.ShapeDtypeStruct((m, LANES), F32),
            jax.ShapeDtypeStruct((m, GQA_HEADS * HEAD_DIM), BF16),
            jax.ShapeDtypeStruct((m, GQA_KV_HEADS * HEAD_DIM), F32),
            jax.ShapeDtypeStruct((m, GQA_KV_HEADS * HEAD_DIM), F32),
        ],
        grid=(m // tm,),
        in_specs=[row(d)] + _mod_specs(1, tables is not None, tm)[:2]
        + [_const_spec(w["w_in"].shape), _const_spec(w["w_qb"].shape), _const_spec(w["q_norm"].shape),
           _const_spec(w["kv_norm"].shape), _const_spec(w["nope_g"].shape), _const_spec(w["rope_g"].shape),
           _const_spec(w["gqa_g"].shape)] + tab_specs,
        out_specs=[row(MLA_HEADS * MLA_QK_PAD), row(MLA_KV_RANK), row(LANES), row(GQA_HEADS * HEAD_DIM),
                   row(GQA_KV_HEADS * HEAD_DIM), row(GQA_KV_HEADS * HEAD_DIM)],
        compiler_params=_cparams("parallel"),
        name="even_proj",
    )(x, mod_tab, mod_tab, w["w_in"], w["w_qb"], w["q_norm"], w["kv_norm"], w["nope_g"], w["rope_g"], w["gqa_g"],
      *tab_args)


def _kv_expand_kernel(ckv_ref, kr_ref, w_ref, ng_ref, km_ref, v_ref):
    kv = jnp.dot(ckv_ref[...].astype(BF16), w_ref[...], preferred_element_type=F32)
    kr = kr_ref[...].astype(BF16)
    for hd in range(MLA_HEADS):
        c0 = hd * (MLA_NOPE + MLA_V)
        km_ref[:, hd * MLA_QK_PAD:hd * MLA_QK_PAD + MLA_NOPE] = (
            _rms_rows(kv[:, c0:c0 + MLA_NOPE]) * ng_ref[1:2, :]).astype(BF16)
        km_ref[:, hd * MLA_QK_PAD + MLA_NOPE:(hd + 1) * MLA_QK_PAD] = kr
        v_ref[:, hd * MLA_V:(hd + 1) * MLA_V] = kv[:, c0 + MLA_NOPE:c0 + MLA_NOPE + MLA_V].astype(BF16)


def _kv_expand(ckv, kr, w_kvb_bf16, nope_g, tr):
    r = ckv.shape[0]
    row = lambda w_: pl.BlockSpec((tr, w_), lambda i: (i, 0))
    return pl.pallas_call(
        _kv_expand_kernel,
        out_shape=[jax.ShapeDtypeStruct((r, MLA_HEADS * MLA_QK_PAD), BF16),
                   jax.ShapeDtypeStruct((r, MLA_HEADS * MLA_V), BF16)],
        grid=(r // tr,),
        in_specs=[row(MLA_KV_RANK), row(LANES), _const_spec(w_kvb_bf16.shape), _const_spec(nope_g.shape)],
        out_specs=[row(MLA_HEADS * MLA_QK_PAD), row(MLA_HEADS * MLA_V)],
        compiler_params=_cparams("parallel"),
        name="kv_expand",
    )(ckv, kr, w_kvb_bf16, nope_g)


def _softmax_pv(score_list, value_list):
    mx = None
    for s in score_list:
        m_ = jnp.max(s, axis=-1, keepdims=True)
        mx = m_ if mx is None else jnp.maximum(mx, m_)
    den, acc = None, None
    for s, v in zip(score_list, value_list):
        p = jnp.exp(s - mx)
        l_ = jnp.sum(p, axis=-1, keepdims=True)
        o_ = jnp.dot(p.astype(BF16), v, preferred_element_type=F32)
        den = l_ if den is None else den + l_
        acc = o_ if acc is None else acc + o_
    return acc / den


def _qk(q, k):
    return lax.dot_general(q, k, (((1,), (1,)), ((), ())), preferred_element_type=F32)


def _attn_kernel(*refs, has_cache):
    if has_cache:
        qm_ref, qg_ref, km_ref, v_ref, kg_ref, vg_ref, kmc_ref, vc_ref, kgc_ref, vgc_ref, o_ref = refs
    else:
        qm_ref, qg_ref, km_ref, v_ref, kg_ref, vg_ref, o_ref = refs
    for hd in range(MLA_HEADS):
        qs = slice(hd * MLA_QK_PAD, (hd + 1) * MLA_QK_PAD)
        vs = slice(hd * MLA_V, (hd + 1) * MLA_V)
        q = qm_ref[:, qs]
        scores = [_qk(q, km_ref[:, qs])]
        values = [v_ref[:, vs]]
        if has_cache:
            scores.append(_qk(q, kmc_ref[:, qs]))
            values.append(vc_ref[:, vs])
        o_ref[:, vs] = _softmax_pv(scores, values).astype(o_ref.dtype)
    base = MLA_HEADS * MLA_V
    for kvh in range(GQA_KV_HEADS):
        ks = slice(kvh * HEAD_DIM, (kvh + 1) * HEAD_DIM)
        k = kg_ref[:, ks].astype(BF16)
        v = vg_ref[:, ks].astype(BF16)
        if has_cache:
            kc = kgc_ref[:, ks].astype(BF16)
            vc = vgc_ref[:, ks].astype(BF16)
        for g in range(GQA_GROUP):
            hd = kvh * GQA_GROUP + g
            hs = slice(hd * HEAD_DIM, (hd + 1) * HEAD_DIM)
            q = qg_ref[:, hs]
            scores, values = [_qk(q, k)], [v]
            if has_cache:
                scores.append(_qk(q, kc))
                values.append(vc)
            o_ref[:, base + hd * HEAD_DIM:base + (hd + 1) * HEAD_DIM] = _softmax_pv(scores, values).astype(o_ref.dtype)


def _attention(qm, qg, km, v, kg, vg, seq, tq, cache=None):
    m = qm.shape[0]
    nq = seq // tq
    qrow = lambda w_: pl.BlockSpec((tq, w_), lambda i: (i, 0))
    krow = lambda w_: pl.BlockSpec((seq, w_), lambda i: (i // nq, 0))
    in_specs = [qrow(qm.shape[1]), qrow(qg.shape[1]), krow(km.shape[1]), krow(v.shape[1]),
                krow(kg.shape[1]), krow(vg.shape[1])]
    args = [qm, qg, km, v, kg, vg]
    if cache is not None:
        past = cache[0].shape[0] // (m // seq)
        crow = lambda w_: pl.BlockSpec((past, w_), lambda i: (i // nq, 0))
        in_specs += [crow(c.shape[1]) for c in cache]
        args += list(cache)
    width = MLA_HEADS * MLA_V + GQA_HEADS * HEAD_DIM
    return pl.pallas_call(
        functools.partial(_attn_kernel, has_cache=cache is not None),
        out_shape=jax.ShapeDtypeStruct((m, width), BF16),
        grid=(m // tq,),
        in_specs=in_specs,
        out_specs=qrow(width),
        compiler_params=_cparams("parallel"),
        name="attention",
    )(*args)


def _odd_proj_kernel(x_ref, sh_ref, sc_ref, w_ref, o_ref, h_ref):
    def project(h):
        return jnp.dot(h, w_ref[...].astype(BF16), preferred_element_type=F32).astype(o_ref.dtype)

    @pl.when(pl.program_id(1) == 0)
    def _():
        h = _modulated(x_ref[...], sh_ref, sc_ref)
        h_ref[...] = h
        o_ref[...] = project(h)

    @pl.when(pl.program_id(1) > 0)
    def _():
        o_ref[...] = project(h_ref[...])


def _odd_proj(x, mod_tab, latent, w_in, layer):
    m, d = x.shape
    n = w_in.shape[-1]
    return pl.pallas_call(
        _odd_proj_kernel,
        out_shape=jax.ShapeDtypeStruct((m, n), BF16),
        grid=(m // TM_ODD, n // TN_ODD),
        in_specs=[pl.BlockSpec((TM_ODD, d), lambda i, j: (i, 0))]
        + _mod_specs(1, latent, TM_ODD, extra_axes=1)[:2]
        + [pl.BlockSpec((None, d, TN_ODD), lambda i, j: (layer, 0, j))],
        out_specs=pl.BlockSpec((TM_ODD, TN_ODD), lambda i, j: (i, j)),
        scratch_shapes=[pltpu.VMEM((TM_ODD, d), BF16)],
        compiler_params=_cparams("parallel", "arbitrary"),
        name="odd_proj",
    )(x, mod_tab, mod_tab, w_in)


def _log_sigmoid(x):
    return jnp.minimum(x, 0.0) - jnp.log1p(jnp.exp(-jnp.abs(x)))


def _retention_kernel(*refs, heads, has_init):
    if has_init:
        logit_ref, q_ref, k_ref, v_ref, g_ref, rg_ref, s0_ref, o_ref, dec_ref, vec_ref = refs
    else:
        logit_ref, q_ref, k_ref, v_ref, g_ref, rg_ref, o_ref, st_ref, dec_ref, vec_ref = refs
    length = q_ref.shape[0]
    h0 = pl.program_id(0) * heads
    scale = RET_DK ** -0.5
    log_g = _log_sigmoid(logit_ref[...])
    lane_h = lax.broadcasted_iota(jnp.int32, (2, RET_HEADS), 1)

    def head_log_decays(hh):
        lg = jnp.sum(jnp.where(lane_h == h0 + hh, log_g, 0.0), axis=1, keepdims=True)
        return lg[0:1, :], lg[1:2, :]

    @pl.when(pl.program_id(1) == 0)
    def _():
        n_i = lax.broadcasted_iota(jnp.int32, (length, length), 0)
        m_i = lax.broadcasted_iota(jnp.int32, (length, length), 1)
        diff = (n_i - m_i).astype(F32)
        pos = lax.broadcasted_iota(jnp.int32, (length, RET_DK), 0).astype(F32)
        for hh in range(heads):
            lg_f, lg_b = head_log_decays(hh)
            dec_ref[hh] = scale * (jnp.where(diff >= 0, jnp.exp(jnp.maximum(diff, 0.0) * lg_f), 0.0)
                                   + jnp.where(diff <= 0, jnp.exp(jnp.maximum(-diff, 0.0) * lg_b), 0.0))
            if has_init:
                vec_ref[4 * hh + 2] = jnp.exp((pos + 1.0) * lg_f)
                vec_ref[4 * hh + 3] = jnp.exp((length - pos) * lg_b)
            else:
                vec_ref[4 * hh + 0] = jnp.exp((length - 1.0 - pos) * lg_f) * scale
                vec_ref[4 * hh + 1] = jnp.exp(pos * lg_b) * scale

    tn = (((0,), (0,)), ((), ()))
    for hh in range(heads):
        cs = slice(hh * RET_DK, (hh + 1) * RET_DK)
        q, k, v = q_ref[:, cs], k_ref[:, cs], v_ref[:, cs]
        att = _qk(q, k) * dec_ref[hh]
        o = jnp.dot(att.astype(BF16), v, preferred_element_type=F32)
        if has_init:
            s0_f, s0_b = s0_ref[0, 0, hh].astype(BF16), s0_ref[0, 1, hh].astype(BF16)
            o = o + jnp.dot(q, s0_f, preferred_element_type=F32) * vec_ref[4 * hh + 2]
            o = o + jnp.dot(q, s0_b, preferred_element_type=F32) * vec_ref[4 * hh + 3]
        else:
            kf = k.astype(F32)
            k_f, k_b = (kf * vec_ref[4 * hh + 0]).astype(BF16), (kf * vec_ref[4 * hh + 1]).astype(BF16)
            st_ref[0, 0, hh] = lax.dot_general(k_f, v, tn, preferred_element_type=F32)
            st_ref[0, 1, hh] = lax.dot_general(k_b, v, tn, preferred_element_type=F32)
        gate = g_ref[:, cs].astype(F32)
        o_ref[:, cs] = (_rms_rows(o) * rg_ref[:, cs] * _silu(gate)).astype(o_ref.dtype)


def _retention(u, decay_logit, ret_g, seq, heads, s0=None):
    m = u.shape[0]
    b = m // seq
    hw = heads * RET_DK
    nh = RET_HEADS // heads
    blocks_per_part = RET_HEADS * RET_DK // hw
    col = lambda part: pl.BlockSpec((seq, hw), lambda j, i, part=part: (i, part * blocks_per_part + j))
    st_spec = pl.BlockSpec((1, 2, heads, RET_DK, RET_DV), lambda j, i: (i, 0, j, 0, 0))
    in_specs = [_const_spec(decay_logit.shape), col(0), col(1), col(2), col(3),
                pl.BlockSpec((1, hw), lambda j, i: (0, j))]
    args = [decay_logit, u, u, u, u, ret_g]
    if s0 is not None:
        in_specs.append(st_spec)
        args.append(s0)
    out_shape = [jax.ShapeDtypeStruct((m, RET_HEADS * RET_DV), BF16)]
    out_specs = [pl.BlockSpec((seq, hw), lambda j, i: (i, j))]
    if s0 is None:
        out_shape.append(jax.ShapeDtypeStruct((b, 2, RET_HEADS, RET_DK, RET_DV), F32))
        out_specs.append(st_spec)
    return pl.pallas_call(
        functools.partial(_retention_kernel, heads=heads, has_init=s0 is not None),
        out_shape=out_shape,
        grid=(nh, b),
        in_specs=in_specs,
        out_specs=out_specs,
        scratch_shapes=[pltpu.VMEM((heads, seq, seq), F32), pltpu.VMEM((4 * heads, seq, RET_DK), F32)],
        compiler_params=_cparams("parallel", "arbitrary"),
        name="retention",
    )(*args)


@functools.lru_cache(maxsize=None)
def _dft_matrices(length):
    n = 2 * length
    f = np.arange(length, dtype=np.float64)[:, None]
    t = np.arange(length, dtype=np.float64)[None, :]
    ang = 2.0 * np.pi * f * t / n
    fwd_a = np.cos(ang)
    fwd_b = np.sin(ang)
    fwd_b[0, :] = np.cos(np.pi * t[0])
    fwd = np.concatenate([fwd_a, fwd_b], axis=0)
    tt = (np.arange(length, dtype=np.float64) + length // 2)[:, None]
    ff = np.arange(length, dtype=np.float64)[None, :]
    ang_i = 2.0 * np.pi * ff * tt / n
    inv_a = 2.0 * np.cos(ang_i) / n
    inv_b = 2.0 * np.sin(ang_i) / n
    inv_a[:, 0] = 1.0 / n
    inv_b[:, 0] = np.cos(np.pi * tt[:, 0]) / n
    inv = np.concatenate([inv_a, inv_b], axis=1)
    return fwd.astype(np.float32), inv.astype(np.float32)


@functools.lru_cache(maxsize=None)
def _filter_features(length):
    t = np.arange(length, dtype=np.float64)
    tn = t / length
    bands = np.arange(1, HY_BANDS + 1, dtype=np.float64)
    ang = 2.0 * np.pi * tn[:, None] * bands[None, :]
    feat = np.concatenate([tn[:, None], np.sin(ang), np.cos(ang)], axis=-1)
    feat = np.pad(feat, ((0, 0), (0, LANES - feat.shape[1])))
    r = (np.abs(t - length // 2) / (length / 2))[:, None]
    return feat.astype(np.float32), r.astype(np.float32)


def _filter_spec_kernel(feat_ref, r_ref, w1_ref, b1_ref, w2_ref, b2_ref, fr_ref, w3_ref, dec_ref, fwd_ref, o_ref,
                        z_ref):
    hp = lax.Precision.HIGHEST

    @pl.when(pl.program_id(0) == 0)
    def _():
        z = jnp.sin(fr_ref[0:1, :] * (jnp.dot(feat_ref[...], w1_ref[...], precision=hp, preferred_element_type=F32)
                                      + b1_ref[...]))
        z_ref[...] = jnp.sin(fr_ref[1:2, :] * (jnp.dot(z, w2_ref[...], precision=hp, preferred_element_type=F32)
                                               + b2_ref[...]))

    filt = jnp.dot(z_ref[...], w3_ref[...], precision=hp, preferred_element_type=F32)
    filt = filt * jnp.exp(-r_ref[...] * jnp.abs(dec_ref[...]))
    o_ref[...] = jnp.dot(fwd_ref[...], filt.astype(BF16), preferred_element_type=F32)


def _filter_spectra(length, w1, b1, w2, b2, freq, w3, decay, fwd_bf16, tc):
    feat, r = _filter_features(length)
    nch = w3.shape[1]
    w1p = jnp.pad(w1, ((0, LANES - w1.shape[0]), (0, 0)))
    return pl.pallas_call(
        _filter_spec_kernel,
        out_shape=jax.ShapeDtypeStruct((2 * length, nch), F32),
        grid=(nch // tc,),
        in_specs=[_const_spec(feat.shape), _const_spec(r.shape), _const_spec(w1p.shape), _const_spec((1, HY_FHID)),
                  _const_spec(w2.shape), _const_spec((1, HY_FHID)), _const_spec(freq.shape),
                  pl.BlockSpec((HY_FHID, tc), lambda j: (0, j)), pl.BlockSpec((1, tc), lambda j: (0, j)),
                  _const_spec(fwd_bf16.shape)],
        out_specs=pl.BlockSpec((2 * length, tc), lambda j: (0, j)),
        scratch_shapes=[pltpu.VMEM((length, HY_FHID), F32)],
        compiler_params=_cparams("arbitrary"),
        name="hyena_filter_spectra",
    )(jnp.asarray(feat), jnp.asarray(r), w1p, b1.reshape(1, -1), w2, b2.reshape(1, -1), freq, w3,
      decay.reshape(1, -1), fwd_bf16)


def _hyena_kernel(v_ref, x1_ref, x2_ref, cwv_ref, cw1_ref, cw2_ref, cbv_ref, cb1_ref, cb2_ref,
                  h1_ref, h2_ref, sk1_ref, sk2_ref, fwd_ref, inv_ref, o_ref):
    length, tc = v_ref.shape
    row = lax.broadcasted_iota(jnp.int32, (length, tc), 0)
    first, last = row == 0, row == length - 1

    def short_conv(x_ref, w_ref, b_ref):
        x = x_ref[...].astype(F32)
        prev = jnp.where(first, 0.0, pltpu.roll(x, 1, axis=0))
        nxt = jnp.where(last, 0.0, pltpu.roll(x, length - 1, axis=0))
        return prev * w_ref[0:1, :] + x * w_ref[1:2, :] + nxt * w_ref[2:3, :] + b_ref[...]

    def long_conv(z, h_ref):
        zs = jnp.dot(fwd_ref[...], z.astype(BF16), preferred_element_type=F32)
        za, zb = zs[:length], zs[length:]
        ha, hb = h_ref[:length, :], h_ref[length:, :]
        bb = zb * hb
        ya = za * ha - jnp.where(first, 0.0, bb)
        yb = jnp.where(first, bb, za * hb + zb * ha)
        y = jnp.concatenate([ya, yb], axis=0).astype(BF16)
        return jnp.dot(inv_ref[...], y, preferred_element_type=F32)

    v = short_conv(v_ref, cwv_ref, cbv_ref)
    x1 = short_conv(x1_ref, cw1_ref, cb1_ref)
    x2 = short_conv(x2_ref, cw2_ref, cb2_ref)
    z = x1 * (long_conv(v, h1_ref) + sk1_ref[...] * v)
    o_ref[...] = (x2 * (long_conv(z, h2_ref) + sk2_ref[...] * z)).astype(o_ref.dtype)


def _hyena(u, seq, tc, conv_w, conv_b, spectra, skip, fwd_bf16, inv_bf16):
    m = u.shape[0]
    b = m // seq
    nc = HY_CH // tc
    hy0 = 4 * RET_HEADS * RET_DK // tc
    ucol = lambda part: pl.BlockSpec((seq, tc), lambda i, j, part=part: (i, hy0 + part * nc + j))
    wcol = lambda rows, part: pl.BlockSpec((rows, tc), lambda i, j, part=part: (0, part * nc + j))
    return pl.pallas_call(
        _hyena_kernel,
        out_shape=jax.ShapeDtypeStruct((m, HY_CH), BF16),
        grid=(b, nc),
        in_specs=[ucol(0), ucol(1), ucol(2), wcol(3, 0), wcol(3, 1), wcol(3, 2), wcol(1, 0), wcol(1, 1), wcol(1, 2),
                  wcol(2 * seq, 0), wcol(2 * seq, 1), wcol(1, 0), wcol(1, 1),
                  _const_spec(fwd_bf16.shape), _const_spec(inv_bf16.shape)],
        out_specs=pl.BlockSpec((seq, tc), lambda i, j: (i, j)),
        compiler_params=_cparams("parallel", "parallel"),
        name="hyena",
    )(u, u, u, conv_w, conv_w, conv_w, conv_b, conv_b, conv_b, spectra, spectra, skip, skip, fwd_bf16, inv_bf16)


@functools.lru_cache(maxsize=None)
def _rope_tables(length):
    rows = np.repeat(np.arange(length // GRID_W, dtype=np.float64), GRID_W)
    cols = np.tile(np.arange(GRID_W, dtype=np.float64), length // GRID_W)

    def pack(dim):
        half = dim // 2
        freq = ROPE_THETA ** (-np.arange(0, half, 2, dtype=np.float64) / half)
        ang = np.concatenate([rows[:, None] * freq[None], cols[:, None] * freq[None]], axis=-1)
        pad = ((0, 0), (0, LANES // 2 - half))
        cos = np.pad(np.cos(ang), pad, constant_values=1.0)
        sin = np.pad(np.sin(ang), pad)
        return (np.concatenate([cos, cos], axis=-1).astype(np.float32),
                np.concatenate([-sin, sin], axis=-1).astype(np.float32))

    cos_m, sin_m = pack(MLA_ROPE)
    cos_g, sin_g = pack(HEAD_DIM)
    return cos_m, sin_m, cos_g, sin_g


def _spread_rope(a):
    half = MLA_ROPE // 2
    z = jnp.zeros(a.shape[:-1] + (LANES // 2 - half,), a.dtype)
    return jnp.concatenate([a[..., :half], z, a[..., half:], z], axis=-1)


def _unspread_rope(a):
    half = MLA_ROPE // 2
    return jnp.concatenate([a[..., :half], a[..., LANES // 2:LANES // 2 + half]], axis=-1)


def _even_weights(w_in, q_norm, w_qb, kv_norm, w_kvb, nope_g, rope_g, gqa_g, w_out):
    q_a, kv_a, kr, qg, kg, vg = jnp.split(w_in, np.cumsum([512, 512, 64, 1024, 256]).tolist(), axis=1)
    w_in_p = jnp.concatenate([q_a, kv_a, qg, kg, vg, _spread_rope(kr)], axis=1)
    wq = w_qb.reshape(MLA_Q_RANK, MLA_HEADS, MLA_NOPE + MLA_ROPE)
    wq = jnp.concatenate([wq[..., :MLA_NOPE], _spread_rope(wq[..., MLA_NOPE:])], axis=-1)
    return {
        "w_in": w_in_p.astype(BF16),
        "w_qb": wq.reshape(MLA_Q_RANK, MLA_HEADS * MLA_QK_PAD).astype(BF16),
        "w_kvb": w_kvb.astype(BF16),
        "w_out": w_out,
        "q_norm": q_norm.reshape(1, -1),
        "kv_norm": kv_norm.reshape(1, -1),
        "nope_g": nope_g,
        "rope_g": _spread_rope(rope_g),
        "gqa_g": gqa_g,
    }


def kernel(x_prompt, x_sample, cache_mla_ckv, cache_mla_krope, cache_gqa_k, cache_gqa_v, state_ret, c, c_ctx, mod_w, mod_b, ffn_w_gate, ffn_w_up, ffn_w_down, ev_w_in, mla_q_norm, mla_w_qb, mla_kv_norm, mla_w_kvb, mla_nope_norm, mla_rope_norm, gqa_qk_norm, ev_w_out, od_w_in, ret_decay_logit, ret_norm, hy_conv_w, hy_conv_b, hy_filt_w1, hy_filt_b1, hy_filt_w2, hy_filt_b2, hy_filt_freq, hy_filt_w3, hy_decay, hy_skip, od_w_out):
    xc = x_prompt.reshape(BATCH * SEQ, D_MODEL)
    xs = x_sample.reshape(DEC_BATCH * DEC_SEQ, D_MODEL)

    cvec = jnp.concatenate([c_ctx[None, :], c, jnp.zeros((8 - 1 - DEC_BATCH, D_MODEL), F32)], axis=0)
    mod_all = _mod_vectors(cvec, mod_w, mod_b)

    tables = tuple(jnp.asarray(t) for t in _rope_tables(DEC_SEQ))

    outs = {}
    for l in range(DEPTH):
        mod_tab = mod_all[l, :1 + DEC_BATCH].reshape((1 + DEC_BATCH) * N_MOD, 1, D_MODEL)
        xc = _ffn(xc, mod_tab, False, 0, ffn_w_gate, ffn_w_up, ffn_w_down, l, 0)
        xs = _ffn(xs, mod_tab, True, 0, ffn_w_gate, ffn_w_up, ffn_w_down, l, 0)
        if l % 2 == 0:
            e = l // 2
            w = _even_weights(ev_w_in[e], mla_q_norm[e], mla_w_qb[e], mla_kv_norm[e], mla_w_kvb[e],
                              mla_nope_norm[e], mla_rope_norm[e], gqa_qk_norm[e], ev_w_out[e])
            qm, ckv, kr, qg, kg, vg = _even_proj(xc, mod_tab, w)
            km, v = _kv_expand(ckv, kr, w["w_kvb"], w["nope_g"], TM_PROJ)
            att = _attention(qm, qg, km, v, kg, vg, SEQ, SEQ)
            xc = _out_proj(xc, mod_tab, False, [att], w["w_out"])
            outs["ckv"], outs["krope"], outs["k"], outs["v"] = ckv, _unspread_rope(kr), kg, vg
            qm, ckv, kr, qg, kg, vg = _even_proj(xs, mod_tab, w, tables)
            km, v = _kv_expand(ckv, kr, w["w_kvb"], w["nope_g"], TM_PROJ)
            c_ckv = cache_mla_ckv[:, e].reshape(DEC_BATCH * PAST_LEN, MLA_KV_RANK)
            c_kr = _spread_rope(cache_mla_krope[:, e].reshape(DEC_BATCH * PAST_LEN, MLA_ROPE))
            kmc, vc = _kv_expand(c_ckv, c_kr, w["w_kvb"], w["nope_g"], PAST_LEN)
            kgc = cache_gqa_k[:, e].reshape(DEC_BATCH * PAST_LEN, GQA_KV_HEADS * HEAD_DIM)
            vgc = cache_gqa_v[:, e].reshape(DEC_BATCH * PAST_LEN, GQA_KV_HEADS * HEAD_DIM)
            att = _attention(qm, qg, km, v, kg, vg, DEC_SEQ, TQ_ATT, cache=(kmc, vc, kgc, vgc))
            xs = _out_proj(xs, mod_tab, True, [att], w["w_out"])
        else:
            o = l // 2
            w_out = od_w_out[o]
            logit = ret_decay_logit[o]
            ret_g = ret_norm[o].reshape(1, -1)
            skip = hy_skip[o].reshape(1, -1)
            conv_b = hy_conv_b[o].reshape(1, -1)
            filt_args = (hy_filt_w1[o], hy_filt_b1[o], hy_filt_w2[o], hy_filt_b2[o], hy_filt_freq[o],
                         hy_filt_w3[o], hy_decay[o])
            for stream in ("ctx", "smp"):
                x, seq, latent = (xc, SEQ, False) if stream == "ctx" else (xs, DEC_SEQ, True)
                fwd, inv = _dft_matrices(seq)
                fwd_b, inv_b = jnp.asarray(fwd).astype(BF16), jnp.asarray(inv).astype(BF16)
                spectra = _filter_spectra(seq, *filt_args, fwd_b, TC_HY)
                u = _odd_proj(x, mod_tab, latent, od_w_in, o)
                if stream == "ctx":
                    o_ret, st = _retention(u, logit, ret_g, seq, RET_HEADS)
                    outs["ret"] = st
                    o_hy = _hyena(u, seq, HY_CH, hy_conv_w[o], conv_b, spectra, skip, fwd_b, inv_b)
                else:
                    (o_ret,) = _retention(u, logit, ret_g, seq, 1, s0=state_ret[:, o])
                    o_hy = _hyena(u, seq, TC_HY, hy_conv_w[o], conv_b, spectra, skip, fwd_b, inv_b)
                x = _out_proj(x, mod_tab, latent, [o_ret, o_hy], w_out)
                if stream == "ctx":
                    xc = x
                else:
                    xs = x
        xc = _ffn(xc, mod_tab, False, 2, ffn_w_gate, ffn_w_up, ffn_w_down, l, 1)
        xs = _ffn(xs, mod_tab, True, 2, ffn_w_gate, ffn_w_up, ffn_w_down, l, 1)

    return (
        xc.reshape(BATCH, SEQ, D_MODEL),
        xs.reshape(DEC_BATCH, DEC_SEQ, D_MODEL),
        outs["ckv"].reshape(BATCH, 1, SEQ, MLA_KV_RANK),
        outs["krope"].reshape(BATCH, 1, SEQ, MLA_ROPE),
        outs["k"].reshape(BATCH, 1, SEQ, GQA_KV_HEADS, HEAD_DIM),
        outs["v"].reshape(BATCH, 1, SEQ, GQA_KV_HEADS, HEAD_DIM),
        outs["ret"].reshape(BATCH, 1, 2, RET_HEADS, RET_DK, RET_DV),
    )
```

```python
import functools
import math

import numpy as np
import jax
import jax.numpy as jnp
from jax import lax
from jax.experimental import pallas as pl
from jax.experimental.pallas import tpu as pltpu

F32 = jnp.float32
BF16 = jnp.bfloat16

D_MODEL = 2048
BATCH = 32
SEQ = 256
DEPTH = 2
DEC_BATCH = 2
DEC_SEQ = 1024
PAST_LEN = 256
GRID_W = 64
N_MOD = 9
D_FF = 5632
ROPE_THETA = 10000.0
EPS = 1e-6

MLA_HEADS = 8
MLA_Q_RANK = 512
MLA_KV_RANK = 512
MLA_NOPE = 128
MLA_ROPE = 64
MLA_V = 128
GQA_HEADS = 8
GQA_KV_HEADS = 2
GQA_GROUP = GQA_HEADS // GQA_KV_HEADS
HEAD_DIM = 128
MLA_SCALE = (MLA_NOPE + MLA_ROPE) ** -0.5
GQA_SCALE = HEAD_DIM ** -0.5
MLA_QK_PAD = 256

RET_HEADS = 8
RET_DK = 128
RET_DV = 128
HY_CH = 1024
HY_ORDER = 2
HY_BANDS = 16
HY_FHID = 64
ODD_IN = 4 * RET_HEADS * RET_DK + (HY_ORDER + 1) * HY_CH

V7X_VMEM_BYTES = 64 * 1024 * 1024
LANES = 128
VMEM_LIMIT = V7X_VMEM_BYTES - 3 * 1024 * 1024

TM_FFN = 1024
TF_FFN = 512
TF_SUB = 256
TM_PROJ = 512
TM_ODD = 1024
TN_MOD = 1024
TN_ODD = 1024
TQ_ATT = 256
TC_HY = 512
HY_SUB_LAT = 256
RET_HEADS_LAT = 2


def _cparams(*sem):
    return pltpu.CompilerParams(dimension_semantics=sem, vmem_limit_bytes=VMEM_LIMIT)


def _const_spec(shape):
    nd = len(shape)
    return pl.BlockSpec(shape, lambda *_: (0,) * nd)


def _silu(x):
    return x * jax.nn.sigmoid(x)


def _rms_rows(x, width=None):
    ss = jnp.sum(x * x, axis=-1, keepdims=True)
    n = x.shape[-1] if width is None else width
    return x * lax.rsqrt(ss * (1.0 / n) + EPS)


def _mod_kernel(c_ref, w_ref, b_ref, o_ref):
    s = _silu(c_ref[...]).astype(BF16)
    o_ref[0] = jnp.dot(s, w_ref[0].astype(BF16), preferred_element_type=F32) + b_ref[0]


def _mod_vectors(cvec, mod_w, mod_b):
    depth, d, n = mod_w.shape
    rows = cvec.shape[0]
    return pl.pallas_call(
        _mod_kernel,
        out_shape=jax.ShapeDtypeStruct((depth, rows, n), F32),
        grid=(depth, n // TN_MOD),
        in_specs=[
            pl.BlockSpec((rows, d), lambda l, j: (0, 0)),
            pl.BlockSpec((1, d, TN_MOD), lambda l, j: (l, 0, j)),
            pl.BlockSpec((1, 1, TN_MOD), lambda l, j: (l, 0, j)),
        ],
        out_specs=pl.BlockSpec((1, rows, TN_MOD), lambda l, j: (l, 0, j)),
        compiler_params=_cparams("parallel", "parallel"),
        name="mod_vectors",
    )(cvec, mod_w, mod_b.reshape(depth, 1, n))


def _mod_specs(sub, latent, tile, extra_axes=0):
    group_of_tile = (lambda i: 1 + (i * tile) // DEC_SEQ) if latent else (lambda i: 0)

    def spec(k):
        if extra_axes:
            return pl.BlockSpec((1, 1, D_MODEL), lambda i, j: (group_of_tile(i) * N_MOD + 3 * sub + k, 0, 0))
        return pl.BlockSpec((1, 1, D_MODEL), lambda i: (group_of_tile(i) * N_MOD + 3 * sub + k, 0, 0))
    return [spec(0), spec(1), spec(2)]


def _modulated(x, sh_ref, sc_ref):
    return (_rms_rows(x) * (1.0 + sc_ref[0]) + sh_ref[0]).astype(BF16)


def _ffn_kernel(x_hbm, sh_ref, sc_ref, gt_ref, wg_ref, wu_ref, wd_ref, o_ref, x_buf, h_ref, x_sem):
    i, f = pl.program_id(0), pl.program_id(1)
    tm = x_buf.shape[0]
    half_gate = 0.5 * gt_ref[0]

    def x_copy(tile):
        return pltpu.make_async_copy(x_hbm.at[pl.ds(tile * tm, tm), :], x_buf, x_sem)

    def partial_out(h):
        out = None
        for c in range(TF_FFN // TF_SUB):
            cs = slice(c * TF_SUB, (c + 1) * TF_SUB)
            g = jnp.dot(h, wg_ref[:, cs].astype(BF16), preferred_element_type=F32)
            u = jnp.dot(h, wu_ref[:, cs].astype(BF16), preferred_element_type=F32)
            a = (_silu(g) * u).astype(BF16)
            t = jnp.dot(a, wd_ref[cs, :].astype(BF16), preferred_element_type=F32)
            out = t if out is None else out + t
        return half_gate * out

    @pl.when(f == 0)
    def _():
        @pl.when(i == 0)
        def _():
            x_copy(0).start()

        x_copy(i).wait()
        x = x_buf[...]
        h = _modulated(x, sh_ref, sc_ref)
        h_ref[...] = h
        o_ref[...] = x + partial_out(h)

    @pl.when(f > 0)
    def _():
        @pl.when(jnp.logical_and(f == pl.num_programs(1) // 2, i + 1 < pl.num_programs(0)))
        def _():
            x_copy(i + 1).start()

        o_ref[...] += partial_out(h_ref[...])


def _ffn(x, mod_tab, latent, sub, w_gate, w_up, w_down, layer, j):
    m, d = x.shape
    dff = w_gate.shape[-1]
    assert dff // TF_FFN >= 2
    return pl.pallas_call(
        _ffn_kernel,
        out_shape=jax.ShapeDtypeStruct((m, d), F32),
        grid=(m // TM_FFN, dff // TF_FFN),
        in_specs=[pl.BlockSpec(memory_space=pl.ANY)]
        + _mod_specs(sub, latent, TM_FFN, extra_axes=1)
        + [
            pl.BlockSpec((None, None, d, TF_FFN), lambda i, f: (layer, j, 0, f)),
            pl.BlockSpec((None, None, d, TF_FFN), lambda i, f: (layer, j, 0, f)),
            pl.BlockSpec((None, None, TF_FFN, d), lambda i, f: (layer, j, f, 0)),
        ],
        out_specs=pl.BlockSpec((TM_FFN, d), lambda i, f: (i, 0)),
        scratch_shapes=[pltpu.VMEM((TM_FFN, d), F32), pltpu.VMEM((TM_FFN, d), BF16), pltpu.SemaphoreType.DMA(())],
        compiler_params=_cparams("arbitrary", "arbitrary"),
        name="ffn",
    )(x, mod_tab, mod_tab, mod_tab, w_gate, w_up, w_down)


def _out_proj_kernel(*refs, n_parts):
    x_ref, gt_ref, w_ref = refs[0], refs[1], refs[2]
    part_refs = refs[3:3 + n_parts]
    o_ref, wb_ref = refs[3 + n_parts], refs[4 + n_parts]

    @pl.when(pl.program_id(0) == 0)
    def _():
        wb_ref[...] = w_ref[...].astype(BF16)

    acc, off = None, 0
    for p_ref in part_refs:
        width = p_ref.shape[1]
        t = jnp.dot(p_ref[...], wb_ref[off:off + width, :], preferred_element_type=F32)
        acc = t if acc is None else acc + t
        off += width
    o_ref[...] = x_ref[...] + gt_ref[0] * acc


def _out_proj(x, mod_tab, latent, parts, w_out):
    m, d = x.shape
    return pl.pallas_call(
        functools.partial(_out_proj_kernel, n_parts=len(parts)),
        out_shape=jax.ShapeDtypeStruct((m, d), F32),
        grid=(m // TM_PROJ,),
        in_specs=[pl.BlockSpec((TM_PROJ, d), lambda i: (i, 0)), _mod_specs(1, latent, TM_PROJ)[2],
                  pl.BlockSpec(w_out.shape, lambda i: (0, 0), pipeline_mode=pl.Buffered(1))]
        + [pl.BlockSpec((TM_PROJ, p.shape[1]), lambda i: (i, 0)) for p in parts],
        out_specs=pl.BlockSpec((TM_PROJ, d), lambda i: (i, 0)),
        scratch_shapes=[pltpu.VMEM(w_out.shape, BF16)],
        compiler_params=_cparams("arbitrary"),
        name="mixer_out_proj",
    )(x, mod_tab, w_out, *parts)


EV_QA, EV_KVA, EV_QG, EV_KG, EV_VG, EV_KR, EV_END = 0, 512, 1024, 2048, 2304, 2560, 2688


def _rotate_half(x, cos, sin):
    return x * cos + pltpu.roll(x, LANES // 2, axis=1) * sin


def _even_proj_kernel(*refs, rotary):
    (x_ref, sh_ref, sc_ref, w_in_ref, w_qb_ref, qn_ref, kvn_ref, ng_ref, rg_ref, gg_ref) = refs[:10]
    if rotary:
        cm_ref, sm_ref, cg_ref, sg_ref = refs[10:14]
        rot_m = lambda t: _rotate_half(t, cm_ref[...], sm_ref[...])
        rot_g = lambda t: _rotate_half(t, cg_ref[...], sg_ref[...])
    else:
        rot_m = rot_g = lambda t: t
    qm_ref, ckv_ref, kr_ref, qg_ref, kg_ref, vg_ref, wb_ref = refs[-7:]

    @pl.when(pl.program_id(0) == 0)
    def _():
        chunk = 256
        lane = lax.broadcasted_iota(jnp.int32, (chunk, LANES), 1)
        half = MLA_ROPE // 2

        def regroup(c, carry):
            rs = pl.ds(pl.multiple_of(c * chunk, chunk), chunk)
            wb_ref[rs, EV_QA:EV_QG] = w_in_ref[rs, EV_QA:EV_QG].astype(BF16)
            tail = w_in_ref[rs, EV_QG:]
            wb_ref[rs, EV_QG:EV_KR] = tail[:, MLA_ROPE:].astype(BF16)
            kr = tail[:, :LANES]
            spread = (jnp.where(lane < half, kr, 0.0)
                      + jnp.where(jnp.logical_and(lane >= LANES // 2, lane < LANES // 2 + half),
                                  pltpu.roll(kr, LANES // 2 - half, axis=1), 0.0))
            wb_ref[rs, EV_KR:EV_END] = spread.astype(BF16)
            return carry

        lax.fori_loop(0, w_in_ref.shape[0] // chunk, regroup, 0)

    h = _modulated(x_ref[...], sh_ref, sc_ref)
    u = jnp.dot(h, wb_ref[...], preferred_element_type=F32)

    nope_gain = ng_ref[0:1, :] * MLA_SCALE
    rope_gain = rg_ref[0:1, :] * MLA_SCALE
    gqa_gain = gg_ref[0:1, :] * GQA_SCALE

    qa = (_rms_rows(u[:, EV_QA:EV_KVA]) * qn_ref[...]).astype(BF16)
    q = jnp.dot(qa, w_qb_ref[...], preferred_element_type=F32)
    for hd in range(MLA_HEADS):
        c0 = hd * MLA_QK_PAD
        nope = _rms_rows(q[:, c0:c0 + MLA_NOPE]) * nope_gain
        rope = _rms_rows(q[:, c0 + MLA_NOPE:c0 + MLA_QK_PAD], MLA_ROPE) * rope_gain
        qm_ref[:, c0:c0 + MLA_NOPE] = nope.astype(BF16)
        qm_ref[:, c0 + MLA_NOPE:c0 + MLA_QK_PAD] = rot_m(rope).astype(BF16)

    ckv_ref[...] = _rms_rows(u[:, EV_KVA:EV_QG]) * kvn_ref[...]
    kr_ref[...] = rot_m(_rms_rows(u[:, EV_KR:EV_END], MLA_ROPE) * rg_ref[1:2, :])

    for hd in range(GQA_HEADS):
        c0 = EV_QG + hd * HEAD_DIM
        qh = _rms_rows(u[:, c0:c0 + HEAD_DIM]) * gqa_gain
        qg_ref[:, hd * HEAD_DIM:(hd + 1) * HEAD_DIM] = rot_g(qh).astype(BF16)
    for hd in range(GQA_KV_HEADS):
        c0 = EV_KG + hd * HEAD_DIM
        kh = _rms_rows(u[:, c0:c0 + HEAD_DIM]) * gg_ref[1:2, :]
        kg_ref[:, hd * HEAD_DIM:(hd + 1) * HEAD_DIM] = rot_g(kh)
    vg_ref[...] = u[:, EV_VG:EV_KR]


def _even_proj(x, mod_tab, w, tables=None):
    m, d = x.shape
    tm = TM_PROJ
    row = lambda w_: pl.BlockSpec((tm, w_), lambda i: (i, 0))
    tab_specs, tab_args = [], []
    if tables is not None:
        tiles_per_seq = tables[0].shape[0] // tm
        tab_specs = [pl.BlockSpec((tm, LANES), lambda i: (i % tiles_per_seq, 0))] * 4
        tab_args = list(tables)
    return pl.pallas_call(
        functools.partial(_even_proj_kernel, rotary=tables is not None),
        out_shape=[
            jax.ShapeDtypeStruct((m, MLA_HEADS * MLA_QK_PAD), BF16),
            jax.ShapeDtypeStruct((m, MLA_KV_RANK), F32),
            jax.ShapeDtypeStruct((m, LANES), F32),
            jax.ShapeDtypeStruct((m, GQA_HEADS * HEAD_DIM), BF16),
            jax.ShapeDtypeStruct((m, GQA_KV_HEADS * HEAD_DIM), F32),
            jax.ShapeDtypeStruct((m, GQA_KV_HEADS * HEAD_DIM), F32),
        ],
        grid=(m // tm,),
        in_specs=[row(d)] + _mod_specs(1, tables is not None, tm)[:2]
        + [pl.BlockSpec((None,) + w["w_in"].shape[1:], lambda i: (w["layer"], 0, 0)),
           _const_spec(w["w_qb"].shape), _const_spec(w["q_norm"].shape),
           _const_spec(w["kv_norm"].shape), _const_spec(w["nope_g"].shape), _const_spec(w["rope_g"].shape),
           _const_spec(w["gqa_g"].shape)] + tab_specs,
        out_specs=[row(MLA_HEADS * MLA_QK_PAD), row(MLA_KV_RANK), row(LANES), row(GQA_HEADS * HEAD_DIM),
                   row(GQA_KV_HEADS * HEAD_DIM), row(GQA_KV_HEADS * HEAD_DIM)],
        scratch_shapes=[pltpu.VMEM((d, EV_END), BF16)],
        compiler_params=_cparams("arbitrary"),
        name="even_proj",
    )(x, mod_tab, mod_tab, w["w_in"], w["w_qb"], w["q_norm"], w["kv_norm"], w["nope_g"], w["rope_g"], w["gqa_g"],
      *tab_args)


def _kv_expand_kernel(ckv_ref, kr_ref, w_ref, ng_ref, km_ref, v_ref):
    kv = jnp.dot(ckv_ref[...].astype(BF16), w_ref[...], preferred_element_type=F32)
    kr = kr_ref[...].astype(BF16)
    for hd in range(MLA_HEADS):
        c0 = hd * (MLA_NOPE + MLA_V)
        km_ref[:, hd * MLA_QK_PAD:hd * MLA_QK_PAD + MLA_NOPE] = (
            _rms_rows(kv[:, c0:c0 + MLA_NOPE]) * ng_ref[1:2, :]).astype(BF16)
        km_ref[:, hd * MLA_QK_PAD + MLA_NOPE:(hd + 1) * MLA_QK_PAD] = kr
        v_ref[:, hd * MLA_V:(hd + 1) * MLA_V] = kv[:, c0 + MLA_NOPE:c0 + MLA_NOPE + MLA_V].astype(BF16)


def _kv_expand(ckv, kr, w_kvb_bf16, nope_g, tr):
    r = ckv.shape[0]
    row = lambda w_: pl.BlockSpec((tr, w_), lambda i: (i, 0))
    return pl.pallas_call(
        _kv_expand_kernel,
        out_shape=[jax.ShapeDtypeStruct((r, MLA_HEADS * MLA_QK_PAD), BF16),
                   jax.ShapeDtypeStruct((r, MLA_HEADS * MLA_V), BF16)],
        grid=(r // tr,),
        in_specs=[row(MLA_KV_RANK), row(LANES), _const_spec(w_kvb_bf16.shape), _const_spec(nope_g.shape)],
        out_specs=[row(MLA_HEADS * MLA_QK_PAD), row(MLA_HEADS * MLA_V)],
        compiler_params=_cparams("parallel"),
        name="kv_expand",
    )(ckv, kr, w_kvb_bf16, nope_g)


def _softmax_pv(score_list, value_list):
    mx = None
    for s in score_list:
        m_ = jnp.max(s, axis=-1, keepdims=True)
        mx = m_ if mx is None else jnp.maximum(mx, m_)
    den, acc = None, None
    for s, v in zip(score_list, value_list):
        p = jnp.exp(s - mx)
        l_ = jnp.sum(p, axis=-1, keepdims=True)
        o_ = jnp.dot(p.astype(BF16), v, preferred_element_type=F32)
        den = l_ if den is None else den + l_
        acc = o_ if acc is None else acc + o_
    return acc / den


def _qk(q, k):
    return lax.dot_general(q, k, (((1,), (1,)), ((), ())), preferred_element_type=F32)


def _attn_kernel(*refs, has_cache):
    if has_cache:
        qm_ref, qg_ref, km_ref, v_ref, kg_ref, vg_ref, kmc_ref, vc_ref, kgc_ref, vgc_ref, o_ref = refs
    else:
        qm_ref, qg_ref, km_ref, v_ref, kg_ref, vg_ref, o_ref = refs
    for hd in range(MLA_HEADS):
        qs = slice(hd * MLA_QK_PAD, (hd + 1) * MLA_QK_PAD)
        vs = slice(hd * MLA_V, (hd + 1) * MLA_V)
        q = qm_ref[:, qs]
        scores = [_qk(q, km_ref[:, qs])]
        values = [v_ref[:, vs]]
        if has_cache:
            scores.append(_qk(q, kmc_ref[:, qs]))
            values.append(vc_ref[:, vs])
        o_ref[:, vs] = _softmax_pv(scores, values).astype(o_ref.dtype)
    base = MLA_HEADS * MLA_V
    for kvh in range(GQA_KV_HEADS):
        ks = slice(kvh * HEAD_DIM, (kvh + 1) * HEAD_DIM)
        k = kg_ref[:, ks].astype(BF16)
        v = vg_ref[:, ks].astype(BF16)
        if has_cache:
            kc = kgc_ref[:, ks].astype(BF16)
            vc = vgc_ref[:, ks].astype(BF16)
        for g in range(GQA_GROUP):
            hd = kvh * GQA_GROUP + g
            hs = slice(hd * HEAD_DIM, (hd + 1) * HEAD_DIM)
            q = qg_ref[:, hs]
            scores, values = [_qk(q, k)], [v]
            if has_cache:
                scores.append(_qk(q, kc))
                values.append(vc)
            o_ref[:, base + hd * HEAD_DIM:base + (hd + 1) * HEAD_DIM] = _softmax_pv(scores, values).astype(o_ref.dtype)


def _attention(qm, qg, km, v, kg, vg, seq, tq, cache=None):
    m = qm.shape[0]
    nq = seq // tq
    qrow = lambda w_: pl.BlockSpec((tq, w_), lambda i: (i, 0))
    krow = lambda w_: pl.BlockSpec((seq, w_), lambda i: (i // nq, 0))
    in_specs = [qrow(qm.shape[1]), qrow(qg.shape[1]), krow(km.shape[1]), krow(v.shape[1]),
                krow(kg.shape[1]), krow(vg.shape[1])]
    args = [qm, qg, km, v, kg, vg]
    if cache is not None:
        past = cache[0].shape[0] // (m // seq)
        crow = lambda w_: pl.BlockSpec((past, w_), lambda i: (i // nq, 0))
        in_specs += [crow(c.shape[1]) for c in cache]
        args += list(cache)
    width = MLA_HEADS * MLA_V + GQA_HEADS * HEAD_DIM
    return pl.pallas_call(
        functools.partial(_attn_kernel, has_cache=cache is not None),
        out_shape=jax.ShapeDtypeStruct((m, width), BF16),
        grid=(m // tq,),
        in_specs=in_specs,
        out_specs=qrow(width),
        compiler_params=_cparams("parallel"),
        name="attention",
    )(*args)


def _odd_proj_kernel(x_ref, sh_ref, sc_ref, w_ref, o_ref, h_ref):
    def project(h):
        return jnp.dot(h, w_ref[...].astype(BF16), preferred_element_type=F32).astype(o_ref.dtype)

    @pl.when(pl.program_id(1) == 0)
    def _():
        h = _modulated(x_ref[...], sh_ref, sc_ref)
        h_ref[...] = h
        o_ref[...] = project(h)

    @pl.when(pl.program_id(1) > 0)
    def _():
        o_ref[...] = project(h_ref[...])


def _odd_proj(x, mod_tab, latent, w_in, layer):
    m, d = x.shape
    n = w_in.shape[-1]
    return pl.pallas_call(
        _odd_proj_kernel,
        out_shape=jax.ShapeDtypeStruct((m, n), BF16),
        grid=(m // TM_ODD, n // TN_ODD),
        in_specs=[pl.BlockSpec((TM_ODD, d), lambda i, j: (i, 0))]
        + _mod_specs(1, latent, TM_ODD, extra_axes=1)[:2]
        + [pl.BlockSpec((None, d, TN_ODD), lambda i, j: (layer, 0, j))],
        out_specs=pl.BlockSpec((TM_ODD, TN_ODD), lambda i, j: (i, j)),
        scratch_shapes=[pltpu.VMEM((TM_ODD, d), BF16)],
        compiler_params=_cparams("parallel", "arbitrary"),
        name="odd_proj",
    )(x, mod_tab, mod_tab, w_in)


def _log_sigmoid(x):
    return jnp.minimum(x, 0.0) - jnp.log1p(jnp.exp(-jnp.abs(x)))


def _retention_kernel(*refs, heads, has_init):
    if has_init:
        logit_ref, q_ref, k_ref, v_ref, g_ref, rg_ref, s0_ref, o_ref, dec_ref, vec_ref = refs
    else:
        logit_ref, q_ref, k_ref, v_ref, g_ref, rg_ref, o_ref, st_ref, dec_ref, vec_ref = refs
    length = q_ref.shape[0]
    h0 = pl.program_id(0) * heads
    scale = RET_DK ** -0.5
    log_g = _log_sigmoid(logit_ref[...])
    lane_h = lax.broadcasted_iota(jnp.int32, (2, RET_HEADS), 1)

    def head_log_decays(hh):
        lg = jnp.sum(jnp.where(lane_h == h0 + hh, log_g, 0.0), axis=1, keepdims=True)
        return lg[0:1, :], lg[1:2, :]

    @pl.when(pl.program_id(1) == 0)
    def _():
        n_i = lax.broadcasted_iota(jnp.int32, (length, length), 0)
        m_i = lax.broadcasted_iota(jnp.int32, (length, length), 1)
        diff = (n_i - m_i).astype(F32)
        pos = lax.broadcasted_iota(jnp.int32, (length, RET_DK), 0).astype(F32)
        for hh in range(heads):
            lg_f, lg_b = head_log_decays(hh)
            dec_ref[hh] = scale * (jnp.where(diff >= 0, jnp.exp(jnp.maximum(diff, 0.0) * lg_f), 0.0)
                                   + jnp.where(diff <= 0, jnp.exp(jnp.maximum(-diff, 0.0) * lg_b), 0.0))
            if has_init:
                vec_ref[4 * hh + 2] = jnp.exp((pos + 1.0) * lg_f)
                vec_ref[4 * hh + 3] = jnp.exp((length - pos) * lg_b)
            else:
                vec_ref[4 * hh + 0] = jnp.exp((length - 1.0 - pos) * lg_f) * scale
                vec_ref[4 * hh + 1] = jnp.exp(pos * lg_b) * scale

    tn = (((0,), (0,)), ((), ()))
    for hh in range(heads):
        cs = slice(hh * RET_DK, (hh + 1) * RET_DK)
        q, k, v = q_ref[:, cs], k_ref[:, cs], v_ref[:, cs]
        att = _qk(q, k) * dec_ref[hh]
        o = jnp.dot(att.astype(BF16), v, preferred_element_type=F32)
        if has_init:
            s0_f, s0_b = s0_ref[0, 0, hh].astype(BF16), s0_ref[0, 1, hh].astype(BF16)
            o = o + jnp.dot(q, s0_f, preferred_element_type=F32) * vec_ref[4 * hh + 2]
            o = o + jnp.dot(q, s0_b, preferred_element_type=F32) * vec_ref[4 * hh + 3]
        else:
            kf = k.astype(F32)
            k_f, k_b = (kf * vec_ref[4 * hh + 0]).astype(BF16), (kf * vec_ref[4 * hh + 1]).astype(BF16)
            st_ref[0, 0, hh] = lax.dot_general(k_f, v, tn, preferred_element_type=F32)
            st_ref[0, 1, hh] = lax.dot_general(k_b, v, tn, preferred_element_type=F32)
        gate = g_ref[:, cs].astype(F32)
        o_ref[:, cs] = (_rms_rows(o) * rg_ref[:, cs] * _silu(gate)).astype(o_ref.dtype)


def _retention(u, decay_logit, ret_g, seq, heads, s0=None):
    m = u.shape[0]
    b = m // seq
    hw = heads * RET_DK
    nh = RET_HEADS // heads
    blocks_per_part = RET_HEADS * RET_DK // hw
    col = lambda part: pl.BlockSpec((seq, hw), lambda j, i, part=part: (i, part * blocks_per_part + j))
    st_spec = pl.BlockSpec((1, 2, heads, RET_DK, RET_DV), lambda j, i: (i, 0, j, 0, 0))
    in_specs = [_const_spec(decay_logit.shape), col(0), col(1), col(2), col(3),
                pl.BlockSpec((1, hw), lambda j, i: (0, j))]
    args = [decay_logit, u, u, u, u, ret_g]
    if s0 is not None:
        in_specs.append(st_spec)
        args.append(s0)
    out_shape = [jax.ShapeDtypeStruct((m, RET_HEADS * RET_DV), BF16)]
    out_specs = [pl.BlockSpec((seq, hw), lambda j, i: (i, j))]
    if s0 is None:
        out_shape.append(jax.ShapeDtypeStruct((b, 2, RET_HEADS, RET_DK, RET_DV), F32))
        out_specs.append(st_spec)
    return pl.pallas_call(
        functools.partial(_retention_kernel, heads=heads, has_init=s0 is not None),
        out_shape=out_shape,
        grid=(nh, b),
        in_specs=in_specs,
        out_specs=out_specs,
        scratch_shapes=[pltpu.VMEM((heads, seq, seq), F32), pltpu.VMEM((4 * heads, seq, RET_DK), F32)],
        compiler_params=_cparams("parallel", "arbitrary"),
        name="retention",
    )(*args)


@functools.lru_cache(maxsize=None)
def _dft_matrices(length):
    n = 2 * length
    f = np.arange(length, dtype=np.float64)[:, None]
    t = np.arange(length, dtype=np.float64)[None, :]
    ang = 2.0 * np.pi * f * t / n
    fwd_a = np.cos(ang)
    fwd_b = np.sin(ang)
    fwd_b[0, :] = np.cos(np.pi * t[0])
    fwd = np.concatenate([fwd_a, fwd_b], axis=0)
    tt = (np.arange(length, dtype=np.float64) + length // 2)[:, None]
    ff = np.arange(length, dtype=np.float64)[None, :]
    ang_i = 2.0 * np.pi * ff * tt / n
    inv_a = 2.0 * np.cos(ang_i) / n
    inv_b = 2.0 * np.sin(ang_i) / n
    inv_a[:, 0] = 1.0 / n
    inv_b[:, 0] = np.cos(np.pi * tt[:, 0]) / n
    inv = np.concatenate([inv_a, inv_b], axis=1)
    return fwd.astype(np.float32), inv.astype(np.float32)


@functools.lru_cache(maxsize=None)
def _filter_features(length):
    t = np.arange(length, dtype=np.float64)
    tn = t / length
    bands = np.arange(1, HY_BANDS + 1, dtype=np.float64)
    ang = 2.0 * np.pi * tn[:, None] * bands[None, :]
    feat = np.concatenate([tn[:, None], np.sin(ang), np.cos(ang)], axis=-1)
    feat = np.pad(feat, ((0, 0), (0, LANES - feat.shape[1])))
    r = (np.abs(t - length // 2) / (length / 2))[:, None]
    return feat.astype(np.float32), r.astype(np.float32)


def _filter_spec_kernel(feat_ref, r_ref, w1_ref, b1_ref, w2_ref, b2_ref, fr_ref, w3_ref, dec_ref, fwd_ref, o_ref,
                        z_ref):
    hp = lax.Precision.HIGHEST

    @pl.when(pl.program_id(0) == 0)
    def _():
        z = jnp.sin(fr_ref[0:1, :] * (jnp.dot(feat_ref[...], w1_ref[...], precision=hp, preferred_element_type=F32)
                                      + b1_ref[...]))
        z_ref[...] = jnp.sin(fr_ref[1:2, :] * (jnp.dot(z, w2_ref[...], precision=hp, preferred_element_type=F32)
                                               + b2_ref[...]))

    filt = jnp.dot(z_ref[...], w3_ref[...], precision=hp, preferred_element_type=F32)
    filt = filt * jnp.exp(-r_ref[...] * jnp.abs(dec_ref[...]))
    spec = jnp.dot(fwd_ref[...], filt.astype(BF16), preferred_element_type=F32)
    length = spec.shape[0] // 2
    sa, sb = spec[:length], spec[length:]
    first = lax.broadcasted_iota(jnp.int32, sa.shape, 0) == 0
    o_ref[0:length, :] = sa
    o_ref[length:2 * length, :] = jnp.where(first, 0.0, sb)
    o_ref[2 * length:, :] = jnp.where(first, sb, sa)


def _filter_spectra(length, w1, b1, w2, b2, freq, w3, decay, fwd_bf16, tc):
    feat, r = _filter_features(length)
    nch = w3.shape[1]
    w1p = jnp.pad(w1, ((0, LANES - w1.shape[0]), (0, 0)))
    return pl.pallas_call(
        _filter_spec_kernel,
        out_shape=jax.ShapeDtypeStruct((3 * length, nch), F32),
        grid=(nch // tc,),
        in_specs=[_const_spec(feat.shape), _const_spec(r.shape), _const_spec(w1p.shape), _const_spec((1, HY_FHID)),
                  _const_spec(w2.shape), _const_spec((1, HY_FHID)), _const_spec(freq.shape),
                  pl.BlockSpec((HY_FHID, tc), lambda j: (0, j)), pl.BlockSpec((1, tc), lambda j: (0, j)),
                  _const_spec(fwd_bf16.shape)],
        out_specs=pl.BlockSpec((3 * length, tc), lambda j: (0, j)),
        scratch_shapes=[pltpu.VMEM((length, HY_FHID), F32)],
        compiler_params=_cparams("arbitrary"),
        name="hyena_filter_spectra",
    )(jnp.asarray(feat), jnp.asarray(r), w1p, b1.reshape(1, -1), w2, b2.reshape(1, -1), freq, w3,
      decay.reshape(1, -1), fwd_bf16)


def _hyena_kernel(v_ref, x1_ref, x2_ref, cwv_ref, cw1_ref, cw2_ref, cbv_ref, cb1_ref, cb2_ref,
                  h1_ref, h2_ref, sk1_ref, sk2_ref, fwd_ref, inv_ref, o_ref, *, sub):
    length, tc = v_ref.shape
    row = lax.broadcasted_iota(jnp.int32, (length, sub), 0)
    first, last = row == 0, row == length - 1

    for c in range(tc // sub):
        cs = slice(c * sub, (c + 1) * sub)

        def short_conv(x_ref, w_ref, b_ref):
            x = x_ref[:, cs].astype(F32)
            prev = jnp.where(first, 0.0, pltpu.roll(x, 1, axis=0))
            nxt = jnp.where(last, 0.0, pltpu.roll(x, length - 1, axis=0))
            return prev * w_ref[0:1, cs] + x * w_ref[1:2, cs] + nxt * w_ref[2:3, cs] + b_ref[:, cs]

        def long_conv(z, h_ref):
            zs = jnp.dot(fwd_ref[...], z.astype(BF16), preferred_element_type=F32)
            za, zb = zs[:length], zs[length:]
            ha, hb, hc = h_ref[0:length, cs], h_ref[length:2 * length, cs], h_ref[2 * length:, cs]
            y = jnp.concatenate([za * ha - zb * hb, za * hb + zb * hc], axis=0).astype(BF16)
            return jnp.dot(inv_ref[...], y, preferred_element_type=F32)

        v = short_conv(v_ref, cwv_ref, cbv_ref)
        x1 = short_conv(x1_ref, cw1_ref, cb1_ref)
        x2 = short_conv(x2_ref, cw2_ref, cb2_ref)
        z = x1 * (long_conv(v, h1_ref) + sk1_ref[:, cs] * v)
        o_ref[:, cs] = (x2 * (long_conv(z, h2_ref) + sk2_ref[:, cs] * z)).astype(o_ref.dtype)


def _hyena(u, seq, tc, sub, conv_w, conv_b, spectra, skip, fwd_bf16, inv_bf16):
    m = u.shape[0]
    b = m // seq
    nc = HY_CH // tc
    hy0 = 4 * RET_HEADS * RET_DK // tc
    ucol = lambda part: pl.BlockSpec((seq, tc), lambda i, j, part=part: (i, hy0 + part * nc + j))
    wcol = lambda rows, part: pl.BlockSpec((rows, tc), lambda i, j, part=part: (0, part * nc + j))
    return pl.pallas_call(
        functools.partial(_hyena_kernel, sub=sub),
        out_shape=jax.ShapeDtypeStruct((m, HY_CH), BF16),
        grid=(b, nc),
        in_specs=[ucol(0), ucol(1), ucol(2), wcol(3, 0), wcol(3, 1), wcol(3, 2), wcol(1, 0), wcol(1, 1), wcol(1, 2),
                  wcol(3 * seq, 0), wcol(3 * seq, 1), wcol(1, 0), wcol(1, 1),
                  _const_spec(fwd_bf16.shape), _const_spec(inv_bf16.shape)],
        out_specs=pl.BlockSpec((seq, tc), lambda i, j: (i, j)),
        compiler_params=_cparams("parallel", "parallel"),
        name="hyena",
    )(u, u, u, conv_w, conv_w, conv_w, conv_b, conv_b, conv_b, spectra, spectra, skip, skip, fwd_bf16, inv_bf16)


@functools.lru_cache(maxsize=None)
def _rope_tables(length):
    rows = np.repeat(np.arange(length // GRID_W, dtype=np.float64), GRID_W)
    cols = np.tile(np.arange(GRID_W, dtype=np.float64), length // GRID_W)

    def pack(dim):
        half = dim // 2
        freq = ROPE_THETA ** (-np.arange(0, half, 2, dtype=np.float64) / half)
        ang = np.concatenate([rows[:, None] * freq[None], cols[:, None] * freq[None]], axis=-1)
        pad = ((0, 0), (0, LANES // 2 - half))
        cos = np.pad(np.cos(ang), pad, constant_values=1.0)
        sin = np.pad(np.sin(ang), pad)
        return (np.concatenate([cos, cos], axis=-1).astype(np.float32),
                np.concatenate([-sin, sin], axis=-1).astype(np.float32))

    cos_m, sin_m = pack(MLA_ROPE)
    cos_g, sin_g = pack(HEAD_DIM)
    return cos_m, sin_m, cos_g, sin_g


def _spread_rope(a):
    half = MLA_ROPE // 2
    z = jnp.zeros(a.shape[:-1] + (LANES // 2 - half,), a.dtype)
    return jnp.concatenate([a[..., :half], z, a[..., half:], z], axis=-1)


def _unspread_rope(a):
    half = MLA_ROPE // 2
    return jnp.concatenate([a[..., :half], a[..., LANES // 2:LANES // 2 + half]], axis=-1)


def _even_weights(layer, w_in, q_norm, w_qb, kv_norm, w_kvb, nope_g, rope_g, gqa_g, w_out):
    wq = w_qb.reshape(MLA_Q_RANK, MLA_HEADS, MLA_NOPE + MLA_ROPE)
    wq = jnp.concatenate([wq[..., :MLA_NOPE], _spread_rope(wq[..., MLA_NOPE:])], axis=-1)
    return {
        "layer": layer,
        "w_in": w_in,
        "w_qb": wq.reshape(MLA_Q_RANK, MLA_HEADS * MLA_QK_PAD).astype(BF16),
        "w_kvb": w_kvb.astype(BF16),
        "w_out": w_out,
        "q_norm": q_norm.reshape(1, -1),
        "kv_norm": kv_norm.reshape(1, -1),
        "nope_g": nope_g,
        "rope_g": _spread_rope(rope_g),
        "gqa_g": gqa_g,
    }


def kernel(x_prompt, x_sample, cache_mla_ckv, cache_mla_krope, cache_gqa_k, cache_gqa_v, state_ret, c, c_ctx, mod_w, mod_b, ffn_w_gate, ffn_w_up, ffn_w_down, ev_w_in, mla_q_norm, mla_w_qb, mla_kv_norm, mla_w_kvb, mla_nope_norm, mla_rope_norm, gqa_qk_norm, ev_w_out, od_w_in, ret_decay_logit, ret_norm, hy_conv_w, hy_conv_b, hy_filt_w1, hy_filt_b1, hy_filt_w2, hy_filt_b2, hy_filt_freq, hy_filt_w3, hy_decay, hy_skip, od_w_out):
    xc = x_prompt.reshape(BATCH * SEQ, D_MODEL)
    xs = x_sample.reshape(DEC_BATCH * DEC_SEQ, D_MODEL)

    cvec = jnp.concatenate([c_ctx[None, :], c, jnp.zeros((8 - 1 - DEC_BATCH, D_MODEL), F32)], axis=0)
    mod_all = _mod_vectors(cvec, mod_w, mod_b)

    tables = tuple(jnp.asarray(t) for t in _rope_tables(DEC_SEQ))

    outs = {}
    for l in range(DEPTH):
        mod_tab = mod_all[l, :1 + DEC_BATCH].reshape((1 + DEC_BATCH) * N_MOD, 1, D_MODEL)
        xc = _ffn(xc, mod_tab, False, 0, ffn_w_gate, ffn_w_up, ffn_w_down, l, 0)
        xs = _ffn(xs, mod_tab, True, 0, ffn_w_gate, ffn_w_up, ffn_w_down, l, 0)
        if l % 2 == 0:
            e = l // 2
            w = _even_weights(e, ev_w_in, mla_q_norm[e], mla_w_qb[e], mla_kv_norm[e], mla_w_kvb[e],
                              mla_nope_norm[e], mla_rope_norm[e], gqa_qk_norm[e], ev_w_out[e])
            qm, ckv, kr, qg, kg, vg = _even_proj(xc, mod_tab, w)
            km, v = _kv_expand(ckv, kr, w["w_kvb"], w["nope_g"], TM_PROJ)
            att = _attention(qm, qg, km, v, kg, vg, SEQ, SEQ)
            xc = _out_proj(xc, mod_tab, False, [att], w["w_out"])
            outs["ckv"], outs["krope"], outs["k"], outs["v"] = ckv, _unspread_rope(kr), kg, vg
            qm, ckv, kr, qg, kg, vg = _even_proj(xs, mod_tab, w, tables)
            km, v = _kv_expand(ckv, kr, w["w_kvb"], w["nope_g"], TM_PROJ)
            c_ckv = cache_mla_ckv[:, e].reshape(DEC_BATCH * PAST_LEN, MLA_KV_RANK)
            c_kr = _spread_rope(cache_mla_krope[:, e].reshape(DEC_BATCH * PAST_LEN, MLA_ROPE))
            kmc, vc = _kv_expand(c_ckv, c_kr, w["w_kvb"], w["nope_g"], PAST_LEN)
            kgc = cache_gqa_k[:, e].reshape(DEC_BATCH * PAST_LEN, GQA_KV_HEADS * HEAD_DIM)
            vgc = cache_gqa_v[:, e].reshape(DEC_BATCH * PAST_LEN, GQA_KV_HEADS * HEAD_DIM)
            att = _attention(qm, qg, km, v, kg, vg, DEC_SEQ, TQ_ATT, cache=(kmc, vc, kgc, vgc))
            xs = _out_proj(xs, mod_tab, True, [att], w["w_out"])
        else:
            o = l // 2
            w_out = od_w_out[o]
            logit = ret_decay_logit[o]
            ret_g = ret_norm[o].reshape(1, -1)
            skip = hy_skip[o].reshape(1, -1)
            conv_b = hy_conv_b[o].reshape(1, -1)
            filt_args = (hy_filt_w1[o], hy_filt_b1[o], hy_filt_w2[o], hy_filt_b2[o], hy_filt_freq[o],
                         hy_filt_w3[o], hy_decay[o])
            for stream in ("ctx", "smp"):
                x, seq, latent = (xc, SEQ, False) if stream == "ctx" else (xs, DEC_SEQ, True)
                fwd, inv = _dft_matrices(seq)
                fwd_b, inv_b = jnp.asarray(fwd).astype(BF16), jnp.asarray(inv).astype(BF16)
                spectra = _filter_spectra(seq, *filt_args, fwd_b, TC_HY)
                u = _odd_proj(x, mod_tab, latent, od_w_in, o)
                if stream == "ctx":
                    o_ret, st = _retention(u, logit, ret_g, seq, RET_HEADS)
                    outs["ret"] = st
                    o_hy = _hyena(u, seq, HY_CH, HY_CH, hy_conv_w[o], conv_b, spectra, skip, fwd_b, inv_b)
                else:
                    (o_ret,) = _retention(u, logit, ret_g, seq, RET_HEADS_LAT, s0=state_ret[:, o])
                    o_hy = _hyena(u, seq, TC_HY, HY_SUB_LAT, hy_conv_w[o], conv_b, spectra, skip, fwd_b, inv_b)
                x = _out_proj(x, mod_tab, latent, [o_ret, o_hy], w_out)
                if stream == "ctx":
                    xc = x
                else:
                    xs = x
        xc = _ffn(xc, mod_tab, False, 2, ffn_w_gate, ffn_w_up, ffn_w_down, l, 1)
        xs = _ffn(xs, mod_tab, True, 2, ffn_w_gate, ffn_w_up, ffn_w_down, l, 1)

    return (
        xc.reshape(BATCH, SEQ, D_MODEL),
        xs.reshape(DEC_BATCH, DEC_SEQ, D_MODEL),
        outs["ckv"].reshape(BATCH, 1, SEQ, MLA_KV_RANK),
        outs["krope"].reshape(BATCH, 1, SEQ, MLA_ROPE),
        outs["k"].reshape(BATCH, 1, SEQ, GQA_KV_HEADS, HEAD_DIM),
        outs["v"].reshape(BATCH, 1, SEQ, GQA_KV_HEADS, HEAD_DIM),
        outs["ret"].reshape(BATCH, 1, 2, RET_HEADS, RET_DK, RET_DV),
    )
```

```python
import functools
import math

import numpy as np
import jax
import jax.numpy as jnp
from jax import lax
from jax.experimental import pallas as pl
from jax.experimental.pallas import tpu as pltpu

F32 = jnp.float32
BF16 = jnp.bfloat16

D_MODEL = 2048
BATCH = 32
SEQ = 256
DEPTH = 2
DEC_BATCH = 2
DEC_SEQ = 1024
PAST_LEN = 256
GRID_W = 64
N_MOD = 9
D_FF = 5632
ROPE_THETA = 10000.0
EPS = 1e-6

MLA_HEADS = 8
MLA_Q_RANK = 512
MLA_KV_RANK = 512
MLA_NOPE = 128
MLA_ROPE = 64
MLA_V = 128
GQA_HEADS = 8
GQA_KV_HEADS = 2
GQA_GROUP = GQA_HEADS // GQA_KV_HEADS
HEAD_DIM = 128
MLA_SCALE = (MLA_NOPE + MLA_ROPE) ** -0.5
GQA_SCALE = HEAD_DIM ** -0.5
MLA_QK_PAD = 256

RET_HEADS = 8
RET_DK = 128
RET_DV = 128
HY_CH = 1024
HY_ORDER = 2
HY_BANDS = 16
HY_FHID = 64
ODD_IN = 4 * RET_HEADS * RET_DK + (HY_ORDER + 1) * HY_CH

V7X_VMEM_BYTES = 64 * 1024 * 1024
LANES = 128
VMEM_LIMIT = V7X_VMEM_BYTES - 3 * 1024 * 1024

TM_FFN = 1024
TF_FFN = 512
TF_SUB = 256
TM_PROJ = 512
TM_ODD_CTX, TN_ODD_CTX = 2048, 512
TM_ODD_LAT, TN_ODD_LAT = 1024, 1024
TN_MOD = 1024
TQ_ATT = 256
TC_HY = 512
HY_SUB_LAT = 256
RET_HEADS_LAT = 2


def _cparams(*sem):
    return pltpu.CompilerParams(dimension_semantics=sem, vmem_limit_bytes=VMEM_LIMIT)


def _const_spec(shape):
    nd = len(shape)
    return pl.BlockSpec(shape, lambda *_: (0,) * nd)


def _silu(x):
    return x * jax.nn.sigmoid(x)


def _rms_rows(x, width=None):
    ss = jnp.sum(x * x, axis=-1, keepdims=True)
    n = x.shape[-1] if width is None else width
    return x * lax.rsqrt(ss * (1.0 / n) + EPS)


def _mod_kernel(c_ref, w_ref, b_ref, o_ref):
    s = _silu(c_ref[...]).astype(BF16)
    o_ref[0] = jnp.dot(s, w_ref[0].astype(BF16), preferred_element_type=F32) + b_ref[0]


def _mod_vectors(cvec, mod_w, mod_b):
    depth, d, n = mod_w.shape
    rows = cvec.shape[0]
    return pl.pallas_call(
        _mod_kernel,
        out_shape=jax.ShapeDtypeStruct((depth, rows, n), F32),
        grid=(depth, n // TN_MOD),
        in_specs=[
            pl.BlockSpec((rows, d), lambda l, j: (0, 0)),
            pl.BlockSpec((1, d, TN_MOD), lambda l, j: (l, 0, j)),
            pl.BlockSpec((1, 1, TN_MOD), lambda l, j: (l, 0, j)),
        ],
        out_specs=pl.BlockSpec((1, rows, TN_MOD), lambda l, j: (l, 0, j)),
        compiler_params=_cparams("parallel", "parallel"),
        name="mod_vectors",
    )(cvec, mod_w, mod_b.reshape(depth, 1, n))


def _mod_specs(sub, latent, tile, extra_axes=0):
    group_of_tile = (lambda i: 1 + (i * tile) // DEC_SEQ) if latent else (lambda i: 0)

    def spec(k):
        if extra_axes:
            return pl.BlockSpec((1, 1, D_MODEL), lambda i, j: (group_of_tile(i) * N_MOD + 3 * sub + k, 0, 0))
        return pl.BlockSpec((1, 1, D_MODEL), lambda i: (group_of_tile(i) * N_MOD + 3 * sub + k, 0, 0))
    return [spec(0), spec(1), spec(2)]


def _modulated(x, sh_ref, sc_ref):
    return (_rms_rows(x) * (1.0 + sc_ref[0]) + sh_ref[0]).astype(BF16)


def _ffn_kernel(x_hbm, sh_ref, sc_ref, gt_ref, wg_ref, wu_ref, wd_ref, o_ref, x_buf, h_ref, x_sem):
    i, f = pl.program_id(0), pl.program_id(1)
    tm = x_buf.shape[0]
    half_gate = 0.5 * gt_ref[0]

    def x_copy(tile):
        return pltpu.make_async_copy(x_hbm.at[pl.ds(tile * tm, tm), :], x_buf, x_sem)

    def partial_out(h):
        out = None
        for c in range(TF_FFN // TF_SUB):
            cs = slice(c * TF_SUB, (c + 1) * TF_SUB)
            g = jnp.dot(h, wg_ref[:, cs].astype(BF16), preferred_element_type=F32)
            u = jnp.dot(h, wu_ref[:, cs].astype(BF16), preferred_element_type=F32)
            a = (_silu(g) * u).astype(BF16)
            t = jnp.dot(a, wd_ref[cs, :].astype(BF16), preferred_element_type=F32)
            out = t if out is None else out + t
        return half_gate * out

    @pl.when(f == 0)
    def _():
        @pl.when(i == 0)
        def _():
            x_copy(0).start()

        x_copy(i).wait()
        x = x_buf[...]
        h = _modulated(x, sh_ref, sc_ref)
        h_ref[...] = h
        o_ref[...] = x + partial_out(h)

    @pl.when(f > 0)
    def _():
        @pl.when(jnp.logical_and(f == pl.num_programs(1) // 2, i + 1 < pl.num_programs(0)))
        def _():
            x_copy(i + 1).start()

        o_ref[...] += partial_out(h_ref[...])


def _ffn(x, mod_tab, latent, sub, w_gate, w_up, w_down, layer, j):
    m, d = x.shape
    dff = w_gate.shape[-1]
    assert dff // TF_FFN >= 2
    return pl.pallas_call(
        _ffn_kernel,
        out_shape=jax.ShapeDtypeStruct((m, d), F32),
        grid=(m // TM_FFN, dff // TF_FFN),
        in_specs=[pl.BlockSpec(memory_space=pl.ANY)]
        + _mod_specs(sub, latent, TM_FFN, extra_axes=1)
        + [
            pl.BlockSpec((None, None, d, TF_FFN), lambda i, f: (layer, j, 0, f)),
            pl.BlockSpec((None, None, d, TF_FFN), lambda i, f: (layer, j, 0, f)),
            pl.BlockSpec((None, None, TF_FFN, d), lambda i, f: (layer, j, f, 0)),
        ],
        out_specs=pl.BlockSpec((TM_FFN, d), lambda i, f: (i, 0)),
        scratch_shapes=[pltpu.VMEM((TM_FFN, d), F32), pltpu.VMEM((TM_FFN, d), BF16), pltpu.SemaphoreType.DMA(())],
        compiler_params=_cparams("arbitrary", "arbitrary"),
        name="ffn",
    )(x, mod_tab, mod_tab, mod_tab, w_gate, w_up, w_down)


def _out_proj_kernel(*refs, n_parts):
    x_ref, gt_ref, w_ref = refs[0], refs[1], refs[2]
    part_refs = refs[3:3 + n_parts]
    o_ref, wb_ref = refs[3 + n_parts], refs[4 + n_parts]

    @pl.when(pl.program_id(0) == 0)
    def _():
        wb_ref[...] = w_ref[...].astype(BF16)

    acc, off = None, 0
    for p_ref in part_refs:
        width = p_ref.shape[1]
        t = jnp.dot(p_ref[...], wb_ref[off:off + width, :], preferred_element_type=F32)
        acc = t if acc is None else acc + t
        off += width
    o_ref[...] = x_ref[...] + gt_ref[0] * acc


def _out_proj(x, mod_tab, latent, parts, w_out):
    m, d = x.shape
    return pl.pallas_call(
        functools.partial(_out_proj_kernel, n_parts=len(parts)),
        out_shape=jax.ShapeDtypeStruct((m, d), F32),
        grid=(m // TM_PROJ,),
        in_specs=[pl.BlockSpec((TM_PROJ, d), lambda i: (i, 0)), _mod_specs(1, latent, TM_PROJ)[2],
                  pl.BlockSpec(w_out.shape, lambda i: (0, 0), pipeline_mode=pl.Buffered(1))]
        + [pl.BlockSpec((TM_PROJ, p.shape[1]), lambda i: (i, 0)) for p in parts],
        out_specs=pl.BlockSpec((TM_PROJ, d), lambda i: (i, 0)),
        scratch_shapes=[pltpu.VMEM(w_out.shape, BF16)],
        compiler_params=_cparams("arbitrary"),
        name="mixer_out_proj",
    )(x, mod_tab, w_out, *parts)


EV_QA, EV_KVA, EV_QG, EV_KG, EV_VG, EV_KR, EV_END = 0, 512, 1024, 2048, 2304, 2560, 2688


def _rotate_half(x, cos, sin):
    return x * cos + pltpu.roll(x, LANES // 2, axis=1) * sin


def _even_proj_kernel(*refs, rotary):
    (x_ref, sh_ref, sc_ref, w_in_ref, w_qb_ref, qn_ref, kvn_ref, ng_ref, rg_ref, gg_ref) = refs[:10]
    if rotary:
        cm_ref, sm_ref, cg_ref, sg_ref = refs[10:14]
        rot_m = lambda t: _rotate_half(t, cm_ref[...], sm_ref[...])
        rot_g = lambda t: _rotate_half(t, cg_ref[...], sg_ref[...])
    else:
        rot_m = rot_g = lambda t: t
    qm_ref, ckv_ref, kr_ref, qg_ref, kg_ref, vg_ref, wb_ref = refs[-7:]

    @pl.when(pl.program_id(0) == 0)
    def _():
        chunk = 128
        half = MLA_ROPE // 2

        def copy_rows(dst0, src0, n_chunks):
            def body(c, carry):
                dst = pl.ds(pl.multiple_of(dst0 + c * chunk, 16), chunk)
                src = pl.ds(pl.multiple_of(src0 + c * chunk, 8), chunk)
                wb_ref[dst, :] = w_in_ref[src, :].astype(BF16)
                return carry
            lax.fori_loop(0, n_chunks, body, 0)

        copy_rows(EV_QA, 0, EV_QG // chunk)
        copy_rows(EV_QG, EV_QG + MLA_ROPE, (EV_KR - EV_QG) // chunk)
        wb_ref[EV_KR:EV_END, :] = jnp.zeros((EV_END - EV_KR, wb_ref.shape[1]), BF16)
        wb_ref[EV_KR:EV_KR + half, :] = w_in_ref[EV_QG:EV_QG + half, :].astype(BF16)
        wb_ref[EV_KR + LANES // 2:EV_KR + LANES // 2 + half, :] = w_in_ref[EV_QG + half:EV_QG + MLA_ROPE, :].astype(BF16)

    h = _modulated(x_ref[...], sh_ref, sc_ref)
    u = _qk(h, wb_ref[...])

    nope_gain = ng_ref[0:1, :] * MLA_SCALE
    rope_gain = rg_ref[0:1, :] * MLA_SCALE
    gqa_gain = gg_ref[0:1, :] * GQA_SCALE

    qa = (_rms_rows(u[:, EV_QA:EV_KVA]) * qn_ref[...]).astype(BF16)
    q = jnp.dot(qa, w_qb_ref[...], preferred_element_type=F32)
    for hd in range(MLA_HEADS):
        c0 = hd * MLA_QK_PAD
        nope = _rms_rows(q[:, c0:c0 + MLA_NOPE]) * nope_gain
        rope = _rms_rows(q[:, c0 + MLA_NOPE:c0 + MLA_QK_PAD], MLA_ROPE) * rope_gain
        qm_ref[:, c0:c0 + MLA_NOPE] = nope.astype(BF16)
        qm_ref[:, c0 + MLA_NOPE:c0 + MLA_QK_PAD] = rot_m(rope).astype(BF16)

    ckv_ref[...] = _rms_rows(u[:, EV_KVA:EV_QG]) * kvn_ref[...]
    kr_ref[...] = rot_m(_rms_rows(u[:, EV_KR:EV_END], MLA_ROPE) * rg_ref[1:2, :])

    for hd in range(GQA_HEADS):
        c0 = EV_QG + hd * HEAD_DIM
        qh = _rms_rows(u[:, c0:c0 + HEAD_DIM]) * gqa_gain
        qg_ref[:, hd * HEAD_DIM:(hd + 1) * HEAD_DIM] = rot_g(qh).astype(BF16)
    for hd in range(GQA_KV_HEADS):
        c0 = EV_KG + hd * HEAD_DIM
        kh = _rms_rows(u[:, c0:c0 + HEAD_DIM]) * gg_ref[1:2, :]
        kg_ref[:, hd * HEAD_DIM:(hd + 1) * HEAD_DIM] = rot_g(kh)
    vg_ref[...] = u[:, EV_VG:EV_KR]


def _even_proj(x, mod_tab, w, tables=None):
    m, d = x.shape
    tm = TM_PROJ
    row = lambda w_: pl.BlockSpec((tm, w_), lambda i: (i, 0))
    tab_specs, tab_args = [], []
    if tables is not None:
        tiles_per_seq = tables[0].shape[0] // tm
        tab_specs = [pl.BlockSpec((tm, LANES), lambda i: (i % tiles_per_seq, 0))] * 4
        tab_args = list(tables)
    return pl.pallas_call(
        functools.partial(_even_proj_kernel, rotary=tables is not None),
        out_shape=[
            jax.ShapeDtypeStruct((m, MLA_HEADS * MLA_QK_PAD), BF16),
            jax.ShapeDtypeStruct((m, MLA_KV_RANK), F32),
            jax.ShapeDtypeStruct((m, LANES), F32),
            jax.ShapeDtypeStruct((m, GQA_HEADS * HEAD_DIM), BF16),
            jax.ShapeDtypeStruct((m, GQA_KV_HEADS * HEAD_DIM), F32),
            jax.ShapeDtypeStruct((m, GQA_KV_HEADS * HEAD_DIM), F32),
        ],
        grid=(m // tm,),
        in_specs=[row(d)] + _mod_specs(1, tables is not None, tm)[:2]
        + [pl.BlockSpec((None,) + w["w_in"].shape[1:], lambda i: (w["layer"], 0, 0)),
           _const_spec(w["w_qb"].shape), _const_spec(w["q_norm"].shape),
           _const_spec(w["kv_norm"].shape), _const_spec(w["nope_g"].shape), _const_spec(w["rope_g"].shape),
           _const_spec(w["gqa_g"].shape)] + tab_specs,
        out_specs=[row(MLA_HEADS * MLA_QK_PAD), row(MLA_KV_RANK), row(LANES), row(GQA_HEADS * HEAD_DIM),
                   row(GQA_KV_HEADS * HEAD_DIM), row(GQA_KV_HEADS * HEAD_DIM)],
        scratch_shapes=[pltpu.VMEM((EV_END, d), BF16)],
        compiler_params=_cparams("arbitrary"),
        name="even_proj",
    )(x, mod_tab, mod_tab, w["w_in"], w["w_qb"], w["q_norm"], w["kv_norm"], w["nope_g"], w["rope_g"], w["gqa_g"],
      *tab_args)


def _kv_expand_kernel(ckv_ref, kr_ref, w_ref, ng_ref, km_ref, v_ref):
    kv = jnp.dot(ckv_ref[...].astype(BF16), w_ref[...], preferred_element_type=F32)
    kr = kr_ref[...].astype(BF16)
    for hd in range(MLA_HEADS):
        c0 = hd * (MLA_NOPE + MLA_V)
        km_ref[:, hd * MLA_QK_PAD:hd * MLA_QK_PAD + MLA_NOPE] = (
            _rms_rows(kv[:, c0:c0 + MLA_NOPE]) * ng_ref[1:2, :]).astype(BF16)
        km_ref[:, hd * MLA_QK_PAD + MLA_NOPE:(hd + 1) * MLA_QK_PAD] = kr
        v_ref[:, hd * MLA_V:(hd + 1) * MLA_V] = kv[:, c0 + MLA_NOPE:c0 + MLA_NOPE + MLA_V].astype(BF16)


def _kv_expand(ckv, kr, w_kvb_bf16, nope_g, tr):
    r = ckv.shape[0]
    row = lambda w_: pl.BlockSpec((tr, w_), lambda i: (i, 0))
    return pl.pallas_call(
        _kv_expand_kernel,
        out_shape=[jax.ShapeDtypeStruct((r, MLA_HEADS * MLA_QK_PAD), BF16),
                   jax.ShapeDtypeStruct((r, MLA_HEADS * MLA_V), BF16)],
        grid=(r // tr,),
        in_specs=[row(MLA_KV_RANK), row(LANES), _const_spec(w_kvb_bf16.shape), _const_spec(nope_g.shape)],
        out_specs=[row(MLA_HEADS * MLA_QK_PAD), row(MLA_HEADS * MLA_V)],
        compiler_params=_cparams("parallel"),
        name="kv_expand",
    )(ckv, kr, w_kvb_bf16, nope_g)


def _softmax_pv(score_list, value_list):
    mx = None
    for s in score_list:
        m_ = jnp.max(s, axis=-1, keepdims=True)
        mx = m_ if mx is None else jnp.maximum(mx, m_)
    den, acc = None, None
    for s, v in zip(score_list, value_list):
        p = jnp.exp(s - mx)
        l_ = jnp.sum(p, axis=-1, keepdims=True)
        o_ = jnp.dot(p.astype(BF16), v, preferred_element_type=F32)
        den = l_ if den is None else den + l_
        acc = o_ if acc is None else acc + o_
    return acc / den


def _qk(q, k):
    return lax.dot_general(q, k, (((1,), (1,)), ((), ())), preferred_element_type=F32)


def _attn_kernel(*refs, has_cache):
    if has_cache:
        qm_ref, qg_ref, km_ref, v_ref, kg_ref, vg_ref, kmc_ref, vc_ref, kgc_ref, vgc_ref, o_ref = refs
    else:
        qm_ref, qg_ref, km_ref, v_ref, kg_ref, vg_ref, o_ref = refs
    for hd in range(MLA_HEADS):
        qs = slice(hd * MLA_QK_PAD, (hd + 1) * MLA_QK_PAD)
        vs = slice(hd * MLA_V, (hd + 1) * MLA_V)
        q = qm_ref[:, qs]
        scores = [_qk(q, km_ref[:, qs])]
        values = [v_ref[:, vs]]
        if has_cache:
            scores.append(_qk(q, kmc_ref[:, qs]))
            values.append(vc_ref[:, vs])
        o_ref[:, vs] = _softmax_pv(scores, values).astype(o_ref.dtype)
    base = MLA_HEADS * MLA_V
    for kvh in range(GQA_KV_HEADS):
        ks = slice(kvh * HEAD_DIM, (kvh + 1) * HEAD_DIM)
        k = kg_ref[:, ks].astype(BF16)
        v = vg_ref[:, ks].astype(BF16)
        if has_cache:
            kc = kgc_ref[:, ks].astype(BF16)
            vc = vgc_ref[:, ks].astype(BF16)
        for g in range(GQA_GROUP):
            hd = kvh * GQA_GROUP + g
            hs = slice(hd * HEAD_DIM, (hd + 1) * HEAD_DIM)
            q = qg_ref[:, hs]
            scores, values = [_qk(q, k)], [v]
            if has_cache:
                scores.append(_qk(q, kc))
                values.append(vc)
            o_ref[:, base + hd * HEAD_DIM:base + (hd + 1) * HEAD_DIM] = _softmax_pv(scores, values).astype(o_ref.dtype)


def _attention(qm, qg, km, v, kg, vg, seq, tq, cache=None):
    m = qm.shape[0]
    nq = seq // tq
    qrow = lambda w_: pl.BlockSpec((tq, w_), lambda i: (i, 0))
    krow = lambda w_: pl.BlockSpec((seq, w_), lambda i: (i // nq, 0))
    in_specs = [qrow(qm.shape[1]), qrow(qg.shape[1]), krow(km.shape[1]), krow(v.shape[1]),
                krow(kg.shape[1]), krow(vg.shape[1])]
    args = [qm, qg, km, v, kg, vg]
    if cache is not None:
        past = cache[0].shape[0] // (m // seq)
        crow = lambda w_: pl.BlockSpec((past, w_), lambda i: (i // nq, 0))
        in_specs += [crow(c.shape[1]) for c in cache]
        args += list(cache)
    width = MLA_HEADS * MLA_V + GQA_HEADS * HEAD_DIM
    return pl.pallas_call(
        functools.partial(_attn_kernel, has_cache=cache is not None),
        out_shape=jax.ShapeDtypeStruct((m, width), BF16),
        grid=(m // tq,),
        in_specs=in_specs,
        out_specs=qrow(width),
        compiler_params=_cparams("parallel"),
        name="attention",
    )(*args)


def _odd_proj_kernel(x_hbm, sh_ref, sc_ref, w_ref, o_ref, x_buf, h_ref, x_sem):
    i, j = pl.program_id(0), pl.program_id(1)
    tm = x_buf.shape[0]

    def x_copy(tile):
        return pltpu.make_async_copy(x_hbm.at[pl.ds(tile * tm, tm), :], x_buf, x_sem)

    def project(h):
        return jnp.dot(h, w_ref[...].astype(BF16), preferred_element_type=F32).astype(o_ref.dtype)

    @pl.when(j == 0)
    def _():
        @pl.when(i == 0)
        def _():
            x_copy(0).start()

        x_copy(i).wait()
        h = _modulated(x_buf[...], sh_ref, sc_ref)
        h_ref[...] = h
        o_ref[...] = project(h)

    @pl.when(j > 0)
    def _():
        @pl.when(jnp.logical_and(j == pl.num_programs(1) // 2, i + 1 < pl.num_programs(0)))
        def _():
            x_copy(i + 1).start()

        o_ref[...] = project(h_ref[...])


def _odd_proj(x, mod_tab, latent, w_in, layer):
    m, d = x.shape
    n = w_in.shape[-1]
    tm, tn = (TM_ODD_LAT, TN_ODD_LAT) if latent else (TM_ODD_CTX, TN_ODD_CTX)
    assert n // tn >= 2
    return pl.pallas_call(
        _odd_proj_kernel,
        out_shape=jax.ShapeDtypeStruct((m, n), BF16),
        grid=(m // tm, n // tn),
        in_specs=[pl.BlockSpec(memory_space=pl.ANY)]
        + _mod_specs(1, latent, tm, extra_axes=1)[:2]
        + [pl.BlockSpec((None, d, tn), lambda i, j: (layer, 0, j))],
        out_specs=pl.BlockSpec((tm, tn), lambda i, j: (i, j)),
        scratch_shapes=[pltpu.VMEM((tm, d), F32), pltpu.VMEM((tm, d), BF16), pltpu.SemaphoreType.DMA(())],
        compiler_params=_cparams("arbitrary", "arbitrary"),
        name="odd_proj",
    )(x, mod_tab, mod_tab, w_in)


def _log_sigmoid(x):
    return jnp.minimum(x, 0.0) - jnp.log1p(jnp.exp(-jnp.abs(x)))


def _retention_kernel(*refs, heads, has_init):
    if has_init:
        logit_ref, q_ref, k_ref, v_ref, g_ref, rg_ref, s0_ref, o_ref, dec_ref, vec_ref = refs
    else:
        logit_ref, q_ref, k_ref, v_ref, g_ref, rg_ref, o_ref, st_ref, dec_ref, vec_ref = refs
    length = q_ref.shape[0]
    h0 = pl.program_id(0) * heads
    scale = RET_DK ** -0.5
    log_g = _log_sigmoid(logit_ref[...])
    lane_h = lax.broadcasted_iota(jnp.int32, (2, RET_HEADS), 1)

    def head_log_decays(hh):
        lg = jnp.sum(jnp.where(lane_h == h0 + hh, log_g, 0.0), axis=1, keepdims=True)
        return lg[0:1, :], lg[1:2, :]

    @pl.when(pl.program_id(1) == 0)
    def _():
        n_i = lax.broadcasted_iota(jnp.int32, (length, length), 0)
        m_i = lax.broadcasted_iota(jnp.int32, (length, length), 1)
        diff = (n_i - m_i).astype(F32)
        pos = lax.broadcasted_iota(jnp.int32, (length, RET_DK), 0).astype(F32)
        for hh in range(heads):
            lg_f, lg_b = head_log_decays(hh)
            dec_ref[hh] = scale * (jnp.where(diff >= 0, jnp.exp(jnp.maximum(diff, 0.0) * lg_f), 0.0)
                                   + jnp.where(diff <= 0, jnp.exp(jnp.maximum(-diff, 0.0) * lg_b), 0.0))
            if has_init:
                vec_ref[4 * hh + 2] = jnp.exp((pos + 1.0) * lg_f)
                vec_ref[4 * hh + 3] = jnp.exp((length - pos) * lg_b)
            else:
                vec_ref[4 * hh + 0] = jnp.exp((length - 1.0 - pos) * lg_f) * scale
                vec_ref[4 * hh + 1] = jnp.exp(pos * lg_b) * scale

    tn = (((0,), (0,)), ((), ()))
    for hh in range(heads):
        cs = slice(hh * RET_DK, (hh + 1) * RET_DK)
        q, k, v = q_ref[:, cs], k_ref[:, cs], v_ref[:, cs]
        att = _qk(q, k) * dec_ref[hh]
        o = jnp.dot(att.astype(BF16), v, preferred_element_type=F32)
        if has_init:
            s0_f, s0_b = s0_ref[0, 0, hh].astype(BF16), s0_ref[0, 1, hh].astype(BF16)
            o = o + jnp.dot(q, s0_f, preferred_element_type=F32) * vec_ref[4 * hh + 2]
            o = o + jnp.dot(q, s0_b, preferred_element_type=F32) * vec_ref[4 * hh + 3]
        else:
            kf = k.astype(F32)
            k_f, k_b = (kf * vec_ref[4 * hh + 0]).astype(BF16), (kf * vec_ref[4 * hh + 1]).astype(BF16)
            st_ref[0, 0, hh] = lax.dot_general(k_f, v, tn, preferred_element_type=F32)
            st_ref[0, 1, hh] = lax.dot_general(k_b, v, tn, preferred_element_type=F32)
        gate = g_ref[:, cs].astype(F32)
        o_ref[:, cs] = (_rms_rows(o) * rg_ref[:, cs] * _silu(gate)).astype(o_ref.dtype)


def _retention(u, decay_logit, ret_g, seq, heads, s0=None):
    m = u.shape[0]
    b = m // seq
    hw = heads * RET_DK
    nh = RET_HEADS // heads
    blocks_per_part = RET_HEADS * RET_DK // hw
    col = lambda part: pl.BlockSpec((seq, hw), lambda j, i, part=part: (i, part * blocks_per_part + j))
    st_spec = pl.BlockSpec((1, 2, heads, RET_DK, RET_DV), lambda j, i: (i, 0, j, 0, 0))
    in_specs = [_const_spec(decay_logit.shape), col(0), col(1), col(2), col(3),
                pl.BlockSpec((1, hw), lambda j, i: (0, j))]
    args = [decay_logit, u, u, u, u, ret_g]
    if s0 is not None:
        in_specs.append(st_spec)
        args.append(s0)
    out_shape = [jax.ShapeDtypeStruct((m, RET_HEADS * RET_DV), BF16)]
    out_specs = [pl.BlockSpec((seq, hw), lambda j, i: (i, j))]
    if s0 is None:
        out_shape.append(jax.ShapeDtypeStruct((b, 2, RET_HEADS, RET_DK, RET_DV), F32))
        out_specs.append(st_spec)
    return pl.pallas_call(
        functools.partial(_retention_kernel, heads=heads, has_init=s0 is not None),
        out_shape=out_shape,
        grid=(nh, b),
        in_specs=in_specs,
        out_specs=out_specs,
        scratch_shapes=[pltpu.VMEM((heads, seq, seq), F32), pltpu.VMEM((4 * heads, seq, RET_DK), F32)],
        compiler_params=_cparams("parallel", "arbitrary"),
        name="retention",
    )(*args)


@functools.lru_cache(maxsize=None)
def _dft_matrices(length):
    n = 2 * length
    f = np.arange(length, dtype=np.float64)[:, None]
    t = np.arange(length, dtype=np.float64)[None, :]
    ang = 2.0 * np.pi * f * t / n
    fwd_a = np.cos(ang)
    fwd_b = np.sin(ang)
    fwd_b[0, :] = np.cos(np.pi * t[0])
    fwd = np.concatenate([fwd_a, fwd_b], axis=0)
    tt = (np.arange(length, dtype=np.float64) + length // 2)[:, None]
    ff = np.arange(length, dtype=np.float64)[None, :]
    ang_i = 2.0 * np.pi * ff * tt / n
    inv_a = 2.0 * np.cos(ang_i) / n
    inv_b = 2.0 * np.sin(ang_i) / n
    inv_a[:, 0] = 1.0 / n
    inv_b[:, 0] = np.cos(np.pi * tt[:, 0]) / n
    inv = np.concatenate([inv_a, inv_b], axis=1)
    return fwd.astype(np.float32), inv.astype(np.float32)


@functools.lru_cache(maxsize=None)
def _filter_features(length):
    t = np.arange(length, dtype=np.float64)
    tn = t / length
    bands = np.arange(1, HY_BANDS + 1, dtype=np.float64)
    ang = 2.0 * np.pi * tn[:, None] * bands[None, :]
    feat = np.concatenate([tn[:, None], np.sin(ang), np.cos(ang)], axis=-1)
    feat = np.pad(feat, ((0, 0), (0, LANES - feat.shape[1])))
    r = (np.abs(t - length // 2) / (length / 2))[:, None]
    return feat.astype(np.float32), r.astype(np.float32)


def _filter_spec_kernel(feat_ref, r_ref, w1_ref, b1_ref, w2_ref, b2_ref, fr_ref, w3_ref, dec_ref, fwd_ref, o_ref,
                        z_ref):
    hp = lax.Precision.HIGHEST

    @pl.when(pl.program_id(0) == 0)
    def _():
        z = jnp.sin(fr_ref[0:1, :] * (jnp.dot(feat_ref[...], w1_ref[...], precision=hp, preferred_element_type=F32)
                                      + b1_ref[...]))
        z_ref[...] = jnp.sin(fr_ref[1:2, :] * (jnp.dot(z, w2_ref[...], precision=hp, preferred_element_type=F32)
                                               + b2_ref[...]))

    filt = jnp.dot(z_ref[...], w3_ref[...], precision=hp, preferred_element_type=F32)
    filt = filt * jnp.exp(-r_ref[...] * jnp.abs(dec_ref[...]))
    spec = jnp.dot(fwd_ref[...], filt.astype(BF16), preferred_element_type=F32)
    length = spec.shape[0] // 2
    sa, sb = spec[:length], spec[length:]
    first = lax.broadcasted_iota(jnp.int32, sa.shape, 0) == 0
    o_ref[0:length, :] = sa
    o_ref[length:2 * length, :] = jnp.where(first, 0.0, sb)
    o_ref[2 * length:, :] = jnp.where(first, sb, sa)


def _filter_spectra(length, w1, b1, w2, b2, freq, w3, decay, fwd_bf16, tc):
    feat, r = _filter_features(length)
    nch = w3.shape[1]
    w1p = jnp.pad(w1, ((0, LANES - w1.shape[0]), (0, 0)))
    return pl.pallas_call(
        _filter_spec_kernel,
        out_shape=jax.ShapeDtypeStruct((3 * length, nch), F32),
        grid=(nch // tc,),
        in_specs=[_const_spec(feat.shape), _const_spec(r.shape), _const_spec(w1p.shape), _const_spec((1, HY_FHID)),
                  _const_spec(w2.shape), _const_spec((1, HY_FHID)), _const_spec(freq.shape),
                  pl.BlockSpec((HY_FHID, tc), lambda j: (0, j)), pl.BlockSpec((1, tc), lambda j: (0, j)),
                  _const_spec(fwd_bf16.shape)],
        out_specs=pl.BlockSpec((3 * length, tc), lambda j: (0, j)),
        scratch_shapes=[pltpu.VMEM((length, HY_FHID), F32)],
        compiler_params=_cparams("arbitrary"),
        name="hyena_filter_spectra",
    )(jnp.asarray(feat), jnp.asarray(r), w1p, b1.reshape(1, -1), w2, b2.reshape(1, -1), freq, w3,
      decay.reshape(1, -1), fwd_bf16)


def _hyena_kernel(v_ref, x1_ref, x2_ref, cwv_ref, cw1_ref, cw2_ref, cbv_ref, cb1_ref, cb2_ref,
                  h1_ref, h2_ref, sk1_ref, sk2_ref, fwd_ref, inv_ref, o_ref, *, sub):
    length, tc = v_ref.shape
    row = lax.broadcasted_iota(jnp.int32, (length, sub), 0)
    first, last = row == 0, row == length - 1

    for c in range(tc // sub):
        cs = slice(c * sub, (c + 1) * sub)

        def short_conv(x_ref, w_ref, b_ref):
            x = x_ref[:, cs].astype(F32)
            prev = jnp.where(first, 0.0, pltpu.roll(x, 1, axis=0))
            nxt = jnp.where(last, 0.0, pltpu.roll(x, length - 1, axis=0))
            return prev * w_ref[0:1, cs] + x * w_ref[1:2, cs] + nxt * w_ref[2:3, cs] + b_ref[:, cs]

        def long_conv(z, h_ref):
            zs = jnp.dot(fwd_ref[...], z.astype(BF16), preferred_element_type=F32)
            za, zb = zs[:length], zs[length:]
            ha, hb, hc = h_ref[0:length, cs], h_ref[length:2 * length, cs], h_ref[2 * length:, cs]
            y = jnp.concatenate([za * ha - zb * hb, za * hb + zb * hc], axis=0).astype(BF16)
            return jnp.dot(inv_ref[...], y, preferred_element_type=F32)

        v = short_conv(v_ref, cwv_ref, cbv_ref)
        x1 = short_conv(x1_ref, cw1_ref, cb1_ref)
        x2 = short_conv(x2_ref, cw2_ref, cb2_ref)
        z = x1 * (long_conv(v, h1_ref) + sk1_ref[:, cs] * v)
        o_ref[:, cs] = (x2 * (long_conv(z, h2_ref) + sk2_ref[:, cs] * z)).astype(o_ref.dtype)


def _hyena(u, seq, tc, sub, conv_w, conv_b, spectra, skip, fwd_bf16, inv_bf16):
    m = u.shape[0]
    b = m // seq
    nc = HY_CH // tc
    hy0 = 4 * RET_HEADS * RET_DK // tc
    ucol = lambda part: pl.BlockSpec((seq, tc), lambda i, j, part=part: (i, hy0 + part * nc + j))
    wcol = lambda rows, part: pl.BlockSpec((rows, tc), lambda i, j, part=part: (0, part * nc + j))
    return pl.pallas_call(
        functools.partial(_hyena_kernel, sub=sub),
        out_shape=jax.ShapeDtypeStruct((m, HY_CH), BF16),
        grid=(b, nc),
        in_specs=[ucol(0), ucol(1), ucol(2), wcol(3, 0), wcol(3, 1), wcol(3, 2), wcol(1, 0), wcol(1, 1), wcol(1, 2),
                  wcol(3 * seq, 0), wcol(3 * seq, 1), wcol(1, 0), wcol(1, 1),
                  _const_spec(fwd_bf16.shape), _const_spec(inv_bf16.shape)],
        out_specs=pl.BlockSpec((seq, tc), lambda i, j: (i, j)),
        compiler_params=_cparams("parallel", "parallel"),
        name="hyena",
    )(u, u, u, conv_w, conv_w, conv_w, conv_b, conv_b, conv_b, spectra, spectra, skip, skip, fwd_bf16, inv_bf16)


@functools.lru_cache(maxsize=None)
def _rope_tables(length):
    rows = np.repeat(np.arange(length // GRID_W, dtype=np.float64), GRID_W)
    cols = np.tile(np.arange(GRID_W, dtype=np.float64), length // GRID_W)

    def pack(dim):
        half = dim // 2
        freq = ROPE_THETA ** (-np.arange(0, half, 2, dtype=np.float64) / half)
        ang = np.concatenate([rows[:, None] * freq[None], cols[:, None] * freq[None]], axis=-1)
        pad = ((0, 0), (0, LANES // 2 - half))
        cos = np.pad(np.cos(ang), pad, constant_values=1.0)
        sin = np.pad(np.sin(ang), pad)
        return (np.concatenate([cos, cos], axis=-1).astype(np.float32),
                np.concatenate([-sin, sin], axis=-1).astype(np.float32))

    cos_m, sin_m = pack(MLA_ROPE)
    cos_g, sin_g = pack(HEAD_DIM)
    return cos_m, sin_m, cos_g, sin_g


def _spread_rope(a):
    half = MLA_ROPE // 2
    z = jnp.zeros(a.shape[:-1] + (LANES // 2 - half,), a.dtype)
    return jnp.concatenate([a[..., :half], z, a[..., half:], z], axis=-1)


def _unspread_rope(a):
    half = MLA_ROPE // 2
    return jnp.concatenate([a[..., :half], a[..., LANES // 2:LANES // 2 + half]], axis=-1)


def _even_weights(layer, w_in, q_norm, w_qb, kv_norm, w_kvb, nope_g, rope_g, gqa_g, w_out):
    wq = w_qb.reshape(MLA_Q_RANK, MLA_HEADS, MLA_NOPE + MLA_ROPE)
    wq = jnp.concatenate([wq[..., :MLA_NOPE], _spread_rope(wq[..., MLA_NOPE:])], axis=-1)
    return {
        "layer": layer,
        "w_in": jnp.swapaxes(w_in, 1, 2),
        "w_qb": wq.reshape(MLA_Q_RANK, MLA_HEADS * MLA_QK_PAD).astype(BF16),
        "w_kvb": w_kvb.astype(BF16),
        "w_out": w_out,
        "q_norm": q_norm.reshape(1, -1),
        "kv_norm": kv_norm.reshape(1, -1),
        "nope_g": nope_g,
        "rope_g": _spread_rope(rope_g),
        "gqa_g": gqa_g,
    }


def kernel(x_prompt, x_sample, cache_mla_ckv, cache_mla_krope, cache_gqa_k, cache_gqa_v, state_ret, c, c_ctx, mod_w, mod_b, ffn_w_gate, ffn_w_up, ffn_w_down, ev_w_in, mla_q_norm, mla_w_qb, mla_kv_norm, mla_w_kvb, mla_nope_norm, mla_rope_norm, gqa_qk_norm, ev_w_out, od_w_in, ret_decay_logit, ret_norm, hy_conv_w, hy_conv_b, hy_filt_w1, hy_filt_b1, hy_filt_w2, hy_filt_b2, hy_filt_freq, hy_filt_w3, hy_decay, hy_skip, od_w_out):
    xc = x_prompt.reshape(BATCH * SEQ, D_MODEL)
    xs = x_sample.reshape(DEC_BATCH * DEC_SEQ, D_MODEL)

    cvec = jnp.concatenate([c_ctx[None, :], c, jnp.zeros((8 - 1 - DEC_BATCH, D_MODEL), F32)], axis=0)
    mod_all = _mod_vectors(cvec, mod_w, mod_b)

    tables = tuple(jnp.asarray(t) for t in _rope_tables(DEC_SEQ))

    outs = {}
    for l in range(DEPTH):
        mod_tab = mod_all[l, :1 + DEC_BATCH].reshape((1 + DEC_BATCH) * N_MOD, 1, D_MODEL)
        xc = _ffn(xc, mod_tab, False, 0, ffn_w_gate, ffn_w_up, ffn_w_down, l, 0)
        xs = _ffn(xs, mod_tab, True, 0, ffn_w_gate, ffn_w_up, ffn_w_down, l, 0)
        if l % 2 == 0:
            e = l // 2
            w = _even_weights(e, ev_w_in, mla_q_norm[e], mla_w_qb[e], mla_kv_norm[e], mla_w_kvb[e],
                              mla_nope_norm[e], mla_rope_norm[e], gqa_qk_norm[e], ev_w_out[e])
            qm, ckv, kr, qg, kg, vg = _even_proj(xc, mod_tab, w)
            km, v = _kv_expand(ckv, kr, w["w_kvb"], w["nope_g"], TM_PROJ)
            att = _attention(qm, qg, km, v, kg, vg, SEQ, SEQ)
            xc = _out_proj(xc, mod_tab, False, [att], w["w_out"])
            outs["ckv"], outs["krope"], outs["k"], outs["v"] = ckv, _unspread_rope(kr), kg, vg
            qm, ckv, kr, qg, kg, vg = _even_proj(xs, mod_tab, w, tables)
            km, v = _kv_expand(ckv, kr, w["w_kvb"], w["nope_g"], TM_PROJ)
            c_ckv = cache_mla_ckv[:, e].reshape(DEC_BATCH * PAST_LEN, MLA_KV_RANK)
            c_kr = _spread_rope(cache_mla_krope[:, e].reshape(DEC_BATCH * PAST_LEN, MLA_ROPE))
            kmc, vc = _kv_expand(c_ckv, c_kr, w["w_kvb"], w["nope_g"], PAST_LEN)
            kgc = cache_gqa_k[:, e].reshape(DEC_BATCH * PAST_LEN, GQA_KV_HEADS * HEAD_DIM)
            vgc = cache_gqa_v[:, e].reshape(DEC_BATCH * PAST_LEN, GQA_KV_HEADS * HEAD_DIM)
            att = _attention(qm, qg, km, v, kg, vg, DEC_SEQ, TQ_ATT, cache=(kmc, vc, kgc, vgc))
            xs = _out_proj(xs, mod_tab, True, [att], w["w_out"])
        else:
            o = l // 2
            w_out = od_w_out[o]
            logit = ret_decay_logit[o]
            ret_g = ret_norm[o].reshape(1, -1)
            skip = hy_skip[o].reshape(1, -1)
            conv_b = hy_conv_b[o].reshape(1, -1)
            filt_args = (hy_filt_w1[o], hy_filt_b1[o], hy_filt_w2[o], hy_filt_b2[o], hy_filt_freq[o],
                         hy_filt_w3[o], hy_decay[o])
            for stream in ("ctx", "smp"):
                x, seq, latent = (xc, SEQ, False) if stream == "ctx" else (xs, DEC_SEQ, True)
                fwd, inv = _dft_matrices(seq)
                fwd_b, inv_b = jnp.asarray(fwd).astype(BF16), jnp.asarray(inv).astype(BF16)
                spectra = _filter_spectra(seq, *filt_args, fwd_b, TC_HY)
                u = _odd_proj(x, mod_tab, latent, od_w_in, o)
                if stream == "ctx":
                    o_ret, st = _retention(u, logit, ret_g, seq, RET_HEADS)
                    outs["ret"] = st
                    o_hy = _hyena(u, seq, HY_CH, HY_CH, hy_conv_w[o], conv_b, spectra, skip, fwd_b, inv_b)
                else:
                    (o_ret,) = _retention(u, logit, ret_g, seq, RET_HEADS_LAT, s0=state_ret[:, o])
                    o_hy = _hyena(u, seq, TC_HY, HY_SUB_LAT, hy_conv_w[o], conv_b, spectra, skip, fwd_b, inv_b)
                x = _out_proj(x, mod_tab, latent, [o_ret, o_hy], w_out)
                if stream == "ctx":
                    xc = x
                else:
                    xs = x
        xc = _ffn(xc, mod_tab, False, 2, ffn_w_gate, ffn_w_up, ffn_w_down, l, 1)
        xs = _ffn(xs, mod_tab, True, 2, ffn_w_gate, ffn_w_up, ffn_w_down, l, 1)

    return (
        xc.reshape(BATCH, SEQ, D_MODEL),
        xs.reshape(DEC_BATCH, DEC_SEQ, D_MODEL),
        outs["ckv"].reshape(BATCH, 1, SEQ, MLA_KV_RANK),
        outs["krope"].reshape(BATCH, 1, SEQ, MLA_ROPE),
        outs["k"].reshape(BATCH, 1, SEQ, GQA_KV_HEADS, HEAD_DIM),
        outs["v"].reshape(BATCH, 1, SEQ, GQA_KV_HEADS, HEAD_DIM),
        outs["ret"].reshape(BATCH, 1, 2, RET_HEADS, RET_DK, RET_DV),
    )
```

```python
import functools
import math

import numpy as np
import jax
import jax.numpy as jnp
from jax import lax
from jax.experimental import pallas as pl
from jax.experimental.pallas import tpu as pltpu

F32 = jnp.float32
BF16 = jnp.bfloat16

D_MODEL = 2048
BATCH = 32
SEQ = 256
DEPTH = 2
DEC_BATCH = 2
DEC_SEQ = 1024
PAST_LEN = 256
GRID_W = 64
N_MOD = 9
D_FF = 5632
ROPE_THETA = 10000.0
EPS = 1e-6

MLA_HEADS = 8
MLA_Q_RANK = 512
MLA_KV_RANK = 512
MLA_NOPE = 128
MLA_ROPE = 64
MLA_V = 128
GQA_HEADS = 8
GQA_KV_HEADS = 2
GQA_GROUP = GQA_HEADS // GQA_KV_HEADS
HEAD_DIM = 128
MLA_SCALE = (MLA_NOPE + MLA_ROPE) ** -0.5
GQA_SCALE = HEAD_DIM ** -0.5
MLA_QK_PAD = 256

RET_HEADS = 8
RET_DK = 128
RET_DV = 128
HY_CH = 1024
HY_ORDER = 2
HY_BANDS = 16
HY_FHID = 64
ODD_IN = 4 * RET_HEADS * RET_DK + (HY_ORDER + 1) * HY_CH

V7X_VMEM_BYTES = 64 * 1024 * 1024
LANES = 128
VMEM_LIMIT = V7X_VMEM_BYTES - 3 * 1024 * 1024

TM_FFN = 1024
TF_FFN = 512
TF_SUB = 256
TM_PROJ = 512
TM_ODD_CTX, TN_ODD_CTX = 2048, 512
TM_ODD_LAT, TN_ODD_LAT = 1024, 1024
TN_MOD = 1024
TQ_ATT = 256
TC_HY = 512
HY_SUB_LAT = 256
RET_HEADS_LAT = 2
SEQS_PER_STEP = 2


def _cparams(*sem):
    return pltpu.CompilerParams(dimension_semantics=sem, vmem_limit_bytes=VMEM_LIMIT)


def _const_spec(shape):
    nd = len(shape)
    return pl.BlockSpec(shape, lambda *_: (0,) * nd)


def _silu(x):
    return x * jax.nn.sigmoid(x)


def _rms_rows(x, width=None):
    ss = jnp.sum(x * x, axis=-1, keepdims=True)
    n = x.shape[-1] if width is None else width
    return x * lax.rsqrt(ss * (1.0 / n) + EPS)


def _mod_kernel(c_ref, w_ref, b_ref, o_ref):
    s = _silu(c_ref[...]).astype(BF16)
    o_ref[0] = jnp.dot(s, w_ref[0].astype(BF16), preferred_element_type=F32) + b_ref[0]


def _mod_vectors(cvec, mod_w, mod_b):
    depth, d, n = mod_w.shape
    rows = cvec.shape[0]
    return pl.pallas_call(
        _mod_kernel,
        out_shape=jax.ShapeDtypeStruct((depth, rows, n), F32),
        grid=(depth, n // TN_MOD),
        in_specs=[
            pl.BlockSpec((rows, d), lambda l, j: (0, 0)),
            pl.BlockSpec((1, d, TN_MOD), lambda l, j: (l, 0, j)),
            pl.BlockSpec((1, 1, TN_MOD), lambda l, j: (l, 0, j)),
        ],
        out_specs=pl.BlockSpec((1, rows, TN_MOD), lambda l, j: (l, 0, j)),
        compiler_params=_cparams("parallel", "parallel"),
        name="mod_vectors",
    )(cvec, mod_w, mod_b.reshape(depth, 1, n))


def _mod_specs(sub, latent, tile, extra_axes=0):
    group_of_tile = (lambda i: 1 + (i * tile) // DEC_SEQ) if latent else (lambda i: 0)

    def spec(k):
        if extra_axes:
            return pl.BlockSpec((1, 1, D_MODEL), lambda i, j: (group_of_tile(i) * N_MOD + 3 * sub + k, 0, 0))
        return pl.BlockSpec((1, 1, D_MODEL), lambda i: (group_of_tile(i) * N_MOD + 3 * sub + k, 0, 0))
    return [spec(0), spec(1), spec(2)]


def _modulated(x, sh_ref, sc_ref):
    return (_rms_rows(x) * (1.0 + sc_ref[0]) + sh_ref[0]).astype(BF16)


def _ffn_kernel(x_hbm, sh_ref, sc_ref, gt_ref, wg_ref, wu_ref, wd_ref, o_ref, x_buf, h_ref, x_sem):
    i, f = pl.program_id(0), pl.program_id(1)
    tm = x_buf.shape[0]
    half_gate = 0.5 * gt_ref[0]

    def x_copy(tile):
        return pltpu.make_async_copy(x_hbm.at[pl.ds(tile * tm, tm), :], x_buf, x_sem)

    def partial_out(h):
        out = None
        for c in range(TF_FFN // TF_SUB):
            cs = slice(c * TF_SUB, (c + 1) * TF_SUB)
            g = jnp.dot(h, wg_ref[:, cs].astype(BF16), preferred_element_type=F32)
            u = jnp.dot(h, wu_ref[:, cs].astype(BF16), preferred_element_type=F32)
            a = (_silu(g) * u).astype(BF16)
            t = jnp.dot(a, wd_ref[cs, :].astype(BF16), preferred_element_type=F32)
            out = t if out is None else out + t
        return half_gate * out

    @pl.when(f == 0)
    def _():
        @pl.when(i == 0)
        def _():
            x_copy(0).start()

        x_copy(i).wait()
        x = x_buf[...]
        h = _modulated(x, sh_ref, sc_ref)
        h_ref[...] = h
        o_ref[...] = x + partial_out(h)

    @pl.when(f > 0)
    def _():
        @pl.when(jnp.logical_and(f == pl.num_programs(1) // 2, i + 1 < pl.num_programs(0)))
        def _():
            x_copy(i + 1).start()

        o_ref[...] += partial_out(h_ref[...])


def _ffn(x, mod_tab, latent, sub, w_gate, w_up, w_down, layer, j):
    m, d = x.shape
    dff = w_gate.shape[-1]
    assert dff // TF_FFN >= 2
    return pl.pallas_call(
        _ffn_kernel,
        out_shape=jax.ShapeDtypeStruct((m, d), F32),
        grid=(m // TM_FFN, dff // TF_FFN),
        in_specs=[pl.BlockSpec(memory_space=pl.ANY)]
        + _mod_specs(sub, latent, TM_FFN, extra_axes=1)
        + [
            pl.BlockSpec((None, None, d, TF_FFN), lambda i, f: (layer, j, 0, f)),
            pl.BlockSpec((None, None, d, TF_FFN), lambda i, f: (layer, j, 0, f)),
            pl.BlockSpec((None, None, TF_FFN, d), lambda i, f: (layer, j, f, 0)),
        ],
        out_specs=pl.BlockSpec((TM_FFN, d), lambda i, f: (i, 0)),
        scratch_shapes=[pltpu.VMEM((TM_FFN, d), F32), pltpu.VMEM((TM_FFN, d), BF16), pltpu.SemaphoreType.DMA(())],
        compiler_params=_cparams("arbitrary", "arbitrary"),
        name="ffn",
    )(x, mod_tab, mod_tab, mod_tab, w_gate, w_up, w_down)


def _out_proj_kernel(*refs, n_parts):
    x_ref, gt_ref, w_ref = refs[0], refs[1], refs[2]
    part_refs = refs[3:3 + n_parts]
    o_ref, wb_ref = refs[3 + n_parts], refs[4 + n_parts]

    @pl.when(pl.program_id(0) == 0)
    def _():
        wb_ref[...] = w_ref[...].astype(BF16)

    acc, off = None, 0
    for p_ref in part_refs:
        width = p_ref.shape[1]
        t = jnp.dot(p_ref[...], wb_ref[off:off + width, :], preferred_element_type=F32)
        acc = t if acc is None else acc + t
        off += width
    o_ref[...] = x_ref[...] + gt_ref[0] * acc


def _out_proj(x, mod_tab, latent, parts, w_out):
    m, d = x.shape
    return pl.pallas_call(
        functools.partial(_out_proj_kernel, n_parts=len(parts)),
        out_shape=jax.ShapeDtypeStruct((m, d), F32),
        grid=(m // TM_PROJ,),
        in_specs=[pl.BlockSpec((TM_PROJ, d), lambda i: (i, 0)), _mod_specs(1, latent, TM_PROJ)[2],
                  pl.BlockSpec(w_out.shape, lambda i: (0, 0), pipeline_mode=pl.Buffered(1))]
        + [pl.BlockSpec((TM_PROJ, p.shape[1]), lambda i: (i, 0)) for p in parts],
        out_specs=pl.BlockSpec((TM_PROJ, d), lambda i: (i, 0)),
        scratch_shapes=[pltpu.VMEM(w_out.shape, BF16)],
        compiler_params=_cparams("arbitrary"),
        name="mixer_out_proj",
    )(x, mod_tab, w_out, *parts)


EV_QA, EV_KVA, EV_QG, EV_KG, EV_VG, EV_KR, EV_END = 0, 512, 1024, 2048, 2304, 2560, 2688


def _rotate_half(x, cos, sin):
    return x * cos + pltpu.roll(x, LANES // 2, axis=1) * sin


def _even_proj_kernel(*refs, rotary):
    (x_ref, sh_ref, sc_ref, w_in_ref, w_qb_ref, qn_ref, kvn_ref, ng_ref, rg_ref, gg_ref) = refs[:10]
    if rotary:
        cm_ref, sm_ref, cg_ref, sg_ref = refs[10:14]
        rot_m = lambda t: _rotate_half(t, cm_ref[...], sm_ref[...])
        rot_g = lambda t: _rotate_half(t, cg_ref[...], sg_ref[...])
    else:
        rot_m = rot_g = lambda t: t
    qm_ref, ckv_ref, kr_ref, qg_ref, kg_ref, vg_ref, wb_ref = refs[-7:]

    @pl.when(pl.program_id(0) == 0)
    def _():
        chunk = 128
        half = MLA_ROPE // 2

        def copy_rows(dst0, src0, n_chunks):
            def body(c, carry):
                dst = pl.ds(pl.multiple_of(dst0 + c * chunk, 16), chunk)
                src = pl.ds(pl.multiple_of(src0 + c * chunk, 8), chunk)
                wb_ref[dst, :] = w_in_ref[src, :].astype(BF16)
                return carry
            lax.fori_loop(0, n_chunks, body, 0)

        copy_rows(EV_QA, 0, EV_QG // chunk)
        copy_rows(EV_QG, EV_QG + MLA_ROPE, (EV_KR - EV_QG) // chunk)
        wb_ref[EV_KR:EV_END, :] = jnp.zeros((EV_END - EV_KR, wb_ref.shape[1]), BF16)
        wb_ref[EV_KR:EV_KR + half, :] = w_in_ref[EV_QG:EV_QG + half, :].astype(BF16)
        wb_ref[EV_KR + LANES // 2:EV_KR + LANES // 2 + half, :] = w_in_ref[EV_QG + half:EV_QG + MLA_ROPE, :].astype(BF16)

    h = _modulated(x_ref[...], sh_ref, sc_ref)
    u = _qk(h, wb_ref[...])

    nope_gain = ng_ref[0:1, :] * MLA_SCALE
    rope_gain = rg_ref[0:1, :] * MLA_SCALE
    gqa_gain = gg_ref[0:1, :] * GQA_SCALE

    qa = (_rms_rows(u[:, EV_QA:EV_KVA]) * qn_ref[...]).astype(BF16)
    q = jnp.dot(qa, w_qb_ref[...], preferred_element_type=F32)
    for hd in range(MLA_HEADS):
        c0 = hd * MLA_QK_PAD
        nope = _rms_rows(q[:, c0:c0 + MLA_NOPE]) * nope_gain
        rope = _rms_rows(q[:, c0 + MLA_NOPE:c0 + MLA_QK_PAD], MLA_ROPE) * rope_gain
        qm_ref[:, c0:c0 + MLA_NOPE] = nope.astype(BF16)
        qm_ref[:, c0 + MLA_NOPE:c0 + MLA_QK_PAD] = rot_m(rope).astype(BF16)

    ckv_ref[...] = _rms_rows(u[:, EV_KVA:EV_QG]) * kvn_ref[...]
    kr_ref[...] = rot_m(_rms_rows(u[:, EV_KR:EV_END], MLA_ROPE) * rg_ref[1:2, :])

    for hd in range(GQA_HEADS):
        c0 = EV_QG + hd * HEAD_DIM
        qh = _rms_rows(u[:, c0:c0 + HEAD_DIM]) * gqa_gain
        qg_ref[:, hd * HEAD_DIM:(hd + 1) * HEAD_DIM] = rot_g(qh).astype(BF16)
    for hd in range(GQA_KV_HEADS):
        c0 = EV_KG + hd * HEAD_DIM
        kh = _rms_rows(u[:, c0:c0 + HEAD_DIM]) * gg_ref[1:2, :]
        kg_ref[:, hd * HEAD_DIM:(hd + 1) * HEAD_DIM] = rot_g(kh)
    vg_ref[...] = u[:, EV_VG:EV_KR]


def _even_proj(x, mod_tab, w, tables=None):
    m, d = x.shape
    tm = TM_PROJ
    row = lambda w_: pl.BlockSpec((tm, w_), lambda i: (i, 0))
    tab_specs, tab_args = [], []
    if tables is not None:
        tiles_per_seq = tables[0].shape[0] // tm
        tab_specs = [pl.BlockSpec((tm, LANES), lambda i: (i % tiles_per_seq, 0))] * 4
        tab_args = list(tables)
    return pl.pallas_call(
        functools.partial(_even_proj_kernel, rotary=tables is not None),
        out_shape=[
            jax.ShapeDtypeStruct((m, MLA_HEADS * MLA_QK_PAD), BF16),
            jax.ShapeDtypeStruct((m, MLA_KV_RANK), F32),
            jax.ShapeDtypeStruct((m, LANES), F32),
            jax.ShapeDtypeStruct((m, GQA_HEADS * HEAD_DIM), BF16),
            jax.ShapeDtypeStruct((m, GQA_KV_HEADS * HEAD_DIM), F32),
            jax.ShapeDtypeStruct((m, GQA_KV_HEADS * HEAD_DIM), F32),
        ],
        grid=(m // tm,),
        in_specs=[row(d)] + _mod_specs(1, tables is not None, tm)[:2]
        + [pl.BlockSpec((None,) + w["w_in"].shape[1:], lambda i: (w["layer"], 0, 0)),
           _const_spec(w["w_qb"].shape), _const_spec(w["q_norm"].shape),
           _const_spec(w["kv_norm"].shape), _const_spec(w["nope_g"].shape), _const_spec(w["rope_g"].shape),
           _const_spec(w["gqa_g"].shape)] + tab_specs,
        out_specs=[row(MLA_HEADS * MLA_QK_PAD), row(MLA_KV_RANK), row(LANES), row(GQA_HEADS * HEAD_DIM),
                   row(GQA_KV_HEADS * HEAD_DIM), row(GQA_KV_HEADS * HEAD_DIM)],
        scratch_shapes=[pltpu.VMEM((EV_END, d), BF16)],
        compiler_params=_cparams("arbitrary"),
        name="even_proj",
    )(x, mod_tab, mod_tab, w["w_in"], w["w_qb"], w["q_norm"], w["kv_norm"], w["nope_g"], w["rope_g"], w["gqa_g"],
      *tab_args)


def _kv_expand_kernel(ckv_ref, kr_ref, w_ref, ng_ref, km_ref, v_ref):
    kv = jnp.dot(ckv_ref[...].astype(BF16), w_ref[...], preferred_element_type=F32)
    kr = kr_ref[...].astype(BF16)
    for hd in range(MLA_HEADS):
        c0 = hd * (MLA_NOPE + MLA_V)
        km_ref[:, hd * MLA_QK_PAD:hd * MLA_QK_PAD + MLA_NOPE] = (
            _rms_rows(kv[:, c0:c0 + MLA_NOPE]) * ng_ref[1:2, :]).astype(BF16)
        km_ref[:, hd * MLA_QK_PAD + MLA_NOPE:(hd + 1) * MLA_QK_PAD] = kr
        v_ref[:, hd * MLA_V:(hd + 1) * MLA_V] = kv[:, c0 + MLA_NOPE:c0 + MLA_NOPE + MLA_V].astype(BF16)


def _kv_expand(ckv, kr, w_kvb_bf16, nope_g, tr):
    r = ckv.shape[0]
    row = lambda w_: pl.BlockSpec((tr, w_), lambda i: (i, 0))
    return pl.pallas_call(
        _kv_expand_kernel,
        out_shape=[jax.ShapeDtypeStruct((r, MLA_HEADS * MLA_QK_PAD), BF16),
                   jax.ShapeDtypeStruct((r, MLA_HEADS * MLA_V), BF16)],
        grid=(r // tr,),
        in_specs=[row(MLA_KV_RANK), row(LANES), _const_spec(w_kvb_bf16.shape), _const_spec(nope_g.shape)],
        out_specs=[row(MLA_HEADS * MLA_QK_PAD), row(MLA_HEADS * MLA_V)],
        compiler_params=_cparams("parallel"),
        name="kv_expand",
    )(ckv, kr, w_kvb_bf16, nope_g)


def _softmax_pv(score_list, value_list):
    mx = None
    for s in score_list:
        m_ = jnp.max(s, axis=-1, keepdims=True)
        mx = m_ if mx is None else jnp.maximum(mx, m_)
    den, acc = None, None
    for s, v in zip(score_list, value_list):
        p = jnp.exp(s - mx)
        l_ = jnp.sum(p, axis=-1, keepdims=True)
        o_ = jnp.dot(p.astype(BF16), v, preferred_element_type=F32)
        den = l_ if den is None else den + l_
        acc = o_ if acc is None else acc + o_
    return acc / den


def _qk(q, k):
    return lax.dot_general(q, k, (((1,), (1,)), ((), ())), preferred_element_type=F32)


def _attn_kernel(*refs, has_cache, seqs):
    if has_cache:
        qm_ref, qg_ref, km_ref, v_ref, kg_ref, vg_ref, kmc_ref, vc_ref, kgc_ref, vgc_ref, o_ref = refs
    else:
        qm_ref, qg_ref, km_ref, v_ref, kg_ref, vg_ref, o_ref = refs
    rows_q, rows_k = qm_ref.shape[0] // seqs, km_ref.shape[0] // seqs
    base = MLA_HEADS * MLA_V
    for sq in range(seqs):
        rq = slice(sq * rows_q, (sq + 1) * rows_q)
        rk = slice(sq * rows_k, (sq + 1) * rows_k)
        for hd in range(MLA_HEADS):
            qs = slice(hd * MLA_QK_PAD, (hd + 1) * MLA_QK_PAD)
            vs = slice(hd * MLA_V, (hd + 1) * MLA_V)
            q = qm_ref[rq, qs]
            scores = [_qk(q, km_ref[rk, qs])]
            values = [v_ref[rk, vs]]
            if has_cache:
                scores.append(_qk(q, kmc_ref[:, qs]))
                values.append(vc_ref[:, vs])
            o_ref[rq, vs] = _softmax_pv(scores, values).astype(o_ref.dtype)
        for kvh in range(GQA_KV_HEADS):
            ks = slice(kvh * HEAD_DIM, (kvh + 1) * HEAD_DIM)
            k = kg_ref[rk, ks].astype(BF16)
            v = vg_ref[rk, ks].astype(BF16)
            if has_cache:
                kc = kgc_ref[:, ks].astype(BF16)
                vc = vgc_ref[:, ks].astype(BF16)
            for g in range(GQA_GROUP):
                hd = kvh * GQA_GROUP + g
                hs = slice(hd * HEAD_DIM, (hd + 1) * HEAD_DIM)
                q = qg_ref[rq, hs]
                scores, values = [_qk(q, k)], [v]
                if has_cache:
                    scores.append(_qk(q, kc))
                    values.append(vc)
                o_ref[rq, base + hd * HEAD_DIM:base + (hd + 1) * HEAD_DIM] = (
                    _softmax_pv(scores, values).astype(o_ref.dtype))


def _attention(qm, qg, km, v, kg, vg, seq, tq, cache=None, seqs=1):
    m = qm.shape[0]
    nq = seq // tq
    assert seqs == 1 or (cache is None and nq == 1)
    qrow = lambda w_: pl.BlockSpec((seqs * tq, w_), lambda i: (i, 0))
    krow = lambda w_: pl.BlockSpec((seqs * seq, w_), lambda i: (i // nq, 0))
    in_specs = [qrow(qm.shape[1]), qrow(qg.shape[1]), krow(km.shape[1]), krow(v.shape[1]),
                krow(kg.shape[1]), krow(vg.shape[1])]
    args = [qm, qg, km, v, kg, vg]
    if cache is not None:
        past = cache[0].shape[0] // (m // seq)
        crow = lambda w_: pl.BlockSpec((past, w_), lambda i: (i // nq, 0))
        in_specs += [crow(c.shape[1]) for c in cache]
        args += list(cache)
    width = MLA_HEADS * MLA_V + GQA_HEADS * HEAD_DIM
    return pl.pallas_call(
        functools.partial(_attn_kernel, has_cache=cache is not None, seqs=seqs),
        out_shape=jax.ShapeDtypeStruct((m, width), BF16),
        grid=(m // (seqs * tq),),
        in_specs=in_specs,
        out_specs=qrow(width),
        compiler_params=_cparams("parallel"),
        name="attention",
    )(*args)


def _odd_proj_kernel(x_hbm, sh_ref, sc_ref, w_ref, o_ref, x_buf, h_ref, x_sem):
    i, j = pl.program_id(0), pl.program_id(1)
    tm = x_buf.shape[0]

    def x_copy(tile):
        return pltpu.make_async_copy(x_hbm.at[pl.ds(tile * tm, tm), :], x_buf, x_sem)

    def project(h):
        return jnp.dot(h, w_ref[...].astype(BF16), preferred_element_type=F32).astype(o_ref.dtype)

    @pl.when(j == 0)
    def _():
        @pl.when(i == 0)
        def _():
            x_copy(0).start()

        x_copy(i).wait()
        h = _modulated(x_buf[...], sh_ref, sc_ref)
        h_ref[...] = h
        o_ref[...] = project(h)

    @pl.when(j > 0)
    def _():
        @pl.when(jnp.logical_and(j == pl.num_programs(1) // 2, i + 1 < pl.num_programs(0)))
        def _():
            x_copy(i + 1).start()

        o_ref[...] = project(h_ref[...])


def _odd_proj(x, mod_tab, latent, w_in, layer):
    m, d = x.shape
    n = w_in.shape[-1]
    tm, tn = (TM_ODD_LAT, TN_ODD_LAT) if latent else (TM_ODD_CTX, TN_ODD_CTX)
    assert n // tn >= 2
    return pl.pallas_call(
        _odd_proj_kernel,
        out_shape=jax.ShapeDtypeStruct((m, n), BF16),
        grid=(m // tm, n // tn),
        in_specs=[pl.BlockSpec(memory_space=pl.ANY)]
        + _mod_specs(1, latent, tm, extra_axes=1)[:2]
        + [pl.BlockSpec((None, d, tn), lambda i, j: (layer, 0, j))],
        out_specs=pl.BlockSpec((tm, tn), lambda i, j: (i, j)),
        scratch_shapes=[pltpu.VMEM((tm, d), F32), pltpu.VMEM((tm, d), BF16), pltpu.SemaphoreType.DMA(())],
        compiler_params=_cparams("arbitrary", "arbitrary"),
        name="odd_proj",
    )(x, mod_tab, mod_tab, w_in)


def _log_sigmoid(x):
    return jnp.minimum(x, 0.0) - jnp.log1p(jnp.exp(-jnp.abs(x)))


def _retention_kernel(*refs, heads, has_init, seqs):
    if has_init:
        logit_ref, q_ref, k_ref, v_ref, g_ref, rg_ref, s0_ref, o_ref, dec_ref, vec_ref = refs
    else:
        logit_ref, q_ref, k_ref, v_ref, g_ref, rg_ref, o_ref, st_ref, dec_ref, vec_ref = refs
    length = q_ref.shape[0] // seqs
    h0 = pl.program_id(0) * heads
    scale = RET_DK ** -0.5
    log_g = _log_sigmoid(logit_ref[...])
    lane_h = lax.broadcasted_iota(jnp.int32, (2, RET_HEADS), 1)

    def head_log_decays(hh):
        lg = jnp.sum(jnp.where(lane_h == h0 + hh, log_g, 0.0), axis=1, keepdims=True)
        return lg[0:1, :], lg[1:2, :]

    @pl.when(pl.program_id(1) == 0)
    def _():
        n_i = lax.broadcasted_iota(jnp.int32, (length, length), 0)
        m_i = lax.broadcasted_iota(jnp.int32, (length, length), 1)
        diff = (n_i - m_i).astype(F32)
        pos = lax.broadcasted_iota(jnp.int32, (length, RET_DK), 0).astype(F32)
        for hh in range(heads):
            lg_f, lg_b = head_log_decays(hh)
            dec_ref[hh] = scale * (jnp.where(diff >= 0, jnp.exp(jnp.maximum(diff, 0.0) * lg_f), 0.0)
                                   + jnp.where(diff <= 0, jnp.exp(jnp.maximum(-diff, 0.0) * lg_b), 0.0))
            if has_init:
                vec_ref[4 * hh + 2] = jnp.exp((pos + 1.0) * lg_f)
                vec_ref[4 * hh + 3] = jnp.exp((length - pos) * lg_b)
            else:
                vec_ref[4 * hh + 0] = jnp.exp((length - 1.0 - pos) * lg_f) * scale
                vec_ref[4 * hh + 1] = jnp.exp(pos * lg_b) * scale

    tn = (((0,), (0,)), ((), ()))
    for sq, hh in [(a, b) for a in range(seqs) for b in range(heads)]:
        cs = slice(hh * RET_DK, (hh + 1) * RET_DK)
        rs = slice(sq * length, (sq + 1) * length)
        q, k, v = q_ref[rs, cs], k_ref[rs, cs], v_ref[rs, cs]
        att = _qk(q, k) * dec_ref[hh]
        o = jnp.dot(att.astype(BF16), v, preferred_element_type=F32)
        if has_init:
            s0_f, s0_b = s0_ref[sq, 0, hh].astype(BF16), s0_ref[sq, 1, hh].astype(BF16)
            o = o + jnp.dot(q, s0_f, preferred_element_type=F32) * vec_ref[4 * hh + 2]
            o = o + jnp.dot(q, s0_b, preferred_element_type=F32) * vec_ref[4 * hh + 3]
        else:
            kf = k.astype(F32)
            k_f, k_b = (kf * vec_ref[4 * hh + 0]).astype(BF16), (kf * vec_ref[4 * hh + 1]).astype(BF16)
            st_ref[sq, 0, hh] = lax.dot_general(k_f, v, tn, preferred_element_type=F32)
            st_ref[sq, 1, hh] = lax.dot_general(k_b, v, tn, preferred_element_type=F32)
        gate = g_ref[rs, cs].astype(F32)
        o_ref[rs, cs] = (_rms_rows(o) * rg_ref[:, cs] * _silu(gate)).astype(o_ref.dtype)


def _retention(u, decay_logit, ret_g, seq, heads, s0=None, seqs=1):
    m = u.shape[0]
    b = m // seq
    hw = heads * RET_DK
    nh = RET_HEADS // heads
    blocks_per_part = RET_HEADS * RET_DK // hw
    col = lambda part: pl.BlockSpec((seqs * seq, hw), lambda j, i, part=part: (i, part * blocks_per_part + j))
    st_spec = pl.BlockSpec((seqs, 2, heads, RET_DK, RET_DV), lambda j, i: (i, 0, j, 0, 0))
    in_specs = [_const_spec(decay_logit.shape), col(0), col(1), col(2), col(3),
                pl.BlockSpec((1, hw), lambda j, i: (0, j))]
    args = [decay_logit, u, u, u, u, ret_g]
    if s0 is not None:
        in_specs.append(st_spec)
        args.append(s0)
    out_shape = [jax.ShapeDtypeStruct((m, RET_HEADS * RET_DV), BF16)]
    out_specs = [pl.BlockSpec((seqs * seq, hw), lambda j, i: (i, j))]
    if s0 is None:
        out_shape.append(jax.ShapeDtypeStruct((b, 2, RET_HEADS, RET_DK, RET_DV), F32))
        out_specs.append(st_spec)
    return pl.pallas_call(
        functools.partial(_retention_kernel, heads=heads, has_init=s0 is not None, seqs=seqs),
        out_shape=out_shape,
        grid=(nh, b // seqs),
        in_specs=in_specs,
        out_specs=out_specs,
        scratch_shapes=[pltpu.VMEM((heads, seq, seq), F32), pltpu.VMEM((4 * heads, seq, RET_DK), F32)],
        compiler_params=_cparams("parallel", "arbitrary"),
        name="retention",
    )(*args)


@functools.lru_cache(maxsize=None)
def _dft_matrices(length):
    n = 2 * length
    f = np.arange(length, dtype=np.float64)[:, None]
    t = np.arange(length, dtype=np.float64)[None, :]
    ang = 2.0 * np.pi * f * t / n
    fwd_a = np.cos(ang)
    fwd_b = np.sin(ang)
    fwd_b[0, :] = np.cos(np.pi * t[0])
    fwd = np.concatenate([fwd_a, fwd_b], axis=0)
    tt = (np.arange(length, dtype=np.float64) + length // 2)[:, None]
    ff = np.arange(length, dtype=np.float64)[None, :]
    ang_i = 2.0 * np.pi * ff * tt / n
    inv_a = 2.0 * np.cos(ang_i) / n
    inv_b = 2.0 * np.sin(ang_i) / n
    inv_a[:, 0] = 1.0 / n
    inv_b[:, 0] = np.cos(np.pi * tt[:, 0]) / n
    inv = np.concatenate([inv_a, inv_b], axis=1)
    return fwd.astype(np.float32), inv.astype(np.float32)


@functools.lru_cache(maxsize=None)
def _filter_features(length):
    t = np.arange(length, dtype=np.float64)
    tn = t / length
    bands = np.arange(1, HY_BANDS + 1, dtype=np.float64)
    ang = 2.0 * np.pi * tn[:, None] * bands[None, :]
    feat = np.concatenate([tn[:, None], np.sin(ang), np.cos(ang)], axis=-1)
    feat = np.pad(feat, ((0, 0), (0, LANES - feat.shape[1])))
    r = (np.abs(t - length // 2) / (length / 2))[:, None]
    return feat.astype(np.float32), r.astype(np.float32)


def _filter_spec_kernel(feat_ref, r_ref, w1_ref, b1_ref, w2_ref, b2_ref, fr_ref, w3_ref, dec_ref, fwd_ref, o_ref,
                        z_ref):
    hp = lax.Precision.HIGHEST

    @pl.when(pl.program_id(0) == 0)
    def _():
        z = jnp.sin(fr_ref[0:1, :] * (jnp.dot(feat_ref[...], w1_ref[...], precision=hp, preferred_element_type=F32)
                                      + b1_ref[...]))
        z_ref[...] = jnp.sin(fr_ref[1:2, :] * (jnp.dot(z, w2_ref[...], precision=hp, preferred_element_type=F32)
                                               + b2_ref[...]))

    filt = jnp.dot(z_ref[...], w3_ref[...], precision=hp, preferred_element_type=F32)
    filt = filt * jnp.exp(-r_ref[...] * jnp.abs(dec_ref[...]))
    spec = jnp.dot(fwd_ref[...], filt.astype(BF16), preferred_element_type=F32)
    length = spec.shape[0] // 2
    sa, sb = spec[:length], spec[length:]
    first = lax.broadcasted_iota(jnp.int32, sa.shape, 0) == 0
    o_ref[0:length, :] = sa
    o_ref[length:2 * length, :] = jnp.where(first, 0.0, sb)
    o_ref[2 * length:, :] = jnp.where(first, sb, sa)


def _filter_spectra(length, w1, b1, w2, b2, freq, w3, decay, fwd_bf16, tc):
    feat, r = _filter_features(length)
    nch = w3.shape[1]
    w1p = jnp.pad(w1, ((0, LANES - w1.shape[0]), (0, 0)))
    return pl.pallas_call(
        _filter_spec_kernel,
        out_shape=jax.ShapeDtypeStruct((3 * length, nch), F32),
        grid=(nch // tc,),
        in_specs=[_const_spec(feat.shape), _const_spec(r.shape), _const_spec(w1p.shape), _const_spec((1, HY_FHID)),
                  _const_spec(w2.shape), _const_spec((1, HY_FHID)), _const_spec(freq.shape),
                  pl.BlockSpec((HY_FHID, tc), lambda j: (0, j)), pl.BlockSpec((1, tc), lambda j: (0, j)),
                  _const_spec(fwd_bf16.shape)],
        out_specs=pl.BlockSpec((3 * length, tc), lambda j: (0, j)),
        scratch_shapes=[pltpu.VMEM((length, HY_FHID), F32)],
        compiler_params=_cparams("arbitrary"),
        name="hyena_filter_spectra",
    )(jnp.asarray(feat), jnp.asarray(r), w1p, b1.reshape(1, -1), w2, b2.reshape(1, -1), freq, w3,
      decay.reshape(1, -1), fwd_bf16)


def _hyena_kernel(v_ref, x1_ref, x2_ref, cwv_ref, cw1_ref, cw2_ref, cbv_ref, cb1_ref, cb2_ref,
                  h1_ref, h2_ref, sk1_ref, sk2_ref, fwd_ref, inv_ref, o_ref, *, sub):
    length, tc = v_ref.shape
    row = lax.broadcasted_iota(jnp.int32, (length, sub), 0)
    first, last = row == 0, row == length - 1

    for c in range(tc // sub):
        cs = slice(c * sub, (c + 1) * sub)

        def short_conv(x_ref, w_ref, b_ref):
            x = x_ref[:, cs].astype(F32)
            prev = jnp.where(first, 0.0, pltpu.roll(x, 1, axis=0))
            nxt = jnp.where(last, 0.0, pltpu.roll(x, length - 1, axis=0))
            return prev * w_ref[0:1, cs] + x * w_ref[1:2, cs] + nxt * w_ref[2:3, cs] + b_ref[:, cs]

        def long_conv(z, h_ref):
            zs = jnp.dot(fwd_ref[...], z.astype(BF16), preferred_element_type=F32)
            za, zb = zs[:length], zs[length:]
            ha, hb, hc = h_ref[0:length, cs], h_ref[length:2 * length, cs], h_ref[2 * length:, cs]
            y = jnp.concatenate([za * ha - zb * hb, za * hb + zb * hc], axis=0).astype(BF16)
            return jnp.dot(inv_ref[...], y, preferred_element_type=F32)

        v = short_conv(v_ref, cwv_ref, cbv_ref)
        x1 = short_conv(x1_ref, cw1_ref, cb1_ref)
        x2 = short_conv(x2_ref, cw2_ref, cb2_ref)
        z = x1 * (long_conv(v, h1_ref) + sk1_ref[:, cs] * v)
        o_ref[:, cs] = (x2 * (long_conv(z, h2_ref) + sk2_ref[:, cs] * z)).astype(o_ref.dtype)


def _hyena(u, seq, tc, sub, conv_w, conv_b, spectra, skip, fwd_bf16, inv_bf16):
    m = u.shape[0]
    b = m // seq
    nc = HY_CH // tc
    hy0 = 4 * RET_HEADS * RET_DK // tc
    ucol = lambda part: pl.BlockSpec((seq, tc), lambda i, j, part=part: (i, hy0 + part * nc + j))
    wcol = lambda rows, part: pl.BlockSpec((rows, tc), lambda i, j, part=part: (0, part * nc + j))
    return pl.pallas_call(
        functools.partial(_hyena_kernel, sub=sub),
        out_shape=jax.ShapeDtypeStruct((m, HY_CH), BF16),
        grid=(b, nc),
        in_specs=[ucol(0), ucol(1), ucol(2), wcol(3, 0), wcol(3, 1), wcol(3, 2), wcol(1, 0), wcol(1, 1), wcol(1, 2),
                  wcol(3 * seq, 0), wcol(3 * seq, 1), wcol(1, 0), wcol(1, 1),
                  _const_spec(fwd_bf16.shape), _const_spec(inv_bf16.shape)],
        out_specs=pl.BlockSpec((seq, tc), lambda i, j: (i, j)),
        compiler_params=_cparams("parallel", "parallel"),
        name="hyena",
    )(u, u, u, conv_w, conv_w, conv_w, conv_b, conv_b, conv_b, spectra, spectra, skip, skip, fwd_bf16, inv_bf16)


@functools.lru_cache(maxsize=None)
def _rope_tables(length):
    rows = np.repeat(np.arange(length // GRID_W, dtype=np.float64), GRID_W)
    cols = np.tile(np.arange(GRID_W, dtype=np.float64), length // GRID_W)

    def pack(dim):
        half = dim // 2
        freq = ROPE_THETA ** (-np.arange(0, half, 2, dtype=np.float64) / half)
        ang = np.concatenate([rows[:, None] * freq[None], cols[:, None] * freq[None]], axis=-1)
        pad = ((0, 0), (0, LANES // 2 - half))
        cos = np.pad(np.cos(ang), pad, constant_values=1.0)
        sin = np.pad(np.sin(ang), pad)
        return (np.concatenate([cos, cos], axis=-1).astype(np.float32),
                np.concatenate([-sin, sin], axis=-1).astype(np.float32))

    cos_m, sin_m = pack(MLA_ROPE)
    cos_g, sin_g = pack(HEAD_DIM)
    return cos_m, sin_m, cos_g, sin_g


def _spread_rope(a):
    half = MLA_ROPE // 2
    z = jnp.zeros(a.shape[:-1] + (LANES // 2 - half,), a.dtype)
    return jnp.concatenate([a[..., :half], z, a[..., half:], z], axis=-1)


def _unspread_rope(a):
    half = MLA_ROPE // 2
    return jnp.concatenate([a[..., :half], a[..., LANES // 2:LANES // 2 + half]], axis=-1)


def _even_weights(layer, w_in, q_norm, w_qb, kv_norm, w_kvb, nope_g, rope_g, gqa_g, w_out):
    wq = w_qb.reshape(MLA_Q_RANK, MLA_HEADS, MLA_NOPE + MLA_ROPE)
    wq = jnp.concatenate([wq[..., :MLA_NOPE], _spread_rope(wq[..., MLA_NOPE:])], axis=-1)
    return {
        "layer": layer,
        "w_in": jnp.swapaxes(w_in, 1, 2),
        "w_qb": wq.reshape(MLA_Q_RANK, MLA_HEADS * MLA_QK_PAD).astype(BF16),
        "w_kvb": w_kvb.astype(BF16),
        "w_out": w_out,
        "q_norm": q_norm.reshape(1, -1),
        "kv_norm": kv_norm.reshape(1, -1),
        "nope_g": nope_g,
        "rope_g": _spread_rope(rope_g),
        "gqa_g": gqa_g,
    }


def kernel(x_prompt, x_sample, cache_mla_ckv, cache_mla_krope, cache_gqa_k, cache_gqa_v, state_ret, c, c_ctx, mod_w, mod_b, ffn_w_gate, ffn_w_up, ffn_w_down, ev_w_in, mla_q_norm, mla_w_qb, mla_kv_norm, mla_w_kvb, mla_nope_norm, mla_rope_norm, gqa_qk_norm, ev_w_out, od_w_in, ret_decay_logit, ret_norm, hy_conv_w, hy_conv_b, hy_filt_w1, hy_filt_b1, hy_filt_w2, hy_filt_b2, hy_filt_freq, hy_filt_w3, hy_decay, hy_skip, od_w_out):
    xc = x_prompt.reshape(BATCH * SEQ, D_MODEL)
    xs = x_sample.reshape(DEC_BATCH * DEC_SEQ, D_MODEL)

    cvec = jnp.concatenate([c_ctx[None, :], c, jnp.zeros((8 - 1 - DEC_BATCH, D_MODEL), F32)], axis=0)
    mod_all = _mod_vectors(cvec, mod_w, mod_b)

    tables = tuple(jnp.asarray(t) for t in _rope_tables(DEC_SEQ))

    outs = {}
    for l in range(DEPTH):
        mod_tab = mod_all[l, :1 + DEC_BATCH].reshape((1 + DEC_BATCH) * N_MOD, 1, D_MODEL)
        xc = _ffn(xc, mod_tab, False, 0, ffn_w_gate, ffn_w_up, ffn_w_down, l, 0)
        xs = _ffn(xs, mod_tab, True, 0, ffn_w_gate, ffn_w_up, ffn_w_down, l, 0)
        if l % 2 == 0:
            e = l // 2
            w = _even_weights(e, ev_w_in, mla_q_norm[e], mla_w_qb[e], mla_kv_norm[e], mla_w_kvb[e],
                              mla_nope_norm[e], mla_rope_norm[e], gqa_qk_norm[e], ev_w_out[e])
            qm, ckv, kr, qg, kg, vg = _even_proj(xc, mod_tab, w)
            km, v = _kv_expand(ckv, kr, w["w_kvb"], w["nope_g"], TM_PROJ)
            att = _attention(qm, qg, km, v, kg, vg, SEQ, SEQ, seqs=SEQS_PER_STEP)
            xc = _out_proj(xc, mod_tab, False, [att], w["w_out"])
            outs["ckv"], outs["krope"], outs["k"], outs["v"] = ckv, _unspread_rope(kr), kg, vg
            qm, ckv, kr, qg, kg, vg = _even_proj(xs, mod_tab, w, tables)
            km, v = _kv_expand(ckv, kr, w["w_kvb"], w["nope_g"], TM_PROJ)
            c_ckv = cache_mla_ckv[:, e].reshape(DEC_BATCH * PAST_LEN, MLA_KV_RANK)
            c_kr = _spread_rope(cache_mla_krope[:, e].reshape(DEC_BATCH * PAST_LEN, MLA_ROPE))
            kmc, vc = _kv_expand(c_ckv, c_kr, w["w_kvb"], w["nope_g"], PAST_LEN)
            kgc = cache_gqa_k[:, e].reshape(DEC_BATCH * PAST_LEN, GQA_KV_HEADS * HEAD_DIM)
            vgc = cache_gqa_v[:, e].reshape(DEC_BATCH * PAST_LEN, GQA_KV_HEADS * HEAD_DIM)
            att = _attention(qm, qg, km, v, kg, vg, DEC_SEQ, TQ_ATT, cache=(kmc, vc, kgc, vgc))
            xs = _out_proj(xs, mod_tab, True, [att], w["w_out"])
        else:
            o = l // 2
            w_out = od_w_out[o]
            logit = ret_decay_logit[o]
            ret_g = ret_norm[o].reshape(1, -1)
            skip = hy_skip[o].reshape(1, -1)
            conv_b = hy_conv_b[o].reshape(1, -1)
            filt_args = (hy_filt_w1[o], hy_filt_b1[o], hy_filt_w2[o], hy_filt_b2[o], hy_filt_freq[o],
                         hy_filt_w3[o], hy_decay[o])
            for stream in ("ctx", "smp"):
                x, seq, latent = (xc, SEQ, False) if stream == "ctx" else (xs, DEC_SEQ, True)
                fwd, inv = _dft_matrices(seq)
                fwd_b, inv_b = jnp.asarray(fwd).astype(BF16), jnp.asarray(inv).astype(BF16)
                spectra = _filter_spectra(seq, *filt_args, fwd_b, TC_HY)
                u = _odd_proj(x, mod_tab, latent, od_w_in, o)
                if stream == "ctx":
                    o_ret, st = _retention(u, logit, ret_g, seq, RET_HEADS, seqs=SEQS_PER_STEP)
                    outs["ret"] = st
                    o_hy = _hyena(u, seq, HY_CH, HY_CH, hy_conv_w[o], conv_b, spectra, skip, fwd_b, inv_b)
                else:
                    (o_ret,) = _retention(u, logit, ret_g, seq, RET_HEADS_LAT, s0=state_ret[:, o])
                    o_hy = _hyena(u, seq, TC_HY, HY_SUB_LAT, hy_conv_w[o], conv_b, spectra, skip, fwd_b, inv_b)
                x = _out_proj(x, mod_tab, latent, [o_ret, o_hy], w_out)
                if stream == "ctx":
                    xc = x
                else:
                    xs = x
        xc = _ffn(xc, mod_tab, False, 2, ffn_w_gate, ffn_w_up, ffn_w_down, l, 1)
        xs = _ffn(xs, mod_tab, True, 2, ffn_w_gate, ffn_w_up, ffn_w_down, l, 1)

    return (
        xc.reshape(BATCH, SEQ, D_MODEL),
        xs.reshape(DEC_BATCH, DEC_SEQ, D_MODEL),
        outs["ckv"].reshape(BATCH, 1, SEQ, MLA_KV_RANK),
        outs["krope"].reshape(BATCH, 1, SEQ, MLA_ROPE),
        outs["k"].reshape(BATCH, 1, SEQ, GQA_KV_HEADS, HEAD_DIM),
        outs["v"].reshape(BATCH, 1, SEQ, GQA_KV_HEADS, HEAD_DIM),
        outs["ret"].reshape(BATCH, 1, 2, RET_HEADS, RET_DK, RET_DV),
    )
```

```python
import functools
import math

import numpy as np
import jax
import jax.numpy as jnp
from jax import lax
from jax.experimental import pallas as pl
from jax.experimental.pallas import tpu as pltpu

F32 = jnp.float32
BF16 = jnp.bfloat16

D_MODEL = 2048
BATCH = 32
SEQ = 256
DEPTH = 2
DEC_BATCH = 2
DEC_SEQ = 1024
PAST_LEN = 256
GRID_W = 64
N_MOD = 9
D_FF = 5632
ROPE_THETA = 10000.0
EPS = 1e-6

MLA_HEADS = 8
MLA_Q_RANK = 512
MLA_KV_RANK = 512
MLA_NOPE = 128
MLA_ROPE = 64
MLA_V = 128
GQA_HEADS = 8
GQA_KV_HEADS = 2
GQA_GROUP = GQA_HEADS // GQA_KV_HEADS
HEAD_DIM = 128
MLA_SCALE = (MLA_NOPE + MLA_ROPE) ** -0.5
GQA_SCALE = HEAD_DIM ** -0.5
MLA_QK_PAD = 256

RET_HEADS = 8
RET_DK = 128
RET_DV = 128
HY_CH = 1024
HY_ORDER = 2
HY_BANDS = 16
HY_FHID = 64
ODD_IN = 4 * RET_HEADS * RET_DK + (HY_ORDER + 1) * HY_CH

V7X_VMEM_BYTES = 64 * 1024 * 1024
LANES = 128
VMEM_LIMIT = V7X_VMEM_BYTES - 3 * 1024 * 1024

TM_FFN = 1024
TF_FFN = 512
TF_SUB = 256
TM_PROJ = 512
TM_ODD_CTX, TN_ODD_CTX = 2048, 512
TM_ODD_LAT, TN_ODD_LAT = 1024, 1024
TN_MOD = 1024
TQ_ATT = 256
TC_HY = 512
HY_SUB_LAT = 256
RET_HEADS_LAT = 2
SEQS_PER_STEP = 2


def _cparams(*sem):
    return pltpu.CompilerParams(dimension_semantics=sem, vmem_limit_bytes=VMEM_LIMIT)


def _const_spec(shape):
    nd = len(shape)
    return pl.BlockSpec(shape, lambda *_: (0,) * nd)


def _silu(x):
    return x * jax.nn.sigmoid(x)


def _rms_rows(x, width=None):
    ss = jnp.sum(x * x, axis=-1, keepdims=True)
    n = x.shape[-1] if width is None else width
    return x * lax.rsqrt(ss * (1.0 / n) + EPS)


def _mod_kernel(c_ref, w_ref, b_ref, o_ref):
    s = _silu(c_ref[...]).astype(BF16)
    o_ref[0] = jnp.dot(s, w_ref[0].astype(BF16), preferred_element_type=F32) + b_ref[0]


def _mod_vectors(cvec, mod_w, mod_b):
    depth, d, n = mod_w.shape
    rows = cvec.shape[0]
    return pl.pallas_call(
        _mod_kernel,
        out_shape=jax.ShapeDtypeStruct((depth, rows, n), F32),
        grid=(depth, n // TN_MOD),
        in_specs=[
            pl.BlockSpec((rows, d), lambda l, j: (0, 0)),
            pl.BlockSpec((1, d, TN_MOD), lambda l, j: (l, 0, j)),
            pl.BlockSpec((1, 1, TN_MOD), lambda l, j: (l, 0, j)),
        ],
        out_specs=pl.BlockSpec((1, rows, TN_MOD), lambda l, j: (l, 0, j)),
        compiler_params=_cparams("parallel", "parallel"),
        name="mod_vectors",
    )(cvec, mod_w, mod_b.reshape(depth, 1, n))


def _mod_specs(sub, latent, tile, extra_axes=0):
    group_of_tile = (lambda i: 1 + (i * tile) // DEC_SEQ) if latent else (lambda i: 0)

    def spec(k):
        if extra_axes:
            return pl.BlockSpec((1, 1, D_MODEL), lambda i, j: (group_of_tile(i) * N_MOD + 3 * sub + k, 0, 0))
        return pl.BlockSpec((1, 1, D_MODEL), lambda i: (group_of_tile(i) * N_MOD + 3 * sub + k, 0, 0))
    return [spec(0), spec(1), spec(2)]


def _modulated(x, sh_ref, sc_ref):
    return (_rms_rows(x) * (1.0 + sc_ref[0]) + sh_ref[0]).astype(BF16)


def _ffn_kernel(x_hbm, sh_ref, sc_ref, gt_ref, wg_ref, wu_ref, wd_ref, o_ref, x_buf, h_ref, x_sem):
    i, f = pl.program_id(0), pl.program_id(1)
    tm = x_buf.shape[0]
    half_gate = 0.5 * gt_ref[0]

    def x_copy(tile):
        return pltpu.make_async_copy(x_hbm.at[pl.ds(tile * tm, tm), :], x_buf, x_sem)

    def partial_out(h):
        out = None
        for c in range(TF_FFN // TF_SUB):
            cs = slice(c * TF_SUB, (c + 1) * TF_SUB)
            g = jnp.dot(h, wg_ref[:, cs].astype(BF16), preferred_element_type=F32)
            u = jnp.dot(h, wu_ref[:, cs].astype(BF16), preferred_element_type=F32)
            a = (_silu(g) * u).astype(BF16)
            t = jnp.dot(a, wd_ref[cs, :].astype(BF16), preferred_element_type=F32)
            out = t if out is None else out + t
        return half_gate * out

    @pl.when(f == 0)
    def _():
        @pl.when(i == 0)
        def _():
            x_copy(0).start()

        x_copy(i).wait()
        x = x_buf[...]
        h = _modulated(x, sh_ref, sc_ref)
        h_ref[...] = h
        o_ref[...] = x + partial_out(h)

    @pl.when(f > 0)
    def _():
        @pl.when(jnp.logical_and(f == pl.num_programs(1) // 2, i + 1 < pl.num_programs(0)))
        def _():
            x_copy(i + 1).start()

        o_ref[...] += partial_out(h_ref[...])


def _ffn(x, mod_tab, latent, sub, w_gate, w_up, w_down, layer, j):
    m, d = x.shape
    dff = w_gate.shape[-1]
    assert dff // TF_FFN >= 2
    return pl.pallas_call(
        _ffn_kernel,
        out_shape=jax.ShapeDtypeStruct((m, d), F32),
        grid=(m // TM_FFN, dff // TF_FFN),
        in_specs=[pl.BlockSpec(memory_space=pl.ANY)]
        + _mod_specs(sub, latent, TM_FFN, extra_axes=1)
        + [
            pl.BlockSpec((None, None, d, TF_FFN), lambda i, f: (layer, j, 0, f)),
            pl.BlockSpec((None, None, d, TF_FFN), lambda i, f: (layer, j, 0, f)),
            pl.BlockSpec((None, None, TF_FFN, d), lambda i, f: (layer, j, f, 0)),
        ],
        out_specs=pl.BlockSpec((TM_FFN, d), lambda i, f: (i, 0)),
        scratch_shapes=[pltpu.VMEM((TM_FFN, d), F32), pltpu.VMEM((TM_FFN, d), BF16), pltpu.SemaphoreType.DMA(())],
        compiler_params=_cparams("arbitrary", "arbitrary"),
        name="ffn",
    )(x, mod_tab, mod_tab, mod_tab, w_gate, w_up, w_down)


def _out_proj_kernel(*refs, n_parts):
    x_ref, gt_ref, w_ref = refs[0], refs[1], refs[2]
    part_refs = refs[3:3 + n_parts]
    o_ref, wb_ref = refs[3 + n_parts], refs[4 + n_parts]

    @pl.when(pl.program_id(0) == 0)
    def _():
        wb_ref[...] = w_ref[...].astype(BF16)

    acc, off = None, 0
    for p_ref in part_refs:
        width = p_ref.shape[1]
        t = jnp.dot(p_ref[...], wb_ref[off:off + width, :], preferred_element_type=F32)
        acc = t if acc is None else acc + t
        off += width
    o_ref[...] = x_ref[...] + gt_ref[0] * acc


def _out_proj(x, mod_tab, latent, parts, w_out):
    m, d = x.shape
    return pl.pallas_call(
        functools.partial(_out_proj_kernel, n_parts=len(parts)),
        out_shape=jax.ShapeDtypeStruct((m, d), F32),
        grid=(m // TM_PROJ,),
        in_specs=[pl.BlockSpec((TM_PROJ, d), lambda i: (i, 0)), _mod_specs(1, latent, TM_PROJ)[2],
                  pl.BlockSpec(w_out.shape, lambda i: (0, 0), pipeline_mode=pl.Buffered(1))]
        + [pl.BlockSpec((TM_PROJ, p.shape[1]), lambda i: (i, 0)) for p in parts],
        out_specs=pl.BlockSpec((TM_PROJ, d), lambda i: (i, 0)),
        scratch_shapes=[pltpu.VMEM(w_out.shape, BF16)],
        compiler_params=_cparams("arbitrary"),
        name="mixer_out_proj",
    )(x, mod_tab, w_out, *parts)


EV_QA, EV_KVA, EV_QG, EV_KG, EV_VG, EV_KR, EV_END = 0, 512, 1024, 2048, 2304, 2560, 2688


def _rotate_half(x, cos, sin):
    return x * cos + pltpu.roll(x, LANES // 2, axis=1) * sin


def _even_proj_kernel(*refs, rotary):
    (x_ref, sh_ref, sc_ref, w_in_ref, w_qb_ref, qn_ref, kvn_ref, ng_ref, rg_ref, gg_ref) = refs[:10]
    if rotary:
        cm_ref, sm_ref, cg_ref, sg_ref = refs[10:14]
        rot_m = lambda t: _rotate_half(t, cm_ref[...], sm_ref[...])
        rot_g = lambda t: _rotate_half(t, cg_ref[...], sg_ref[...])
    else:
        rot_m = rot_g = lambda t: t
    qm_ref, ckv_ref, kr_ref, qg_ref, kg_ref, vg_ref, wb_ref = refs[-7:]

    @pl.when(pl.program_id(0) == 0)
    def _():
        chunk = 128
        half = MLA_ROPE // 2

        def copy_rows(dst0, src0, n_chunks):
            def body(c, carry):
                dst = pl.ds(pl.multiple_of(dst0 + c * chunk, 16), chunk)
                src = pl.ds(pl.multiple_of(src0 + c * chunk, 8), chunk)
                wb_ref[dst, :] = w_in_ref[src, :].astype(BF16)
                return carry
            lax.fori_loop(0, n_chunks, body, 0)

        copy_rows(EV_QA, 0, EV_QG // chunk)
        copy_rows(EV_QG, EV_QG + MLA_ROPE, (EV_KR - EV_QG) // chunk)
        wb_ref[EV_KR:EV_END, :] = jnp.zeros((EV_END - EV_KR, wb_ref.shape[1]), BF16)
        wb_ref[EV_KR:EV_KR + half, :] = w_in_ref[EV_QG:EV_QG + half, :].astype(BF16)
        wb_ref[EV_KR + LANES // 2:EV_KR + LANES // 2 + half, :] = w_in_ref[EV_QG + half:EV_QG + MLA_ROPE, :].astype(BF16)

    h = _modulated(x_ref[...], sh_ref, sc_ref)
    u = _qk(h, wb_ref[...])

    nope_gain = ng_ref[0:1, :] * MLA_SCALE
    rope_gain = rg_ref[0:1, :] * MLA_SCALE
    gqa_gain = gg_ref[0:1, :] * GQA_SCALE

    qa = (_rms_rows(u[:, EV_QA:EV_KVA]) * qn_ref[...]).astype(BF16)
    q = jnp.dot(qa, w_qb_ref[...], preferred_element_type=F32)
    for hd in range(MLA_HEADS):
        c0 = hd * MLA_QK_PAD
        nope = _rms_rows(q[:, c0:c0 + MLA_NOPE]) * nope_gain
        rope = _rms_rows(q[:, c0 + MLA_NOPE:c0 + MLA_QK_PAD], MLA_ROPE) * rope_gain
        qm_ref[:, c0:c0 + MLA_NOPE] = nope.astype(BF16)
        qm_ref[:, c0 + MLA_NOPE:c0 + MLA_QK_PAD] = rot_m(rope).astype(BF16)

    ckv_ref[...] = _rms_rows(u[:, EV_KVA:EV_QG]) * kvn_ref[...]
    kr_ref[...] = rot_m(_rms_rows(u[:, EV_KR:EV_END], MLA_ROPE) * rg_ref[1:2, :])

    for hd in range(GQA_HEADS):
        c0 = EV_QG + hd * HEAD_DIM
        qh = _rms_rows(u[:, c0:c0 + HEAD_DIM]) * gqa_gain
        qg_ref[:, hd * HEAD_DIM:(hd + 1) * HEAD_DIM] = rot_g(qh).astype(BF16)
    for hd in range(GQA_KV_HEADS):
        c0 = EV_KG + hd * HEAD_DIM
        kh = _rms_rows(u[:, c0:c0 + HEAD_DIM]) * gg_ref[1:2, :]
        kg_ref[:, hd * HEAD_DIM:(hd + 1) * HEAD_DIM] = rot_g(kh)
    vg_ref[...] = u[:, EV_VG:EV_KR]


def _even_proj(x, mod_tab, w, tables=None):
    m, d = x.shape
    tm = TM_PROJ
    row = lambda w_: pl.BlockSpec((tm, w_), lambda i: (i, 0))
    tab_specs, tab_args = [], []
    if tables is not None:
        tiles_per_seq = tables[0].shape[0] // tm
        tab_specs = [pl.BlockSpec((tm, LANES), lambda i: (i % tiles_per_seq, 0))] * 4
        tab_args = list(tables)
    return pl.pallas_call(
        functools.partial(_even_proj_kernel, rotary=tables is not None),
        out_shape=[
            jax.ShapeDtypeStruct((m, MLA_HEADS * MLA_QK_PAD), BF16),
            jax.ShapeDtypeStruct((m, MLA_KV_RANK), F32),
            jax.ShapeDtypeStruct((m, LANES), F32),
            jax.ShapeDtypeStruct((m, GQA_HEADS * HEAD_DIM), BF16),
            jax.ShapeDtypeStruct((m, GQA_KV_HEADS * HEAD_DIM), F32),
            jax.ShapeDtypeStruct((m, GQA_KV_HEADS * HEAD_DIM), F32),
        ],
        grid=(m // tm,),
        in_specs=[row(d)] + _mod_specs(1, tables is not None, tm)[:2]
        + [pl.BlockSpec((None,) + w["w_in"].shape[1:], lambda i: (w["layer"], 0, 0)),
           _const_spec(w["w_qb"].shape), _const_spec(w["q_norm"].shape),
           _const_spec(w["kv_norm"].shape), _const_spec(w["nope_g"].shape), _const_spec(w["rope_g"].shape),
           _const_spec(w["gqa_g"].shape)] + tab_specs,
        out_specs=[row(MLA_HEADS * MLA_QK_PAD), row(MLA_KV_RANK), row(LANES), row(GQA_HEADS * HEAD_DIM),
                   row(GQA_KV_HEADS * HEAD_DIM), row(GQA_KV_HEADS * HEAD_DIM)],
        scratch_shapes=[pltpu.VMEM((EV_END, d), BF16)],
        compiler_params=_cparams("arbitrary"),
        name="even_proj",
    )(x, mod_tab, mod_tab, w["w_in"], w["w_qb"], w["q_norm"], w["kv_norm"], w["nope_g"], w["rope_g"], w["gqa_g"],
      *tab_args)


def _kv_expand_kernel(ckv_ref, kr_ref, w_ref, ng_ref, km_ref, v_ref):
    kv = jnp.dot(ckv_ref[...].astype(BF16), w_ref[...], preferred_element_type=F32)
    kr = kr_ref[...].astype(BF16)
    for hd in range(MLA_HEADS):
        c0 = hd * (MLA_NOPE + MLA_V)
        km_ref[:, hd * MLA_QK_PAD:hd * MLA_QK_PAD + MLA_NOPE] = (
            _rms_rows(kv[:, c0:c0 + MLA_NOPE]) * ng_ref[1:2, :]).astype(BF16)
        km_ref[:, hd * MLA_QK_PAD + MLA_NOPE:(hd + 1) * MLA_QK_PAD] = kr
        v_ref[:, hd * MLA_V:(hd + 1) * MLA_V] = kv[:, c0 + MLA_NOPE:c0 + MLA_NOPE + MLA_V].astype(BF16)


def _kv_expand(ckv, kr, w_kvb_bf16, nope_g, tr):
    r = ckv.shape[0]
    row = lambda w_: pl.BlockSpec((tr, w_), lambda i: (i, 0))
    return pl.pallas_call(
        _kv_expand_kernel,
        out_shape=[jax.ShapeDtypeStruct((r, MLA_HEADS * MLA_QK_PAD), BF16),
                   jax.ShapeDtypeStruct((r, MLA_HEADS * MLA_V), BF16)],
        grid=(r // tr,),
        in_specs=[row(MLA_KV_RANK), row(LANES), _const_spec(w_kvb_bf16.shape), _const_spec(nope_g.shape)],
        out_specs=[row(MLA_HEADS * MLA_QK_PAD), row(MLA_HEADS * MLA_V)],
        compiler_params=_cparams("parallel"),
        name="kv_expand",
    )(ckv, kr, w_kvb_bf16, nope_g)


def _softmax_pv(score_list, value_list):
    mx = None
    for s in score_list:
        m_ = jnp.max(s, axis=-1, keepdims=True)
        mx = m_ if mx is None else jnp.maximum(mx, m_)
    den, acc = None, None
    for s, v in zip(score_list, value_list):
        p = jnp.exp(s - mx)
        l_ = jnp.sum(p, axis=-1, keepdims=True)
        o_ = jnp.dot(p.astype(BF16), v, preferred_element_type=F32)
        den = l_ if den is None else den + l_
        acc = o_ if acc is None else acc + o_
    return acc / den


def _qk(q, k):
    return lax.dot_general(q, k, (((1,), (1,)), ((), ())), preferred_element_type=F32)


def _attn_kernel(*refs, has_cache, seqs):
    if has_cache:
        qm_ref, qg_ref, km_ref, v_ref, kg_ref, vg_ref, kmc_ref, vc_ref, kgc_ref, vgc_ref, o_ref = refs
    else:
        qm_ref, qg_ref, km_ref, v_ref, kg_ref, vg_ref, o_ref = refs
    rows_q, rows_k = qm_ref.shape[0] // seqs, km_ref.shape[0] // seqs
    base = MLA_HEADS * MLA_V
    for sq in range(seqs):
        rq = slice(sq * rows_q, (sq + 1) * rows_q)
        rk = slice(sq * rows_k, (sq + 1) * rows_k)
        for hd in range(MLA_HEADS):
            qs = slice(hd * MLA_QK_PAD, (hd + 1) * MLA_QK_PAD)
            vs = slice(hd * MLA_V, (hd + 1) * MLA_V)
            q = qm_ref[rq, qs]
            scores = [_qk(q, km_ref[rk, qs])]
            values = [v_ref[rk, vs]]
            if has_cache:
                scores.append(_qk(q, kmc_ref[:, qs]))
                values.append(vc_ref[:, vs])
            o_ref[rq, vs] = _softmax_pv(scores, values).astype(o_ref.dtype)
        for kvh in range(GQA_KV_HEADS):
            ks = slice(kvh * HEAD_DIM, (kvh + 1) * HEAD_DIM)
            k = kg_ref[rk, ks].astype(BF16)
            v = vg_ref[rk, ks].astype(BF16)
            if has_cache:
                kc = kgc_ref[:, ks].astype(BF16)
                vc = vgc_ref[:, ks].astype(BF16)
            for g in range(GQA_GROUP):
                hd = kvh * GQA_GROUP + g
                hs = slice(hd * HEAD_DIM, (hd + 1) * HEAD_DIM)
                q = qg_ref[rq, hs]
                scores, values = [_qk(q, k)], [v]
                if has_cache:
                    scores.append(_qk(q, kc))
                    values.append(vc)
                o_ref[rq, base + hd * HEAD_DIM:base + (hd + 1) * HEAD_DIM] = (
                    _softmax_pv(scores, values).astype(o_ref.dtype))


def _attention(qm, qg, km, v, kg, vg, seq, tq, cache=None, seqs=1):
    m = qm.shape[0]
    nq = seq // tq
    assert seqs == 1 or (cache is None and nq == 1)
    qrow = lambda w_: pl.BlockSpec((seqs * tq, w_), lambda i: (i, 0))
    krow = lambda w_: pl.BlockSpec((seqs * seq, w_), lambda i: (i // nq, 0))
    in_specs = [qrow(qm.shape[1]), qrow(qg.shape[1]), krow(km.shape[1]), krow(v.shape[1]),
                krow(kg.shape[1]), krow(vg.shape[1])]
    args = [qm, qg, km, v, kg, vg]
    if cache is not None:
        past = cache[0].shape[0] // (m // seq)
        crow = lambda w_: pl.BlockSpec((past, w_), lambda i: (i // nq, 0))
        in_specs += [crow(c.shape[1]) for c in cache]
        args += list(cache)
    width = MLA_HEADS * MLA_V + GQA_HEADS * HEAD_DIM
    return pl.pallas_call(
        functools.partial(_attn_kernel, has_cache=cache is not None, seqs=seqs),
        out_shape=jax.ShapeDtypeStruct((m, width), BF16),
        grid=(m // (seqs * tq),),
        in_specs=in_specs,
        out_specs=qrow(width),
        compiler_params=_cparams("parallel"),
        name="attention",
    )(*args)


def _odd_proj_kernel(x_hbm, sh_ref, sc_ref, w_ref, o_ref, x_buf, h_ref, x_sem):
    i, j = pl.program_id(0), pl.program_id(1)
    tm = x_buf.shape[0]

    def x_copy(tile):
        return pltpu.make_async_copy(x_hbm.at[pl.ds(tile * tm, tm), :], x_buf, x_sem)

    def project(h):
        return jnp.dot(h, w_ref[...].astype(BF16), preferred_element_type=F32).astype(o_ref.dtype)

    @pl.when(j == 0)
    def _():
        @pl.when(i == 0)
        def _():
            x_copy(0).start()

        x_copy(i).wait()
        h = _modulated(x_buf[...], sh_ref, sc_ref)
        h_ref[...] = h
        o_ref[...] = project(h)

    @pl.when(j > 0)
    def _():
        @pl.when(jnp.logical_and(j == pl.num_programs(1) // 2, i + 1 < pl.num_programs(0)))
        def _():
            x_copy(i + 1).start()

        o_ref[...] = project(h_ref[...])


def _odd_proj(x, mod_tab, latent, w_in, layer):
    m, d = x.shape
    n = w_in.shape[-1]
    tm, tn = (TM_ODD_LAT, TN_ODD_LAT) if latent else (TM_ODD_CTX, TN_ODD_CTX)
    assert n // tn >= 2
    return pl.pallas_call(
        _odd_proj_kernel,
        out_shape=jax.ShapeDtypeStruct((m, n), BF16),
        grid=(m // tm, n // tn),
        in_specs=[pl.BlockSpec(memory_space=pl.ANY)]
        + _mod_specs(1, latent, tm, extra_axes=1)[:2]
        + [pl.BlockSpec((None, d, tn), lambda i, j: (layer, 0, j))],
        out_specs=pl.BlockSpec((tm, tn), lambda i, j: (i, j)),
        scratch_shapes=[pltpu.VMEM((tm, d), F32), pltpu.VMEM((tm, d), BF16), pltpu.SemaphoreType.DMA(())],
        compiler_params=_cparams("arbitrary", "arbitrary"),
        name="odd_proj",
    )(x, mod_tab, mod_tab, w_in)


def _log_sigmoid(x):
    return jnp.minimum(x, 0.0) - jnp.log1p(jnp.exp(-jnp.abs(x)))


def _retention_kernel(*refs, heads, has_init, seqs):
    if has_init:
        logit_ref, q_ref, k_ref, v_ref, g_ref, rg_ref, s0_ref, o_ref, dec_ref, vec_ref = refs
    else:
        logit_ref, q_ref, k_ref, v_ref, g_ref, rg_ref, o_ref, st_ref, dec_ref, vec_ref = refs
    length = q_ref.shape[0] // seqs
    h0 = pl.program_id(0) * heads
    scale = RET_DK ** -0.5
    log_g = _log_sigmoid(logit_ref[...])
    lane_h = lax.broadcasted_iota(jnp.int32, (2, RET_HEADS), 1)

    def head_log_decays(hh):
        lg = jnp.sum(jnp.where(lane_h == h0 + hh, log_g, 0.0), axis=1, keepdims=True)
        return lg[0:1, :], lg[1:2, :]

    @pl.when(pl.program_id(1) == 0)
    def _():
        n_i = lax.broadcasted_iota(jnp.int32, (length, length), 0)
        m_i = lax.broadcasted_iota(jnp.int32, (length, length), 1)
        diff = (n_i - m_i).astype(F32)
        pos = lax.broadcasted_iota(jnp.int32, (length, RET_DK), 0).astype(F32)
        for hh in range(heads):
            lg_f, lg_b = head_log_decays(hh)
            dec_ref[hh] = scale * (jnp.where(diff >= 0, jnp.exp(jnp.maximum(diff, 0.0) * lg_f), 0.0)
                                   + jnp.where(diff <= 0, jnp.exp(jnp.maximum(-diff, 0.0) * lg_b), 0.0))
            if has_init:
                vec_ref[4 * hh + 2] = jnp.exp((pos + 1.0) * lg_f)
                vec_ref[4 * hh + 3] = jnp.exp((length - pos) * lg_b)
            else:
                vec_ref[4 * hh + 0] = jnp.exp((length - 1.0 - pos) * lg_f) * scale
                vec_ref[4 * hh + 1] = jnp.exp(pos * lg_b) * scale

    tn = (((0,), (0,)), ((), ()))
    for sq, hh in [(a, b) for a in range(seqs) for b in range(heads)]:
        cs = slice(hh * RET_DK, (hh + 1) * RET_DK)
        rs = slice(sq * length, (sq + 1) * length)
        q, k, v = q_ref[rs, cs], k_ref[rs, cs], v_ref[rs, cs]
        att = _qk(q, k) * dec_ref[hh]
        o = jnp.dot(att.astype(BF16), v, preferred_element_type=F32)
        if has_init:
            s0_f, s0_b = s0_ref[sq, 0, hh].astype(BF16), s0_ref[sq, 1, hh].astype(BF16)
            o = o + jnp.dot(q, s0_f, preferred_element_type=F32) * vec_ref[4 * hh + 2]
            o = o + jnp.dot(q, s0_b, preferred_element_type=F32) * vec_ref[4 * hh + 3]
        else:
            kf = k.astype(F32)
            k_f, k_b = (kf * vec_ref[4 * hh + 0]).astype(BF16), (kf * vec_ref[4 * hh + 1]).astype(BF16)
            st_ref[sq, 0, hh] = lax.dot_general(k_f, v, tn, preferred_element_type=F32)
            st_ref[sq, 1, hh] = lax.dot_general(k_b, v, tn, preferred_element_type=F32)
        gate = g_ref[rs, cs].astype(F32)
        o_ref[rs, cs] = (_rms_rows(o) * rg_ref[:, cs] * _silu(gate)).astype(o_ref.dtype)


def _retention(u, decay_logit, ret_g, seq, heads, s0=None, seqs=1):
    m = u.shape[0]
    b = m // seq
    hw = heads * RET_DK
    nh = RET_HEADS // heads
    blocks_per_part = RET_HEADS * RET_DK // hw
    col = lambda part: pl.BlockSpec((seqs * seq, hw), lambda j, i, part=part: (i, part * blocks_per_part + j))
    st_spec = pl.BlockSpec((seqs, 2, heads, RET_DK, RET_DV), lambda j, i: (i, 0, j, 0, 0))
    in_specs = [_const_spec(decay_logit.shape), col(0), col(1), col(2), col(3),
                pl.BlockSpec((1, hw), lambda j, i: (0, j))]
    args = [decay_logit, u, u, u, u, ret_g]
    if s0 is not None:
        in_specs.append(st_spec)
        args.append(s0)
    out_shape = [jax.ShapeDtypeStruct((m, RET_HEADS * RET_DV), BF16)]
    out_specs = [pl.BlockSpec((seqs * seq, hw), lambda j, i: (i, j))]
    if s0 is None:
        out_shape.append(jax.ShapeDtypeStruct((b, 2, RET_HEADS, RET_DK, RET_DV), F32))
        out_specs.append(st_spec)
    return pl.pallas_call(
        functools.partial(_retention_kernel, heads=heads, has_init=s0 is not None, seqs=seqs),
        out_shape=out_shape,
        grid=(nh, b // seqs),
        in_specs=in_specs,
        out_specs=out_specs,
        scratch_shapes=[pltpu.VMEM((heads, seq, seq), F32), pltpu.VMEM((4 * heads, seq, RET_DK), F32)],
        compiler_params=_cparams("parallel", "arbitrary"),
        name="retention",
    )(*args)


@functools.lru_cache(maxsize=None)
def _dft_matrices(length):
    n = 2 * length
    f = np.arange(length, dtype=np.float64)[:, None]
    t = np.arange(length, dtype=np.float64)[None, :]
    ang = 2.0 * np.pi * f * t / n
    fwd_a = np.cos(ang)
    fwd_b = np.sin(ang)
    fwd_b[0, :] = np.cos(np.pi * t[0])
    fwd = np.concatenate([fwd_a, fwd_b], axis=0)
    tt = (np.arange(length, dtype=np.float64) + length // 2)[:, None]
    ff = np.arange(length, dtype=np.float64)[None, :]
    ang_i = 2.0 * np.pi * ff * tt / n
    inv_a = 2.0 * np.cos(ang_i) / n
    inv_b = 2.0 * np.sin(ang_i) / n
    inv_a[:, 0] = 1.0 / n
    inv_b[:, 0] = np.cos(np.pi * tt[:, 0]) / n
    inv = np.concatenate([inv_a, inv_b], axis=1)
    return fwd.astype(np.float32), inv.astype(np.float32)


@functools.lru_cache(maxsize=None)
def _filter_features(length):
    t = np.arange(length, dtype=np.float64)
    tn = t / length
    bands = np.arange(1, HY_BANDS + 1, dtype=np.float64)
    ang = 2.0 * np.pi * tn[:, None] * bands[None, :]
    feat = np.concatenate([tn[:, None], np.sin(ang), np.cos(ang)], axis=-1)
    feat = np.pad(feat, ((0, 0), (0, LANES - feat.shape[1])))
    r = (np.abs(t - length // 2) / (length / 2))[:, None]
    return feat.astype(np.float32), r.astype(np.float32)


def _filter_spec_kernel(feat_ref, r_ref, w1_ref, b1_ref, w2_ref, b2_ref, fr_ref, w3_ref, dec_ref, fwd_ref, o_ref,
                        z_ref):
    hp = lax.Precision.HIGHEST

    @pl.when(pl.program_id(0) == 0)
    def _():
        z = jnp.sin(fr_ref[0:1, :] * (jnp.dot(feat_ref[...], w1_ref[...], precision=hp, preferred_element_type=F32)
                                      + b1_ref[...]))
        z_ref[...] = jnp.sin(fr_ref[1:2, :] * (jnp.dot(z, w2_ref[...], precision=hp, preferred_element_type=F32)
                                               + b2_ref[...]))

    filt = jnp.dot(z_ref[...], w3_ref[...], precision=hp, preferred_element_type=F32)
    filt = filt * jnp.exp(-r_ref[...] * jnp.abs(dec_ref[...]))
    spec = jnp.dot(fwd_ref[...], filt.astype(BF16), preferred_element_type=F32)
    length = spec.shape[0] // 2
    sa, sb = spec[:length], spec[length:]
    first = lax.broadcasted_iota(jnp.int32, sa.shape, 0) == 0
    o_ref[0:length, :] = sa
    o_ref[length:2 * length, :] = jnp.where(first, 0.0, sb)
    o_ref[2 * length:, :] = jnp.where(first, sb, sa)


def _filter_spectra(length, w1, b1, w2, b2, freq, w3, decay, fwd_bf16, tc):
    feat, r = _filter_features(length)
    nch = w3.shape[1]
    w1p = jnp.pad(w1, ((0, LANES - w1.shape[0]), (0, 0)))
    return pl.pallas_call(
        _filter_spec_kernel,
        out_shape=jax.ShapeDtypeStruct((3 * length, nch), F32),
        grid=(nch // tc,),
        in_specs=[_const_spec(feat.shape), _const_spec(r.shape), _const_spec(w1p.shape), _const_spec((1, HY_FHID)),
                  _const_spec(w2.shape), _const_spec((1, HY_FHID)), _const_spec(freq.shape),
                  pl.BlockSpec((HY_FHID, tc), lambda j: (0, j)), pl.BlockSpec((1, tc), lambda j: (0, j)),
                  _const_spec(fwd_bf16.shape)],
        out_specs=pl.BlockSpec((3 * length, tc), lambda j: (0, j)),
        scratch_shapes=[pltpu.VMEM((length, HY_FHID), F32)],
        compiler_params=_cparams("arbitrary"),
        name="hyena_filter_spectra",
    )(jnp.asarray(feat), jnp.asarray(r), w1p, b1.reshape(1, -1), w2, b2.reshape(1, -1), freq, w3,
      decay.reshape(1, -1), fwd_bf16)


def _hyena_kernel(v_ref, x1_ref, x2_ref, cwv_ref, cw1_ref, cw2_ref, cbv_ref, cb1_ref, cb2_ref,
                  h1_ref, h2_ref, sk1_ref, sk2_ref, fwd_ref, inv_ref, o_ref, *, sub, seqs):
    length, tc = v_ref.shape[0] // seqs, v_ref.shape[1]
    row = lax.broadcasted_iota(jnp.int32, (length, sub), 0)
    first, last = row == 0, row == length - 1

    for sq, c in [(a, b) for a in range(seqs) for b in range(tc // sub)]:
        cs = slice(c * sub, (c + 1) * sub)
        rs = slice(sq * length, (sq + 1) * length)

        def short_conv(x_ref, w_ref, b_ref):
            x = x_ref[rs, cs].astype(F32)
            prev = jnp.where(first, 0.0, pltpu.roll(x, 1, axis=0))
            nxt = jnp.where(last, 0.0, pltpu.roll(x, length - 1, axis=0))
            return prev * w_ref[0:1, cs] + x * w_ref[1:2, cs] + nxt * w_ref[2:3, cs] + b_ref[:, cs]

        def long_conv(z, h_ref):
            zs = jnp.dot(fwd_ref[...], z.astype(BF16), preferred_element_type=F32)
            za, zb = zs[:length], zs[length:]
            ha, hb, hc = h_ref[0:length, cs], h_ref[length:2 * length, cs], h_ref[2 * length:, cs]
            y = jnp.concatenate([za * ha - zb * hb, za * hb + zb * hc], axis=0).astype(BF16)
            return jnp.dot(inv_ref[...], y, preferred_element_type=F32)

        v = short_conv(v_ref, cwv_ref, cbv_ref)
        x1 = short_conv(x1_ref, cw1_ref, cb1_ref)
        x2 = short_conv(x2_ref, cw2_ref, cb2_ref)
        z = x1 * (long_conv(v, h1_ref) + sk1_ref[:, cs] * v)
        o_ref[rs, cs] = (x2 * (long_conv(z, h2_ref) + sk2_ref[:, cs] * z)).astype(o_ref.dtype)


def _hyena(u, seq, tc, sub, conv_w, conv_b, spectra, skip, fwd_bf16, inv_bf16, seqs=1):
    m = u.shape[0]
    b = m // (seqs * seq)
    nc = HY_CH // tc
    hy0 = 4 * RET_HEADS * RET_DK // tc
    ucol = lambda part: pl.BlockSpec((seqs * seq, tc), lambda i, j, part=part: (i, hy0 + part * nc + j))
    wcol = lambda rows, part: pl.BlockSpec((rows, tc), lambda i, j, part=part: (0, part * nc + j))
    return pl.pallas_call(
        functools.partial(_hyena_kernel, sub=sub, seqs=seqs),
        out_shape=jax.ShapeDtypeStruct((m, HY_CH), BF16),
        grid=(b, nc),
        in_specs=[ucol(0), ucol(1), ucol(2), wcol(3, 0), wcol(3, 1), wcol(3, 2), wcol(1, 0), wcol(1, 1), wcol(1, 2),
                  wcol(3 * seq, 0), wcol(3 * seq, 1), wcol(1, 0), wcol(1, 1),
                  _const_spec(fwd_bf16.shape), _const_spec(inv_bf16.shape)],
        out_specs=pl.BlockSpec((seqs * seq, tc), lambda i, j: (i, j)),
        compiler_params=_cparams("parallel", "parallel"),
        name="hyena",
    )(u, u, u, conv_w, conv_w, conv_w, conv_b, conv_b, conv_b, spectra, spectra, skip, skip, fwd_bf16, inv_bf16)


@functools.lru_cache(maxsize=None)
def _rope_tables(length):
    rows = np.repeat(np.arange(length // GRID_W, dtype=np.float64), GRID_W)
    cols = np.tile(np.arange(GRID_W, dtype=np.float64), length // GRID_W)

    def pack(dim):
        half = dim // 2
        freq = ROPE_THETA ** (-np.arange(0, half, 2, dtype=np.float64) / half)
        ang = np.concatenate([rows[:, None] * freq[None], cols[:, None] * freq[None]], axis=-1)
        pad = ((0, 0), (0, LANES // 2 - half))
        cos = np.pad(np.cos(ang), pad, constant_values=1.0)
        sin = np.pad(np.sin(ang), pad)
        return (np.concatenate([cos, cos], axis=-1).astype(np.float32),
                np.concatenate([-sin, sin], axis=-1).astype(np.float32))

    cos_m, sin_m = pack(MLA_ROPE)
    cos_g, sin_g = pack(HEAD_DIM)
    return cos_m, sin_m, cos_g, sin_g


def _spread_rope(a):
    half = MLA_ROPE // 2
    z = jnp.zeros(a.shape[:-1] + (LANES // 2 - half,), a.dtype)
    return jnp.concatenate([a[..., :half], z, a[..., half:], z], axis=-1)


def _unspread_rope(a):
    half = MLA_ROPE // 2
    return jnp.concatenate([a[..., :half], a[..., LANES // 2:LANES // 2 + half]], axis=-1)


def _even_weights(layer, w_in, q_norm, w_qb, kv_norm, w_kvb, nope_g, rope_g, gqa_g, w_out):
    wq = w_qb.reshape(MLA_Q_RANK, MLA_HEADS, MLA_NOPE + MLA_ROPE)
    wq = jnp.concatenate([wq[..., :MLA_NOPE], _spread_rope(wq[..., MLA_NOPE:])], axis=-1)
    return {
        "layer": layer,
        "w_in": jnp.swapaxes(w_in, 1, 2),
        "w_qb": wq.reshape(MLA_Q_RANK, MLA_HEADS * MLA_QK_PAD).astype(BF16),
        "w_kvb": w_kvb.astype(BF16),
        "w_out": w_out,
        "q_norm": q_norm.reshape(1, -1),
        "kv_norm": kv_norm.reshape(1, -1),
        "nope_g": nope_g,
        "rope_g": _spread_rope(rope_g),
        "gqa_g": gqa_g,
    }


def kernel(x_prompt, x_sample, cache_mla_ckv, cache_mla_krope, cache_gqa_k, cache_gqa_v, state_ret, c, c_ctx, mod_w, mod_b, ffn_w_gate, ffn_w_up, ffn_w_down, ev_w_in, mla_q_norm, mla_w_qb, mla_kv_norm, mla_w_kvb, mla_nope_norm, mla_rope_norm, gqa_qk_norm, ev_w_out, od_w_in, ret_decay_logit, ret_norm, hy_conv_w, hy_conv_b, hy_filt_w1, hy_filt_b1, hy_filt_w2, hy_filt_b2, hy_filt_freq, hy_filt_w3, hy_decay, hy_skip, od_w_out):
    xc = x_prompt.reshape(BATCH * SEQ, D_MODEL)
    xs = x_sample.reshape(DEC_BATCH * DEC_SEQ, D_MODEL)

    cvec = jnp.concatenate([c_ctx[None, :], c, jnp.zeros((8 - 1 - DEC_BATCH, D_MODEL), F32)], axis=0)
    mod_all = _mod_vectors(cvec, mod_w, mod_b)

    tables = tuple(jnp.asarray(t) for t in _rope_tables(DEC_SEQ))

    outs = {}
    for l in range(DEPTH):
        mod_tab = mod_all[l, :1 + DEC_BATCH].reshape((1 + DEC_BATCH) * N_MOD, 1, D_MODEL)
        xc = _ffn(xc, mod_tab, False, 0, ffn_w_gate, ffn_w_up, ffn_w_down, l, 0)
        xs = _ffn(xs, mod_tab, True, 0, ffn_w_gate, ffn_w_up, ffn_w_down, l, 0)
        if l % 2 == 0:
            e = l // 2
            w = _even_weights(e, ev_w_in, mla_q_norm[e], mla_w_qb[e], mla_kv_norm[e], mla_w_kvb[e],
                              mla_nope_norm[e], mla_rope_norm[e], gqa_qk_norm[e], ev_w_out[e])
            qm, ckv, kr, qg, kg, vg = _even_proj(xc, mod_tab, w)
            km, v = _kv_expand(ckv, kr, w["w_kvb"], w["nope_g"], 2 * TM_PROJ)
            att = _attention(qm, qg, km, v, kg, vg, SEQ, SEQ, seqs=SEQS_PER_STEP)
            xc = _out_proj(xc, mod_tab, False, [att], w["w_out"])
            outs["ckv"], outs["krope"], outs["k"], outs["v"] = ckv, _unspread_rope(kr), kg, vg
            qm, ckv, kr, qg, kg, vg = _even_proj(xs, mod_tab, w, tables)
            km, v = _kv_expand(ckv, kr, w["w_kvb"], w["nope_g"], TM_PROJ)
            c_ckv = cache_mla_ckv[:, e].reshape(DEC_BATCH * PAST_LEN, MLA_KV_RANK)
            c_kr = _spread_rope(cache_mla_krope[:, e].reshape(DEC_BATCH * PAST_LEN, MLA_ROPE))
            kmc, vc = _kv_expand(c_ckv, c_kr, w["w_kvb"], w["nope_g"], PAST_LEN)
            kgc = cache_gqa_k[:, e].reshape(DEC_BATCH * PAST_LEN, GQA_KV_HEADS * HEAD_DIM)
            vgc = cache_gqa_v[:, e].reshape(DEC_BATCH * PAST_LEN, GQA_KV_HEADS * HEAD_DIM)
            att = _attention(qm, qg, km, v, kg, vg, DEC_SEQ, TQ_ATT, cache=(kmc, vc, kgc, vgc))
            xs = _out_proj(xs, mod_tab, True, [att], w["w_out"])
        else:
            o = l // 2
            w_out = od_w_out[o]
            logit = ret_decay_logit[o]
            ret_g = ret_norm[o].reshape(1, -1)
            skip = hy_skip[o].reshape(1, -1)
            conv_b = hy_conv_b[o].reshape(1, -1)
            filt_args = (hy_filt_w1[o], hy_filt_b1[o], hy_filt_w2[o], hy_filt_b2[o], hy_filt_freq[o],
                         hy_filt_w3[o], hy_decay[o])
            for stream in ("ctx", "smp"):
                x, seq, latent = (xc, SEQ, False) if stream == "ctx" else (xs, DEC_SEQ, True)
                fwd, inv = _dft_matrices(seq)
                fwd_b, inv_b = jnp.asarray(fwd).astype(BF16), jnp.asarray(inv).astype(BF16)
                spectra = _filter_spectra(seq, *filt_args, fwd_b, TC_HY)
                u = _odd_proj(x, mod_tab, latent, od_w_in, o)
                if stream == "ctx":
                    o_ret, st = _retention(u, logit, ret_g, seq, RET_HEADS, seqs=SEQS_PER_STEP)
                    outs["ret"] = st
                    o_hy = _hyena(u, seq, HY_CH, HY_CH, hy_conv_w[o], conv_b, spectra, skip, fwd_b, inv_b, seqs=SEQS_PER_STEP)
                else:
                    (o_ret,) = _retention(u, logit, ret_g, seq, RET_HEADS_LAT, s0=state_ret[:, o])
                    o_hy = _hyena(u, seq, TC_HY, HY_SUB_LAT, hy_conv_w[o], conv_b, spectra, skip, fwd_b, inv_b)
                x = _out_proj(x, mod_tab, latent, [o_ret, o_hy], w_out)
                if stream == "ctx":
                    xc = x
                else:
                    xs = x
        xc = _ffn(xc, mod_tab, False, 2, ffn_w_gate, ffn_w_up, ffn_w_down, l, 1)
        xs = _ffn(xs, mod_tab, True, 2, ffn_w_gate, ffn_w_up, ffn_w_down, l, 1)

    return (
        xc.reshape(BATCH, SEQ, D_MODEL),
        xs.reshape(DEC_BATCH, DEC_SEQ, D_MODEL),
        outs["ckv"].reshape(BATCH, 1, SEQ, MLA_KV_RANK),
        outs["krope"].reshape(BATCH, 1, SEQ, MLA_ROPE),
        outs["k"].reshape(BATCH, 1, SEQ, GQA_KV_HEADS, HEAD_DIM),
        outs["v"].reshape(BATCH, 1, SEQ, GQA_KV_HEADS, HEAD_DIM),
        outs["ret"].reshape(BATCH, 1, 2, RET_HEADS, RET_DK, RET_DV),
    )
```

```python
import functools
import math

import numpy as np
import jax
import jax.numpy as jnp
from jax import lax
from jax.experimental import pallas as pl
from jax.experimental.pallas import tpu as pltpu

F32 = jnp.float32
BF16 = jnp.bfloat16

D_MODEL = 2048
BATCH = 32
SEQ = 256
DEPTH = 2
DEC_BATCH = 2
DEC_SEQ = 1024
PAST_LEN = 256
GRID_W = 64
N_MOD = 9
D_FF = 5632
ROPE_THETA = 10000.0
EPS = 1e-6

MLA_HEADS = 8
MLA_Q_RANK = 512
MLA_KV_RANK = 512
MLA_NOPE = 128
MLA_ROPE = 64
MLA_V = 128
GQA_HEADS = 8
GQA_KV_HEADS = 2
GQA_GROUP = GQA_HEADS // GQA_KV_HEADS
HEAD_DIM = 128
MLA_SCALE = (MLA_NOPE + MLA_ROPE) ** -0.5
GQA_SCALE = HEAD_DIM ** -0.5
MLA_QK_PAD = 256

RET_HEADS = 8
RET_DK = 128
RET_DV = 128
HY_CH = 1024
HY_ORDER = 2
HY_BANDS = 16
HY_FHID = 64
ODD_IN = 4 * RET_HEADS * RET_DK + (HY_ORDER + 1) * HY_CH

V7X_VMEM_BYTES = 64 * 1024 * 1024
LANES = 128
VMEM_LIMIT = V7X_VMEM_BYTES - 3 * 1024 * 1024

TM_FFN = 1024
TF_FFN = 512
TF_SUB = 256
TM_PROJ = 512
TM_ODD_CTX, TN_ODD_CTX = 2048, 512
TM_ODD_LAT, TN_ODD_LAT = 1024, 1024
TN_MOD = 1024
TQ_ATT = 256
TC_HY = 512
HY_SUB_LAT = 256
RET_HEADS_LAT = 2
SEQS_PER_STEP = 2


def _cparams(*sem):
    return pltpu.CompilerParams(dimension_semantics=sem, vmem_limit_bytes=VMEM_LIMIT)


def _const_spec(shape):
    nd = len(shape)
    return pl.BlockSpec(shape, lambda *_: (0,) * nd)


def _silu(x):
    return x * jax.nn.sigmoid(x)


def _rms_rows(x, width=None):
    ss = jnp.sum(x * x, axis=-1, keepdims=True)
    n = x.shape[-1] if width is None else width
    return x * lax.rsqrt(ss * (1.0 / n) + EPS)


def _mod_tile(c_ref, w_ref, b_ref):
    s = _silu(c_ref[...]).astype(BF16)
    return jnp.dot(s, w_ref[...].astype(BF16), preferred_element_type=F32) + b_ref[...]


def _mod_kernel(c_ref, w_ref, b_ref, o_ref):
    o_ref[0] = _mod_tile(c_ref, w_ref.at[0], b_ref.at[0])


def _mod_vectors(cvec, mod_w, mod_b, depth):
    _, d, n = mod_w.shape
    rows = cvec.shape[0]
    return pl.pallas_call(
        _mod_kernel,
        out_shape=jax.ShapeDtypeStruct((depth, rows, n), F32),
        grid=(depth, n // TN_MOD),
        in_specs=[
            pl.BlockSpec((rows, d), lambda l, j: (0, 0)),
            pl.BlockSpec((1, d, TN_MOD), lambda l, j: (l, 0, j)),
            pl.BlockSpec((1, 1, TN_MOD), lambda l, j: (l, 0, j)),
        ],
        out_specs=pl.BlockSpec((1, rows, TN_MOD), lambda l, j: (l, 0, j)),
        compiler_params=_cparams("parallel", "parallel"),
        name="mod_vectors",
    )(cvec, mod_w, mod_b.reshape(mod_b.shape[0], 1, n))


def _mod_specs(sub, latent, tile, extra_axes=0):
    group_of_tile = (lambda i: 1 + (i * tile) // DEC_SEQ) if latent else (lambda i: 0)

    def spec(k):
        if extra_axes:
            return pl.BlockSpec((1, 1, D_MODEL), lambda i, j: (group_of_tile(i) * N_MOD + 3 * sub + k, 0, 0))
        return pl.BlockSpec((1, 1, D_MODEL), lambda i: (group_of_tile(i) * N_MOD + 3 * sub + k, 0, 0))
    return [spec(0), spec(1), spec(2)]


def _modulated(x, sh_ref, sc_ref):
    return (_rms_rows(x) * (1.0 + sc_ref[0]) + sh_ref[0]).astype(BF16)


def _ffn_kernel(x_hbm, sh_ref, sc_ref, gt_ref, wg_ref, wu_ref, wd_ref, o_ref, x_buf, h_ref, x_sem):
    i, f = pl.program_id(0), pl.program_id(1)
    tm = x_buf.shape[0]
    half_gate = 0.5 * gt_ref[0]

    def x_copy(tile):
        return pltpu.make_async_copy(x_hbm.at[pl.ds(tile * tm, tm), :], x_buf, x_sem)

    def partial_out(h):
        out = None
        for c in range(TF_FFN // TF_SUB):
            cs = slice(c * TF_SUB, (c + 1) * TF_SUB)
            g = jnp.dot(h, wg_ref[:, cs].astype(BF16), preferred_element_type=F32)
            u = jnp.dot(h, wu_ref[:, cs].astype(BF16), preferred_element_type=F32)
            a = (_silu(g) * u).astype(BF16)
            t = jnp.dot(a, wd_ref[cs, :].astype(BF16), preferred_element_type=F32)
            out = t if out is None else out + t
        return half_gate * out

    @pl.when(f == 0)
    def _():
        @pl.when(i == 0)
        def _():
            x_copy(0).start()

        x_copy(i).wait()
        x = x_buf[...]
        h = _modulated(x, sh_ref, sc_ref)
        h_ref[...] = h
        o_ref[...] = x + partial_out(h)

    @pl.when(f > 0)
    def _():
        @pl.when(jnp.logical_and(f == pl.num_programs(1) // 2, i + 1 < pl.num_programs(0)))
        def _():
            x_copy(i + 1).start()

        o_ref[...] += partial_out(h_ref[...])


def _ffn(x, mod_tab, latent, sub, w_gate, w_up, w_down, layer, j):
    m, d = x.shape
    dff = w_gate.shape[-1]
    assert dff // TF_FFN >= 2
    return pl.pallas_call(
        _ffn_kernel,
        out_shape=jax.ShapeDtypeStruct((m, d), F32),
        grid=(m // TM_FFN, dff // TF_FFN),
        in_specs=[pl.BlockSpec(memory_space=pl.ANY)]
        + _mod_specs(sub, latent, TM_FFN, extra_axes=1)
        + [
            pl.BlockSpec((None, None, d, TF_FFN), lambda i, f: (layer, j, 0, f)),
            pl.BlockSpec((None, None, d, TF_FFN), lambda i, f: (layer, j, 0, f)),
            pl.BlockSpec((None, None, TF_FFN, d), lambda i, f: (layer, j, f, 0)),
        ],
        out_specs=pl.BlockSpec((TM_FFN, d), lambda i, f: (i, 0)),
        scratch_shapes=[pltpu.VMEM((TM_FFN, d), F32), pltpu.VMEM((TM_FFN, d), BF16), pltpu.SemaphoreType.DMA(())],
        compiler_params=_cparams("arbitrary", "arbitrary"),
        name="ffn",
    )(x, mod_tab, mod_tab, mod_tab, w_gate, w_up, w_down)


def _out_proj_kernel(*refs, n_parts):
    x_ref, gt_ref, w_ref = refs[0], refs[1], refs[2]
    part_refs = refs[3:3 + n_parts]
    o_ref, wb_ref = refs[3 + n_parts], refs[4 + n_parts]

    @pl.when(pl.program_id(0) == 0)
    def _():
        wb_ref[...] = w_ref[...].astype(BF16)

    acc, off = None, 0
    for p_ref in part_refs:
        width = p_ref.shape[1]
        t = jnp.dot(p_ref[...], wb_ref[off:off + width, :], preferred_element_type=F32)
        acc = t if acc is None else acc + t
        off += width
    o_ref[...] = x_ref[...] + gt_ref[0] * acc


def _out_proj(x, mod_tab, latent, parts, w_out):
    m, d = x.shape
    return pl.pallas_call(
        functools.partial(_out_proj_kernel, n_parts=len(parts)),
        out_shape=jax.ShapeDtypeStruct((m, d), F32),
        grid=(m // TM_PROJ,),
        in_specs=[pl.BlockSpec((TM_PROJ, d), lambda i: (i, 0)), _mod_specs(1, latent, TM_PROJ)[2],
                  pl.BlockSpec(w_out.shape, lambda i: (0, 0), pipeline_mode=pl.Buffered(1))]
        + [pl.BlockSpec((TM_PROJ, p.shape[1]), lambda i: (i, 0)) for p in parts],
        out_specs=pl.BlockSpec((TM_PROJ, d), lambda i: (i, 0)),
        scratch_shapes=[pltpu.VMEM(w_out.shape, BF16)],
        compiler_params=_cparams("arbitrary"),
        name="mixer_out_proj",
    )(x, mod_tab, w_out, *parts)


EV_QA, EV_KVA, EV_QG, EV_KG, EV_VG, EV_KR, EV_END = 0, 512, 1024, 2048, 2304, 2560, 2688


def _rotate_half(x, cos, sin):
    return x * cos + pltpu.roll(x, LANES // 2, axis=1) * sin


def _even_proj_kernel(*refs, rotary):
    (x_ref, sh_ref, sc_ref, w_in_ref, w_qb_ref, qn_ref, kvn_ref, ng_ref, rg_ref, gg_ref) = refs[:10]
    if rotary:
        cm_ref, sm_ref, cg_ref, sg_ref = refs[10:14]
        rot_m = lambda t: _rotate_half(t, cm_ref[...], sm_ref[...])
        rot_g = lambda t: _rotate_half(t, cg_ref[...], sg_ref[...])
    else:
        rot_m = rot_g = lambda t: t
    qm_ref, ckv_ref, kr_ref, qg_ref, kg_ref, vg_ref, wb_ref = refs[-7:]

    @pl.when(pl.program_id(0) == 0)
    def _():
        chunk = 128
        half = MLA_ROPE // 2

        def copy_rows(dst0, src0, n_chunks):
            def body(c, carry):
                dst = pl.ds(pl.multiple_of(dst0 + c * chunk, 16), chunk)
                src = pl.ds(pl.multiple_of(src0 + c * chunk, 8), chunk)
                wb_ref[dst, :] = w_in_ref[src, :].astype(BF16)
                return carry
            lax.fori_loop(0, n_chunks, body, 0)

        copy_rows(EV_QA, 0, EV_QG // chunk)
        copy_rows(EV_QG, EV_QG + MLA_ROPE, (EV_KR - EV_QG) // chunk)
        wb_ref[EV_KR:EV_END, :] = jnp.zeros((EV_END - EV_KR, wb_ref.shape[1]), BF16)
        wb_ref[EV_KR:EV_KR + half, :] = w_in_ref[EV_QG:EV_QG + half, :].astype(BF16)
        wb_ref[EV_KR + LANES // 2:EV_KR + LANES // 2 + half, :] = w_in_ref[EV_QG + half:EV_QG + MLA_ROPE, :].astype(BF16)

    h = _modulated(x_ref[...], sh_ref, sc_ref)
    u = _qk(h, wb_ref[...])

    nope_gain = ng_ref[0:1, :] * MLA_SCALE
    rope_gain = rg_ref[0:1, :] * MLA_SCALE
    gqa_gain = gg_ref[0:1, :] * GQA_SCALE

    qa = (_rms_rows(u[:, EV_QA:EV_KVA]) * qn_ref[...]).astype(BF16)
    q = jnp.dot(qa, w_qb_ref[...], preferred_element_type=F32)
    for hd in range(MLA_HEADS):
        c0 = hd * MLA_QK_PAD
        nope = _rms_rows(q[:, c0:c0 + MLA_NOPE]) * nope_gain
        rope = _rms_rows(q[:, c0 + MLA_NOPE:c0 + MLA_QK_PAD], MLA_ROPE) * rope_gain
        qm_ref[:, c0:c0 + MLA_NOPE] = nope.astype(BF16)
        qm_ref[:, c0 + MLA_NOPE:c0 + MLA_QK_PAD] = rot_m(rope).astype(BF16)

    ckv_ref[...] = _rms_rows(u[:, EV_KVA:EV_QG]) * kvn_ref[...]
    kr_ref[...] = rot_m(_rms_rows(u[:, EV_KR:EV_END], MLA_ROPE) * rg_ref[1:2, :])

    for hd in range(GQA_HEADS):
        c0 = EV_QG + hd * HEAD_DIM
        qh = _rms_rows(u[:, c0:c0 + HEAD_DIM]) * gqa_gain
        qg_ref[:, hd * HEAD_DIM:(hd + 1) * HEAD_DIM] = rot_g(qh).astype(BF16)
    for hd in range(GQA_KV_HEADS):
        c0 = EV_KG + hd * HEAD_DIM
        kh = _rms_rows(u[:, c0:c0 + HEAD_DIM]) * gg_ref[1:2, :]
        kg_ref[:, hd * HEAD_DIM:(hd + 1) * HEAD_DIM] = rot_g(kh)
    vg_ref[...] = u[:, EV_VG:EV_KR]


def _even_proj(x, mod_tab, w, tables=None):
    m, d = x.shape
    tm = TM_PROJ
    row = lambda w_: pl.BlockSpec((tm, w_), lambda i: (i, 0))
    tab_specs, tab_args = [], []
    if tables is not None:
        tiles_per_seq = tables[0].shape[0] // tm
        tab_specs = [pl.BlockSpec((tm, LANES), lambda i: (i % tiles_per_seq, 0))] * 4
        tab_args = list(tables)
    return pl.pallas_call(
        functools.partial(_even_proj_kernel, rotary=tables is not None),
        out_shape=[
            jax.ShapeDtypeStruct((m, MLA_HEADS * MLA_QK_PAD), BF16),
            jax.ShapeDtypeStruct((m, MLA_KV_RANK), F32),
            jax.ShapeDtypeStruct((m, LANES), F32),
            jax.ShapeDtypeStruct((m, GQA_HEADS * HEAD_DIM), BF16),
            jax.ShapeDtypeStruct((m, GQA_KV_HEADS * HEAD_DIM), F32),
            jax.ShapeDtypeStruct((m, GQA_KV_HEADS * HEAD_DIM), F32),
        ],
        grid=(m // tm,),
        in_specs=[row(d)] + _mod_specs(1, tables is not None, tm)[:2]
        + [pl.BlockSpec((None,) + w["w_in"].shape[1:], lambda i: (w["layer"], 0, 0)),
           _const_spec(w["w_qb"].shape), _const_spec(w["q_norm"].shape),
           _const_spec(w["kv_norm"].shape), _const_spec(w["nope_g"].shape), _const_spec(w["rope_g"].shape),
           _const_spec(w["gqa_g"].shape)] + tab_specs,
        out_specs=[row(MLA_HEADS * MLA_QK_PAD), row(MLA_KV_RANK), row(LANES), row(GQA_HEADS * HEAD_DIM),
                   row(GQA_KV_HEADS * HEAD_DIM), row(GQA_KV_HEADS * HEAD_DIM)],
        scratch_shapes=[pltpu.VMEM((EV_END, d), BF16)],
        compiler_params=_cparams("arbitrary"),
        name="even_proj",
    )(x, mod_tab, mod_tab, w["w_in"], w["w_qb"], w["q_norm"], w["kv_norm"], w["nope_g"], w["rope_g"], w["gqa_g"],
      *tab_args)


def _kv_expand_kernel(ckv_ref, kr_ref, w_ref, ng_ref, km_ref, v_ref):
    kv = jnp.dot(ckv_ref[...].astype(BF16), w_ref[...], preferred_element_type=F32)
    kr = kr_ref[...].astype(BF16)
    for hd in range(MLA_HEADS):
        c0 = hd * (MLA_NOPE + MLA_V)
        km_ref[:, hd * MLA_QK_PAD:hd * MLA_QK_PAD + MLA_NOPE] = (
            _rms_rows(kv[:, c0:c0 + MLA_NOPE]) * ng_ref[1:2, :]).astype(BF16)
        km_ref[:, hd * MLA_QK_PAD + MLA_NOPE:(hd + 1) * MLA_QK_PAD] = kr
        v_ref[:, hd * MLA_V:(hd + 1) * MLA_V] = kv[:, c0 + MLA_NOPE:c0 + MLA_NOPE + MLA_V].astype(BF16)


def _kv_expand(ckv, kr, w_kvb_bf16, nope_g, tr):
    r = ckv.shape[0]
    row = lambda w_: pl.BlockSpec((tr, w_), lambda i: (i, 0))
    return pl.pallas_call(
        _kv_expand_kernel,
        out_shape=[jax.ShapeDtypeStruct((r, MLA_HEADS * MLA_QK_PAD), BF16),
                   jax.ShapeDtypeStruct((r, MLA_HEADS * MLA_V), BF16)],
        grid=(r // tr,),
        in_specs=[row(MLA_KV_RANK), row(LANES), _const_spec(w_kvb_bf16.shape), _const_spec(nope_g.shape)],
        out_specs=[row(MLA_HEADS * MLA_QK_PAD), row(MLA_HEADS * MLA_V)],
        compiler_params=_cparams("parallel"),
        name="kv_expand",
    )(ckv, kr, w_kvb_bf16, nope_g)


def _softmax_pv(score_list, value_list):
    mx = None
    for s in score_list:
        m_ = jnp.max(s, axis=-1, keepdims=True)
        mx = m_ if mx is None else jnp.maximum(mx, m_)
    den, acc = None, None
    for s, v in zip(score_list, value_list):
        p = jnp.exp(s - mx)
        l_ = jnp.sum(p, axis=-1, keepdims=True)
        o_ = jnp.dot(p.astype(BF16), v, preferred_element_type=F32)
        den = l_ if den is None else den + l_
        acc = o_ if acc is None else acc + o_
    return acc / den


def _qk(q, k):
    return lax.dot_general(q, k, (((1,), (1,)), ((), ())), preferred_element_type=F32)


def _attn_kernel(*refs, has_cache, seqs, has_side):
    if has_side:
        c_ref, mw_ref, mb_ref, mo_ref = refs[-5], refs[-4], refs[-3], refs[-1]
        mo_ref[...] = _mod_tile(c_ref, mw_ref, mb_ref)
        refs = refs[:-5] + (refs[-2],)
    if has_cache:
        qm_ref, qg_ref, km_ref, v_ref, kg_ref, vg_ref, kmc_ref, vc_ref, kgc_ref, vgc_ref, o_ref = refs
    else:
        qm_ref, qg_ref, km_ref, v_ref, kg_ref, vg_ref, o_ref = refs
    rows_q, rows_k = qm_ref.shape[0] // seqs, km_ref.shape[0] // seqs
    base = MLA_HEADS * MLA_V
    for sq in range(seqs):
        rq = slice(sq * rows_q, (sq + 1) * rows_q)
        rk = slice(sq * rows_k, (sq + 1) * rows_k)
        for hd in range(MLA_HEADS):
            qs = slice(hd * MLA_QK_PAD, (hd + 1) * MLA_QK_PAD)
            vs = slice(hd * MLA_V, (hd + 1) * MLA_V)
            q = qm_ref[rq, qs]
            scores = [_qk(q, km_ref[rk, qs])]
            values = [v_ref[rk, vs]]
            if has_cache:
                scores.append(_qk(q, kmc_ref[:, qs]))
                values.append(vc_ref[:, vs])
            o_ref[rq, vs] = _softmax_pv(scores, values).astype(o_ref.dtype)
        for kvh in range(GQA_KV_HEADS):
            ks = slice(kvh * HEAD_DIM, (kvh + 1) * HEAD_DIM)
            k = kg_ref[rk, ks].astype(BF16)
            v = vg_ref[rk, ks].astype(BF16)
            if has_cache:
                kc = kgc_ref[:, ks].astype(BF16)
                vc = vgc_ref[:, ks].astype(BF16)
            for g in range(GQA_GROUP):
                hd = kvh * GQA_GROUP + g
                hs = slice(hd * HEAD_DIM, (hd + 1) * HEAD_DIM)
                q = qg_ref[rq, hs]
                scores, values = [_qk(q, k)], [v]
                if has_cache:
                    scores.append(_qk(q, kc))
                    values.append(vc)
                o_ref[rq, base + hd * HEAD_DIM:base + (hd + 1) * HEAD_DIM] = (
                    _softmax_pv(scores, values).astype(o_ref.dtype))


def _attention(qm, qg, km, v, kg, vg, seq, tq, cache=None, seqs=1, side=None):
    m = qm.shape[0]
    nq = seq // tq
    assert seqs == 1 or (cache is None and nq == 1)
    qrow = lambda w_: pl.BlockSpec((seqs * tq, w_), lambda i: (i, 0))
    krow = lambda w_: pl.BlockSpec((seqs * seq, w_), lambda i: (i // nq, 0))
    in_specs = [qrow(qm.shape[1]), qrow(qg.shape[1]), krow(km.shape[1]), krow(v.shape[1]),
                krow(kg.shape[1]), krow(vg.shape[1])]
    args = [qm, qg, km, v, kg, vg]
    if cache is not None:
        past = cache[0].shape[0] // (m // seq)
        crow = lambda w_: pl.BlockSpec((past, w_), lambda i: (i // nq, 0))
        in_specs += [crow(c.shape[1]) for c in cache]
        args += list(cache)
    width = MLA_HEADS * MLA_V + GQA_HEADS * HEAD_DIM
    steps = m // (seqs * tq)
    out_shape = [jax.ShapeDtypeStruct((m, width), BF16)]
    out_specs = [qrow(width)]
    if side is not None:
        cvec, mod_w, mod_b, layer = side
        _, d, n = mod_w.shape
        tn = n // steps
        assert tn * steps == n and tn % LANES == 0
        in_specs += [_const_spec(cvec.shape), pl.BlockSpec((None, d, tn), lambda i: (layer, 0, i)),
                     pl.BlockSpec((None, 1, tn), lambda i: (layer, 0, i))]
        args += [cvec, mod_w, mod_b.reshape(mod_b.shape[0], 1, n)]
        out_shape.append(jax.ShapeDtypeStruct((cvec.shape[0], n), F32))
        out_specs.append(pl.BlockSpec((cvec.shape[0], tn), lambda i: (0, i)))
    res = pl.pallas_call(
        functools.partial(_attn_kernel, has_cache=cache is not None, seqs=seqs, has_side=side is not None),
        out_shape=out_shape,
        grid=(steps,),
        in_specs=in_specs,
        out_specs=out_specs,
        compiler_params=_cparams("parallel"),
        name="attention",
    )(*args)
    return res if side is not None else res[0]


def _odd_proj_kernel(x_hbm, sh_ref, sc_ref, w_ref, o_ref, x_buf, h_ref, x_sem):
    i, j = pl.program_id(0), pl.program_id(1)
    tm = x_buf.shape[0]

    def x_copy(tile):
        return pltpu.make_async_copy(x_hbm.at[pl.ds(tile * tm, tm), :], x_buf, x_sem)

    def project(h):
        return jnp.dot(h, w_ref[...].astype(BF16), preferred_element_type=F32).astype(o_ref.dtype)

    @pl.when(j == 0)
    def _():
        @pl.when(i == 0)
        def _():
            x_copy(0).start()

        x_copy(i).wait()
        h = _modulated(x_buf[...], sh_ref, sc_ref)
        h_ref[...] = h
        o_ref[...] = project(h)

    @pl.when(j > 0)
    def _():
        @pl.when(jnp.logical_and(j == pl.num_programs(1) // 2, i + 1 < pl.num_programs(0)))
        def _():
            x_copy(i + 1).start()

        o_ref[...] = project(h_ref[...])


def _odd_proj(x, mod_tab, latent, w_in, layer):
    m, d = x.shape
    n = w_in.shape[-1]
    tm, tn = (TM_ODD_LAT, TN_ODD_LAT) if latent else (TM_ODD_CTX, TN_ODD_CTX)
    assert n // tn >= 2
    return pl.pallas_call(
        _odd_proj_kernel,
        out_shape=jax.ShapeDtypeStruct((m, n), BF16),
        grid=(m // tm, n // tn),
        in_specs=[pl.BlockSpec(memory_space=pl.ANY)]
        + _mod_specs(1, latent, tm, extra_axes=1)[:2]
        + [pl.BlockSpec((None, d, tn), lambda i, j: (layer, 0, j))],
        out_specs=pl.BlockSpec((tm, tn), lambda i, j: (i, j)),
        scratch_shapes=[pltpu.VMEM((tm, d), F32), pltpu.VMEM((tm, d), BF16), pltpu.SemaphoreType.DMA(())],
        compiler_params=_cparams("arbitrary", "arbitrary"),
        name="odd_proj",
    )(x, mod_tab, mod_tab, w_in)


def _log_sigmoid(x):
    return jnp.minimum(x, 0.0) - jnp.log1p(jnp.exp(-jnp.abs(x)))


def _retention_kernel(*refs, heads, has_init, seqs):
    if has_init:
        logit_ref, q_ref, k_ref, v_ref, g_ref, rg_ref, s0_ref, o_ref, dec_ref, vec_ref = refs
    else:
        logit_ref, q_ref, k_ref, v_ref, g_ref, rg_ref, o_ref, st_ref, dec_ref, vec_ref = refs
    length = q_ref.shape[0] // seqs
    h0 = pl.program_id(0) * heads
    scale = RET_DK ** -0.5
    log_g = _log_sigmoid(logit_ref[...])
    lane_h = lax.broadcasted_iota(jnp.int32, (2, RET_HEADS), 1)

    def head_log_decays(hh):
        lg = jnp.sum(jnp.where(lane_h == h0 + hh, log_g, 0.0), axis=1, keepdims=True)
        return lg[0:1, :], lg[1:2, :]

    @pl.when(pl.program_id(1) == 0)
    def _():
        n_i = lax.broadcasted_iota(jnp.int32, (length, length), 0)
        m_i = lax.broadcasted_iota(jnp.int32, (length, length), 1)
        diff = (n_i - m_i).astype(F32)
        pos = lax.broadcasted_iota(jnp.int32, (length, RET_DK), 0).astype(F32)
        for hh in range(heads):
            lg_f, lg_b = head_log_decays(hh)
            dec_ref[hh] = scale * (jnp.where(diff >= 0, jnp.exp(jnp.maximum(diff, 0.0) * lg_f), 0.0)
                                   + jnp.where(diff <= 0, jnp.exp(jnp.maximum(-diff, 0.0) * lg_b), 0.0))
            if has_init:
                vec_ref[4 * hh + 2] = jnp.exp((pos + 1.0) * lg_f)
                vec_ref[4 * hh + 3] = jnp.exp((length - pos) * lg_b)
            else:
                vec_ref[4 * hh + 0] = jnp.exp((length - 1.0 - pos) * lg_f) * scale
                vec_ref[4 * hh + 1] = jnp.exp(pos * lg_b) * scale

    tn = (((0,), (0,)), ((), ()))
    for sq, hh in [(a, b) for a in range(seqs) for b in range(heads)]:
        cs = slice(hh * RET_DK, (hh + 1) * RET_DK)
        rs = slice(sq * length, (sq + 1) * length)
        q, k, v = q_ref[rs, cs], k_ref[rs, cs], v_ref[rs, cs]
        att = _qk(q, k) * dec_ref[hh]
        o = jnp.dot(att.astype(BF16), v, preferred_element_type=F32)
        if has_init:
            s0_f, s0_b = s0_ref[sq, 0, hh].astype(BF16), s0_ref[sq, 1, hh].astype(BF16)
            o = o + jnp.dot(q, s0_f, preferred_element_type=F32) * vec_ref[4 * hh + 2]
            o = o + jnp.dot(q, s0_b, preferred_element_type=F32) * vec_ref[4 * hh + 3]
        else:
            kf = k.astype(F32)
            k_f, k_b = (kf * vec_ref[4 * hh + 0]).astype(BF16), (kf * vec_ref[4 * hh + 1]).astype(BF16)
            st_ref[sq, 0, hh] = lax.dot_general(k_f, v, tn, preferred_element_type=F32)
            st_ref[sq, 1, hh] = lax.dot_general(k_b, v, tn, preferred_element_type=F32)
        gate = g_ref[rs, cs].astype(F32)
        o_ref[rs, cs] = (_rms_rows(o) * rg_ref[:, cs] * _silu(gate)).astype(o_ref.dtype)


def _retention(u, decay_logit, ret_g, seq, heads, s0=None, seqs=1):
    m = u.shape[0]
    b = m // seq
    hw = heads * RET_DK
    nh = RET_HEADS // heads
    blocks_per_part = RET_HEADS * RET_DK // hw
    col = lambda part: pl.BlockSpec((seqs * seq, hw), lambda j, i, part=part: (i, part * blocks_per_part + j))
    st_spec = pl.BlockSpec((seqs, 2, heads, RET_DK, RET_DV), lambda j, i: (i, 0, j, 0, 0))
    in_specs = [_const_spec(decay_logit.shape), col(0), col(1), col(2), col(3),
                pl.BlockSpec((1, hw), lambda j, i: (0, j))]
    args = [decay_logit, u, u, u, u, ret_g]
    if s0 is not None:
        in_specs.append(st_spec)
        args.append(s0)
    out_shape = [jax.ShapeDtypeStruct((m, RET_HEADS * RET_DV), BF16)]
    out_specs = [pl.BlockSpec((seqs * seq, hw), lambda j, i: (i, j))]
    if s0 is None:
        out_shape.append(jax.ShapeDtypeStruct((b, 2, RET_HEADS, RET_DK, RET_DV), F32))
        out_specs.append(st_spec)
    return pl.pallas_call(
        functools.partial(_retention_kernel, heads=heads, has_init=s0 is not None, seqs=seqs),
        out_shape=out_shape,
        grid=(nh, b // seqs),
        in_specs=in_specs,
        out_specs=out_specs,
        scratch_shapes=[pltpu.VMEM((heads, seq, seq), F32), pltpu.VMEM((4 * heads, seq, RET_DK), F32)],
        compiler_params=_cparams("parallel", "arbitrary"),
        name="retention",
    )(*args)


@functools.lru_cache(maxsize=None)
def _dft_matrices(length):
    n = 2 * length
    f = np.arange(length, dtype=np.float64)[:, None]
    t = np.arange(length, dtype=np.float64)[None, :]
    ang = 2.0 * np.pi * f * t / n
    fwd_a = np.cos(ang)
    fwd_b = np.sin(ang)
    fwd_b[0, :] = np.cos(np.pi * t[0])
    fwd = np.concatenate([fwd_a, fwd_b], axis=0)
    tt = (np.arange(length, dtype=np.float64) + length // 2)[:, None]
    ff = np.arange(length, dtype=np.float64)[None, :]
    ang_i = 2.0 * np.pi * ff * tt / n
    inv_a = 2.0 * np.cos(ang_i) / n
    inv_b = 2.0 * np.sin(ang_i) / n
    inv_a[:, 0] = 1.0 / n
    inv_b[:, 0] = np.cos(np.pi * tt[:, 0]) / n
    inv = np.concatenate([inv_a, inv_b], axis=1)
    return fwd.astype(np.float32), inv.astype(np.float32)


@functools.lru_cache(maxsize=None)
def _filter_features(length):
    t = np.arange(length, dtype=np.float64)
    tn = t / length
    bands = np.arange(1, HY_BANDS + 1, dtype=np.float64)
    ang = 2.0 * np.pi * tn[:, None] * bands[None, :]
    feat = np.concatenate([tn[:, None], np.sin(ang), np.cos(ang)], axis=-1)
    feat = np.pad(feat, ((0, 0), (0, LANES - feat.shape[1])))
    r = (np.abs(t - length // 2) / (length / 2))[:, None]
    return feat.astype(np.float32), r.astype(np.float32)


def _filter_spec_kernel(feat_ref, r_ref, w1_ref, b1_ref, w2_ref, b2_ref, fr_ref, w3_ref, dec_ref, fwd_ref, o_ref,
                        z_ref):
    hp = lax.Precision.HIGHEST

    @pl.when(pl.program_id(0) == 0)
    def _():
        z = jnp.sin(fr_ref[0:1, :] * (jnp.dot(feat_ref[...], w1_ref[...], precision=hp, preferred_element_type=F32)
                                      + b1_ref[...]))
        z_ref[...] = jnp.sin(fr_ref[1:2, :] * (jnp.dot(z, w2_ref[...], precision=hp, preferred_element_type=F32)
                                               + b2_ref[...]))

    filt = jnp.dot(z_ref[...], w3_ref[...], precision=hp, preferred_element_type=F32)
    filt = filt * jnp.exp(-r_ref[...] * jnp.abs(dec_ref[...]))
    spec = jnp.dot(fwd_ref[...], filt.astype(BF16), preferred_element_type=F32)
    length = spec.shape[0] // 2
    sa, sb = spec[:length], spec[length:]
    first = lax.broadcasted_iota(jnp.int32, sa.shape, 0) == 0
    o_ref[0:length, :] = sa
    o_ref[length:2 * length, :] = jnp.where(first, 0.0, sb)
    o_ref[2 * length:, :] = jnp.where(first, sb, sa)


def _filter_spectra(length, w1, b1, w2, b2, freq, w3, decay, fwd_bf16, tc):
    feat, r = _filter_features(length)
    nch = w3.shape[1]
    w1p = jnp.pad(w1, ((0, LANES - w1.shape[0]), (0, 0)))
    return pl.pallas_call(
        _filter_spec_kernel,
        out_shape=jax.ShapeDtypeStruct((3 * length, nch), F32),
        grid=(nch // tc,),
        in_specs=[_const_spec(feat.shape), _const_spec(r.shape), _const_spec(w1p.shape), _const_spec((1, HY_FHID)),
                  _const_spec(w2.shape), _const_spec((1, HY_FHID)), _const_spec(freq.shape),
                  pl.BlockSpec((HY_FHID, tc), lambda j: (0, j)), pl.BlockSpec((1, tc), lambda j: (0, j)),
                  _const_spec(fwd_bf16.shape)],
        out_specs=pl.BlockSpec((3 * length, tc), lambda j: (0, j)),
        scratch_shapes=[pltpu.VMEM((length, HY_FHID), F32)],
        compiler_params=_cparams("arbitrary"),
        name="hyena_filter_spectra",
    )(jnp.asarray(feat), jnp.asarray(r), w1p, b1.reshape(1, -1), w2, b2.reshape(1, -1), freq, w3,
      decay.reshape(1, -1), fwd_bf16)


def _hyena_kernel(v_ref, x1_ref, x2_ref, cwv_ref, cw1_ref, cw2_ref, cbv_ref, cb1_ref, cb2_ref,
                  h1_ref, h2_ref, sk1_ref, sk2_ref, fwd_ref, inv_ref, o_ref, *, sub, seqs):
    length, tc = v_ref.shape[0] // seqs, v_ref.shape[1]
    row = lax.broadcasted_iota(jnp.int32, (length, sub), 0)
    first, last = row == 0, row == length - 1

    for sq, c in [(a, b) for a in range(seqs) for b in range(tc // sub)]:
        cs = slice(c * sub, (c + 1) * sub)
        rs = slice(sq * length, (sq + 1) * length)

        def short_conv(x_ref, w_ref, b_ref):
            x = x_ref[rs, cs].astype(F32)
            prev = jnp.where(first, 0.0, pltpu.roll(x, 1, axis=0))
            nxt = jnp.where(last, 0.0, pltpu.roll(x, length - 1, axis=0))
            return prev * w_ref[0:1, cs] + x * w_ref[1:2, cs] + nxt * w_ref[2:3, cs] + b_ref[:, cs]

        def long_conv(z, h_ref):
            zs = jnp.dot(fwd_ref[...], z.astype(BF16), preferred_element_type=F32)
            za, zb = zs[:length], zs[length:]
            ha, hb, hc = h_ref[0:length, cs], h_ref[length:2 * length, cs], h_ref[2 * length:, cs]
            y = jnp.concatenate([za * ha - zb * hb, za * hb + zb * hc], axis=0).astype(BF16)
            return jnp.dot(inv_ref[...], y, preferred_element_type=F32)

        v = short_conv(v_ref, cwv_ref, cbv_ref)
        x1 = short_conv(x1_ref, cw1_ref, cb1_ref)
        x2 = short_conv(x2_ref, cw2_ref, cb2_ref)
        z = x1 * (long_conv(v, h1_ref) + sk1_ref[:, cs] * v)
        o_ref[rs, cs] = (x2 * (long_conv(z, h2_ref) + sk2_ref[:, cs] * z)).astype(o_ref.dtype)


def _hyena(u, seq, tc, sub, conv_w, conv_b, spectra, skip, fwd_bf16, inv_bf16, seqs=1):
    m = u.shape[0]
    b = m // (seqs * seq)
    nc = HY_CH // tc
    hy0 = 4 * RET_HEADS * RET_DK // tc
    ucol = lambda part: pl.BlockSpec((seqs * seq, tc), lambda i, j, part=part: (i, hy0 + part * nc + j))
    wcol = lambda rows, part: pl.BlockSpec((rows, tc), lambda i, j, part=part: (0, part * nc + j))
    return pl.pallas_call(
        functools.partial(_hyena_kernel, sub=sub, seqs=seqs),
        out_shape=jax.ShapeDtypeStruct((m, HY_CH), BF16),
        grid=(b, nc),
        in_specs=[ucol(0), ucol(1), ucol(2), wcol(3, 0), wcol(3, 1), wcol(3, 2), wcol(1, 0), wcol(1, 1), wcol(1, 2),
                  wcol(3 * seq, 0), wcol(3 * seq, 1), wcol(1, 0), wcol(1, 1),
                  _const_spec(fwd_bf16.shape), _const_spec(inv_bf16.shape)],
        out_specs=pl.BlockSpec((seqs * seq, tc), lambda i, j: (i, j)),
        compiler_params=_cparams("parallel", "parallel"),
        name="hyena",
    )(u, u, u, conv_w, conv_w, conv_w, conv_b, conv_b, conv_b, spectra, spectra, skip, skip, fwd_bf16, inv_bf16)


@functools.lru_cache(maxsize=None)
def _rope_tables(length):
    rows = np.repeat(np.arange(length // GRID_W, dtype=np.float64), GRID_W)
    cols = np.tile(np.arange(GRID_W, dtype=np.float64), length // GRID_W)

    def pack(dim):
        half = dim // 2
        freq = ROPE_THETA ** (-np.arange(0, half, 2, dtype=np.float64) / half)
        ang = np.concatenate([rows[:, None] * freq[None], cols[:, None] * freq[None]], axis=-1)
        pad = ((0, 0), (0, LANES // 2 - half))
        cos = np.pad(np.cos(ang), pad, constant_values=1.0)
        sin = np.pad(np.sin(ang), pad)
        return (np.concatenate([cos, cos], axis=-1).astype(np.float32),
                np.concatenate([-sin, sin], axis=-1).astype(np.float32))

    cos_m, sin_m = pack(MLA_ROPE)
    cos_g, sin_g = pack(HEAD_DIM)
    return cos_m, sin_m, cos_g, sin_g


def _spread_rope(a):
    half = MLA_ROPE // 2
    z = jnp.zeros(a.shape[:-1] + (LANES // 2 - half,), a.dtype)
    return jnp.concatenate([a[..., :half], z, a[..., half:], z], axis=-1)


def _unspread_rope(a):
    half = MLA_ROPE // 2
    return jnp.concatenate([a[..., :half], a[..., LANES // 2:LANES // 2 + half]], axis=-1)


def _even_weights(layer, w_in, q_norm, w_qb, kv_norm, w_kvb, nope_g, rope_g, gqa_g, w_out):
    wq = w_qb.reshape(MLA_Q_RANK, MLA_HEADS, MLA_NOPE + MLA_ROPE)
    wq = jnp.concatenate([wq[..., :MLA_NOPE], _spread_rope(wq[..., MLA_NOPE:])], axis=-1)
    return {
        "layer": layer,
        "w_in": jnp.swapaxes(w_in, 1, 2),
        "w_qb": wq.reshape(MLA_Q_RANK, MLA_HEADS * MLA_QK_PAD).astype(BF16),
        "w_kvb": w_kvb.astype(BF16),
        "w_out": w_out,
        "q_norm": q_norm.reshape(1, -1),
        "kv_norm": kv_norm.reshape(1, -1),
        "nope_g": nope_g,
        "rope_g": _spread_rope(rope_g),
        "gqa_g": gqa_g,
    }


def kernel(x_prompt, x_sample, cache_mla_ckv, cache_mla_krope, cache_gqa_k, cache_gqa_v, state_ret, c, c_ctx, mod_w, mod_b, ffn_w_gate, ffn_w_up, ffn_w_down, ev_w_in, mla_q_norm, mla_w_qb, mla_kv_norm, mla_w_kvb, mla_nope_norm, mla_rope_norm, gqa_qk_norm, ev_w_out, od_w_in, ret_decay_logit, ret_norm, hy_conv_w, hy_conv_b, hy_filt_w1, hy_filt_b1, hy_filt_w2, hy_filt_b2, hy_filt_freq, hy_filt_w3, hy_decay, hy_skip, od_w_out):
    xc = x_prompt.reshape(BATCH * SEQ, D_MODEL)
    xs = x_sample.reshape(DEC_BATCH * DEC_SEQ, D_MODEL)

    cvec = jnp.concatenate([c_ctx[None, :], c, jnp.zeros((8 - 1 - DEC_BATCH, D_MODEL), F32)], axis=0)
    assert DEPTH == 2
    mod_rows = {0: _mod_vectors(cvec, mod_w, mod_b, 1)[0]}

    tables = tuple(jnp.asarray(t) for t in _rope_tables(DEC_SEQ))

    outs = {}
    for l in range(DEPTH):
        mod_tab = mod_rows[l][:1 + DEC_BATCH].reshape((1 + DEC_BATCH) * N_MOD, 1, D_MODEL)
        xc = _ffn(xc, mod_tab, False, 0, ffn_w_gate, ffn_w_up, ffn_w_down, l, 0)
        xs = _ffn(xs, mod_tab, True, 0, ffn_w_gate, ffn_w_up, ffn_w_down, l, 0)
        if l % 2 == 0:
            e = l // 2
            w = _even_weights(e, ev_w_in, mla_q_norm[e], mla_w_qb[e], mla_kv_norm[e], mla_w_kvb[e],
                              mla_nope_norm[e], mla_rope_norm[e], gqa_qk_norm[e], ev_w_out[e])
            qm, ckv, kr, qg, kg, vg = _even_proj(xc, mod_tab, w)
            km, v = _kv_expand(ckv, kr, w["w_kvb"], w["nope_g"], 2 * TM_PROJ)
            att, mod_rows[l + 1] = _attention(qm, qg, km, v, kg, vg, SEQ, SEQ, seqs=SEQS_PER_STEP,
                                              side=(cvec, mod_w, mod_b, l + 1))
            xc = _out_proj(xc, mod_tab, False, [att], w["w_out"])
            outs["ckv"], outs["krope"], outs["k"], outs["v"] = ckv, _unspread_rope(kr), kg, vg
            qm, ckv, kr, qg, kg, vg = _even_proj(xs, mod_tab, w, tables)
            km, v = _kv_expand(ckv, kr, w["w_kvb"], w["nope_g"], TM_PROJ)
            c_ckv = cache_mla_ckv[:, e].reshape(DEC_BATCH * PAST_LEN, MLA_KV_RANK)
            c_kr = _spread_rope(cache_mla_krope[:, e].reshape(DEC_BATCH * PAST_LEN, MLA_ROPE))
            kmc, vc = _kv_expand(c_ckv, c_kr, w["w_kvb"], w["nope_g"], PAST_LEN)
            kgc = cache_gqa_k[:, e].reshape(DEC_BATCH * PAST_LEN, GQA_KV_HEADS * HEAD_DIM)
            vgc = cache_gqa_v[:, e].reshape(DEC_BATCH * PAST_LEN, GQA_KV_HEADS * HEAD_DIM)
            att = _attention(qm, qg, km, v, kg, vg, DEC_SEQ, TQ_ATT, cache=(kmc, vc, kgc, vgc))
            xs = _out_proj(xs, mod_tab, True, [att], w["w_out"])
        else:
            o = l // 2
            w_out = od_w_out[o]
            logit = ret_decay_logit[o]
            ret_g = ret_norm[o].reshape(1, -1)
            skip = hy_skip[o].reshape(1, -1)
            conv_b = hy_conv_b[o].reshape(1, -1)
            filt_args = (hy_filt_w1[o], hy_filt_b1[o], hy_filt_w2[o], hy_filt_b2[o], hy_filt_freq[o],
                         hy_filt_w3[o], hy_decay[o])
            for stream in ("ctx", "smp"):
                x, seq, latent = (xc, SEQ, False) if stream == "ctx" else (xs, DEC_SEQ, True)
                fwd, inv = _dft_matrices(seq)
                fwd_b, inv_b = jnp.asarray(fwd).astype(BF16), jnp.asarray(inv).astype(BF16)
                spectra = _filter_spectra(seq, *filt_args, fwd_b, TC_HY)
                u = _odd_proj(x, mod_tab, latent, od_w_in, o)
                if stream == "ctx":
                    o_ret, st = _retention(u, logit, ret_g, seq, RET_HEADS, seqs=SEQS_PER_STEP)
                    outs["ret"] = st
                    o_hy = _hyena(u, seq, HY_CH, HY_CH, hy_conv_w[o], conv_b, spectra, skip, fwd_b, inv_b, seqs=SEQS_PER_STEP)
                else:
                    (o_ret,) = _retention(u, logit, ret_g, seq, RET_HEADS_LAT, s0=state_ret[:, o])
                    o_hy = _hyena(u, seq, TC_HY, HY_SUB_LAT, hy_conv_w[o], conv_b, spectra, skip, fwd_b, inv_b)
                x = _out_proj(x, mod_tab, latent, [o_ret, o_hy], w_out)
                if stream == "ctx":
                    xc = x
                else:
                    xs = x
        xc = _ffn(xc, mod_tab, False, 2, ffn_w_gate, ffn_w_up, ffn_w_down, l, 1)
        xs = _ffn(xs, mod_tab, True, 2, ffn_w_gate, ffn_w_up, ffn_w_down, l, 1)

    return (
        xc.reshape(BATCH, SEQ, D_MODEL),
        xs.reshape(DEC_BATCH, DEC_SEQ, D_MODEL),
        outs["ckv"].reshape(BATCH, 1, SEQ, MLA_KV_RANK),
        outs["krope"].reshape(BATCH, 1, SEQ, MLA_ROPE),
        outs["k"].reshape(BATCH, 1, SEQ, GQA_KV_HEADS, HEAD_DIM),
        outs["v"].reshape(BATCH, 1, SEQ, GQA_KV_HEADS, HEAD_DIM),
        outs["ret"].reshape(BATCH, 1, 2, RET_HEADS, RET_DK, RET_DV),
    )
```

```python
import functools
import math

import numpy as np
import jax
import jax.numpy as jnp
from jax import lax
from jax.experimental import pallas as pl
from jax.experimental.pallas import tpu as pltpu

F32 = jnp.float32
BF16 = jnp.bfloat16

D_MODEL = 2048
BATCH = 32
SEQ = 256
DEPTH = 2
DEC_BATCH = 2
DEC_SEQ = 1024
PAST_LEN = 256
GRID_W = 64
N_MOD = 9
D_FF = 5632
ROPE_THETA = 10000.0
EPS = 1e-6

MLA_HEADS = 8
MLA_Q_RANK = 512
MLA_KV_RANK = 512
MLA_NOPE = 128
MLA_ROPE = 64
MLA_V = 128
GQA_HEADS = 8
GQA_KV_HEADS = 2
GQA_GROUP = GQA_HEADS // GQA_KV_HEADS
HEAD_DIM = 128
MLA_SCALE = (MLA_NOPE + MLA_ROPE) ** -0.5
GQA_SCALE = HEAD_DIM ** -0.5
MLA_QK_PAD = 256

RET_HEADS = 8
RET_DK = 128
RET_DV = 128
HY_CH = 1024
HY_ORDER = 2
HY_BANDS = 16
HY_FHID = 64
ODD_IN = 4 * RET_HEADS * RET_DK + (HY_ORDER + 1) * HY_CH

V7X_VMEM_BYTES = 64 * 1024 * 1024
LANES = 128
VMEM_LIMIT = V7X_VMEM_BYTES - 3 * 1024 * 1024

TM_FFN = 1024
TF_FFN = 512
TF_SUB = 256
TM_PROJ = 512
TM_ODD_CTX, TN_ODD_CTX = 2048, 512
TM_ODD_LAT, TN_ODD_LAT = 1024, 1024
TN_MOD = 1024
TQ_ATT = 256
TC_HY = 512
HY_SUB_LAT = 256
RET_HEADS_LAT = 2
SEQS_PER_STEP = 2


def _cparams(*sem):
    return pltpu.CompilerParams(dimension_semantics=sem, vmem_limit_bytes=VMEM_LIMIT)


def _const_spec(shape):
    nd = len(shape)
    return pl.BlockSpec(shape, lambda *_: (0,) * nd)


def _silu(x):
    return x * jax.nn.sigmoid(x)


def _rms_rows(x, width=None):
    ss = jnp.sum(x * x, axis=-1, keepdims=True)
    n = x.shape[-1] if width is None else width
    return x * lax.rsqrt(ss * (1.0 / n) + EPS)


def _mod_tile(c_ref, w_ref, b_ref):
    s = _silu(c_ref[...]).astype(BF16)
    return jnp.dot(s, w_ref[...].astype(BF16), preferred_element_type=F32) + b_ref[...]


def _mod_kernel(c_ref, w_ref, b_ref, o_ref):
    o_ref[0] = _mod_tile(c_ref, w_ref.at[0], b_ref.at[0])


def _mod_vectors(cvec, mod_w, mod_b, depth):
    _, d, n = mod_w.shape
    rows = cvec.shape[0]
    return pl.pallas_call(
        _mod_kernel,
        out_shape=jax.ShapeDtypeStruct((depth, rows, n), F32),
        grid=(depth, n // TN_MOD),
        in_specs=[
            pl.BlockSpec((rows, d), lambda l, j: (0, 0)),
            pl.BlockSpec((1, d, TN_MOD), lambda l, j: (l, 0, j)),
            pl.BlockSpec((1, 1, TN_MOD), lambda l, j: (l, 0, j)),
        ],
        out_specs=pl.BlockSpec((1, rows, TN_MOD), lambda l, j: (l, 0, j)),
        compiler_params=_cparams("parallel", "parallel"),
        name="mod_vectors",
    )(cvec, mod_w, mod_b.reshape(mod_b.shape[0], 1, n))


def _mod_specs(sub, latent, tile, extra_axes=0):
    group_of_tile = (lambda i: 1 + (i * tile) // DEC_SEQ) if latent else (lambda i: 0)

    def spec(k):
        if extra_axes:
            return pl.BlockSpec((1, 1, D_MODEL), lambda i, j: (group_of_tile(i) * N_MOD + 3 * sub + k, 0, 0))
        return pl.BlockSpec((1, 1, D_MODEL), lambda i: (group_of_tile(i) * N_MOD + 3 * sub + k, 0, 0))
    return [spec(0), spec(1), spec(2)]


def _modulated(x, sh_ref, sc_ref):
    return (_rms_rows(x) * (1.0 + sc_ref[0]) + sh_ref[0]).astype(BF16)


def _ffn_kernel(x_hbm, sh_ref, sc_ref, gt_ref, wg_ref, wu_ref, wd_ref, o_ref, x_buf, h_ref, x_sem):
    i, f = pl.program_id(0), pl.program_id(1)
    tm = x_buf.shape[0]
    half_gate = 0.5 * gt_ref[0]

    def x_copy(tile):
        return pltpu.make_async_copy(x_hbm.at[pl.ds(tile * tm, tm), :], x_buf, x_sem)

    def partial_out(h):
        out = None
        for c in range(TF_FFN // TF_SUB):
            cs = slice(c * TF_SUB, (c + 1) * TF_SUB)
            g = jnp.dot(h, wg_ref[:, cs].astype(BF16), preferred_element_type=F32)
            u = jnp.dot(h, wu_ref[:, cs].astype(BF16), preferred_element_type=F32)
            a = (_silu(g) * u).astype(BF16)
            t = jnp.dot(a, wd_ref[cs, :].astype(BF16), preferred_element_type=F32)
            out = t if out is None else out + t
        return half_gate * out

    @pl.when(f == 0)
    def _():
        @pl.when(i == 0)
        def _():
            x_copy(0).start()

        x_copy(i).wait()
        x = x_buf[...]
        h = _modulated(x, sh_ref, sc_ref)
        h_ref[...] = h
        o_ref[...] = x + partial_out(h)

    @pl.when(f > 0)
    def _():
        @pl.when(jnp.logical_and(f == pl.num_programs(1) // 2, i + 1 < pl.num_programs(0)))
        def _():
            x_copy(i + 1).start()

        o_ref[...] += partial_out(h_ref[...])


def _ffn(x, mod_tab, latent, sub, w_gate, w_up, w_down, layer, j):
    m, d = x.shape
    dff = w_gate.shape[-1]
    assert dff // TF_FFN >= 2
    return pl.pallas_call(
        _ffn_kernel,
        out_shape=jax.ShapeDtypeStruct((m, d), F32),
        grid=(m // TM_FFN, dff // TF_FFN),
        in_specs=[pl.BlockSpec(memory_space=pl.ANY)]
        + _mod_specs(sub, latent, TM_FFN, extra_axes=1)
        + [
            pl.BlockSpec((None, None, d, TF_FFN), lambda i, f: (layer, j, 0, f)),
            pl.BlockSpec((None, None, d, TF_FFN), lambda i, f: (layer, j, 0, f)),
            pl.BlockSpec((None, None, TF_FFN, d), lambda i, f: (layer, j, f, 0)),
        ],
        out_specs=pl.BlockSpec((TM_FFN, d), lambda i, f: (i, 0)),
        scratch_shapes=[pltpu.VMEM((TM_FFN, d), F32), pltpu.VMEM((TM_FFN, d), BF16), pltpu.SemaphoreType.DMA(())],
        compiler_params=_cparams("arbitrary", "arbitrary"),
        name="ffn",
    )(x, mod_tab, mod_tab, mod_tab, w_gate, w_up, w_down)


def _out_proj_kernel(*refs, n_parts):
    x_ref, gt_ref, w_ref = refs[0], refs[1], refs[2]
    part_refs = refs[3:3 + n_parts]
    o_ref, wb_ref = refs[3 + n_parts], refs[4 + n_parts]

    @pl.when(pl.program_id(0) == 0)
    def _():
        wb_ref[...] = w_ref[...].astype(BF16)

    acc, off = None, 0
    for p_ref in part_refs:
        width = p_ref.shape[1]
        t = jnp.dot(p_ref[...], wb_ref[off:off + width, :], preferred_element_type=F32)
        acc = t if acc is None else acc + t
        off += width
    o_ref[...] = x_ref[...] + gt_ref[0] * acc


def _out_proj(x, mod_tab, latent, parts, w_out):
    m, d = x.shape
    return pl.pallas_call(
        functools.partial(_out_proj_kernel, n_parts=len(parts)),
        out_shape=jax.ShapeDtypeStruct((m, d), F32),
        grid=(m // TM_PROJ,),
        in_specs=[pl.BlockSpec((TM_PROJ, d), lambda i: (i, 0)), _mod_specs(1, latent, TM_PROJ)[2],
                  pl.BlockSpec(w_out.shape, lambda i: (0, 0), pipeline_mode=pl.Buffered(1))]
        + [pl.BlockSpec((TM_PROJ, p.shape[1]), lambda i: (i, 0)) for p in parts],
        out_specs=pl.BlockSpec((TM_PROJ, d), lambda i: (i, 0)),
        scratch_shapes=[pltpu.VMEM(w_out.shape, BF16)],
        compiler_params=_cparams("arbitrary"),
        name="mixer_out_proj",
    )(x, mod_tab, w_out, *parts)


EV_QA, EV_KVA, EV_QG, EV_KG, EV_VG, EV_KR, EV_END = 0, 512, 1024, 2048, 2304, 2560, 2688


def _rotate_half(x, cos, sin):
    return x * cos + pltpu.roll(x, LANES // 2, axis=1) * sin


def _even_proj_kernel(*refs, rotary):
    (x_ref, sh_ref, sc_ref, w_in_ref, w_qb_ref, qn_ref, kvn_ref, ng_ref, rg_ref, gg_ref) = refs[:10]
    if rotary:
        cm_ref, sm_ref, cg_ref, sg_ref = refs[10:14]
        rot_m = lambda t: _rotate_half(t, cm_ref[...], sm_ref[...])
        rot_g = lambda t: _rotate_half(t, cg_ref[...], sg_ref[...])
    else:
        rot_m = rot_g = lambda t: t
    qm_ref, ckv_ref, kr_ref, qg_ref, kg_ref, vg_ref, wb_ref = refs[-7:]

    @pl.when(pl.program_id(0) == 0)
    def _():
        chunk = 128
        half = MLA_ROPE // 2

        def copy_rows(dst0, src0, n_chunks):
            def body(c, carry):
                dst = pl.ds(pl.multiple_of(dst0 + c * chunk, 16), chunk)
                src = pl.ds(pl.multiple_of(src0 + c * chunk, 8), chunk)
                wb_ref[dst, :] = w_in_ref[src, :].astype(BF16)
                return carry
            lax.fori_loop(0, n_chunks, body, 0)

        copy_rows(EV_QA, 0, EV_QG // chunk)
        copy_rows(EV_QG, EV_QG + MLA_ROPE, (EV_KR - EV_QG) // chunk)
        wb_ref[EV_KR:EV_END, :] = jnp.zeros((EV_END - EV_KR, wb_ref.shape[1]), BF16)
        wb_ref[EV_KR:EV_KR + half, :] = w_in_ref[EV_QG:EV_QG + half, :].astype(BF16)
        wb_ref[EV_KR + LANES // 2:EV_KR + LANES // 2 + half, :] = w_in_ref[EV_QG + half:EV_QG + MLA_ROPE, :].astype(BF16)

    h = _modulated(x_ref[...], sh_ref, sc_ref)
    u = _qk(h, wb_ref[...])

    nope_gain = ng_ref[0:1, :] * MLA_SCALE
    rope_gain = rg_ref[0:1, :] * MLA_SCALE
    gqa_gain = gg_ref[0:1, :] * GQA_SCALE

    qa = (_rms_rows(u[:, EV_QA:EV_KVA]) * qn_ref[...]).astype(BF16)
    q = jnp.dot(qa, w_qb_ref[...], preferred_element_type=F32)
    for hd in range(MLA_HEADS):
        c0 = hd * MLA_QK_PAD
        nope = _rms_rows(q[:, c0:c0 + MLA_NOPE]) * nope_gain
        rope = _rms_rows(q[:, c0 + MLA_NOPE:c0 + MLA_QK_PAD], MLA_ROPE) * rope_gain
        qm_ref[:, c0:c0 + MLA_NOPE] = nope.astype(BF16)
        qm_ref[:, c0 + MLA_NOPE:c0 + MLA_QK_PAD] = rot_m(rope).astype(BF16)

    ckv_ref[...] = _rms_rows(u[:, EV_KVA:EV_QG]) * kvn_ref[...]
    kr_ref[...] = rot_m(_rms_rows(u[:, EV_KR:EV_END], MLA_ROPE) * rg_ref[1:2, :])

    for hd in range(GQA_HEADS):
        c0 = EV_QG + hd * HEAD_DIM
        qh = _rms_rows(u[:, c0:c0 + HEAD_DIM]) * gqa_gain
        qg_ref[:, hd * HEAD_DIM:(hd + 1) * HEAD_DIM] = rot_g(qh).astype(BF16)
    for hd in range(GQA_KV_HEADS):
        c0 = EV_KG + hd * HEAD_DIM
        kh = _rms_rows(u[:, c0:c0 + HEAD_DIM]) * gg_ref[1:2, :]
        kg_ref[pl.ds(hd, kh.shape[0], stride=GQA_KV_HEADS), :] = rot_g(kh)
        vg_ref[pl.ds(hd, kh.shape[0], stride=GQA_KV_HEADS), :] = u[:, EV_VG + hd * HEAD_DIM:EV_VG + (hd + 1) * HEAD_DIM]


def _even_proj(x, mod_tab, w, tables=None):
    m, d = x.shape
    tm = TM_PROJ
    row = lambda w_: pl.BlockSpec((tm, w_), lambda i: (i, 0))
    tab_specs, tab_args = [], []
    if tables is not None:
        tiles_per_seq = tables[0].shape[0] // tm
        tab_specs = [pl.BlockSpec((tm, LANES), lambda i: (i % tiles_per_seq, 0))] * 4
        tab_args = list(tables)
    return pl.pallas_call(
        functools.partial(_even_proj_kernel, rotary=tables is not None),
        out_shape=[
            jax.ShapeDtypeStruct((m, MLA_HEADS * MLA_QK_PAD), BF16),
            jax.ShapeDtypeStruct((m, MLA_KV_RANK), F32),
            jax.ShapeDtypeStruct((m, LANES), F32),
            jax.ShapeDtypeStruct((m, GQA_HEADS * HEAD_DIM), BF16),
            jax.ShapeDtypeStruct((m * GQA_KV_HEADS, HEAD_DIM), F32),
            jax.ShapeDtypeStruct((m * GQA_KV_HEADS, HEAD_DIM), F32),
        ],
        grid=(m // tm,),
        in_specs=[row(d)] + _mod_specs(1, tables is not None, tm)[:2]
        + [pl.BlockSpec((None,) + w["w_in"].shape[1:], lambda i: (w["layer"], 0, 0)),
           _const_spec(w["w_qb"].shape), _const_spec(w["q_norm"].shape),
           _const_spec(w["kv_norm"].shape), _const_spec(w["nope_g"].shape), _const_spec(w["rope_g"].shape),
           _const_spec(w["gqa_g"].shape)] + tab_specs,
        out_specs=[row(MLA_HEADS * MLA_QK_PAD), row(MLA_KV_RANK), row(LANES), row(GQA_HEADS * HEAD_DIM),
                   pl.BlockSpec((tm * GQA_KV_HEADS, HEAD_DIM), lambda i: (i, 0)),
                   pl.BlockSpec((tm * GQA_KV_HEADS, HEAD_DIM), lambda i: (i, 0))],
        scratch_shapes=[pltpu.VMEM((EV_END, d), BF16)],
        compiler_params=_cparams("arbitrary"),
        name="even_proj",
    )(x, mod_tab, mod_tab, w["w_in"], w["w_qb"], w["q_norm"], w["kv_norm"], w["nope_g"], w["rope_g"], w["gqa_g"],
      *tab_args)


def _kv_expand_kernel(ckv_ref, kr_ref, w_ref, ng_ref, km_ref, v_ref):
    kv = jnp.dot(ckv_ref[...].astype(BF16), w_ref[...], preferred_element_type=F32)
    kr = kr_ref[...].astype(BF16)
    for hd in range(MLA_HEADS):
        c0 = hd * (MLA_NOPE + MLA_V)
        km_ref[:, hd * MLA_QK_PAD:hd * MLA_QK_PAD + MLA_NOPE] = (
            _rms_rows(kv[:, c0:c0 + MLA_NOPE]) * ng_ref[1:2, :]).astype(BF16)
        km_ref[:, hd * MLA_QK_PAD + MLA_NOPE:(hd + 1) * MLA_QK_PAD] = kr
        v_ref[:, hd * MLA_V:(hd + 1) * MLA_V] = kv[:, c0 + MLA_NOPE:c0 + MLA_NOPE + MLA_V].astype(BF16)


def _kv_expand(ckv, kr, w_kvb_bf16, nope_g, tr):
    r = ckv.shape[0]
    row = lambda w_: pl.BlockSpec((tr, w_), lambda i: (i, 0))
    return pl.pallas_call(
        _kv_expand_kernel,
        out_shape=[jax.ShapeDtypeStruct((r, MLA_HEADS * MLA_QK_PAD), BF16),
                   jax.ShapeDtypeStruct((r, MLA_HEADS * MLA_V), BF16)],
        grid=(r // tr,),
        in_specs=[row(MLA_KV_RANK), row(LANES), _const_spec(w_kvb_bf16.shape), _const_spec(nope_g.shape)],
        out_specs=[row(MLA_HEADS * MLA_QK_PAD), row(MLA_HEADS * MLA_V)],
        compiler_params=_cparams("parallel"),
        name="kv_expand",
    )(ckv, kr, w_kvb_bf16, nope_g)


def _softmax_pv(score_list, value_list):
    mx = None
    for s in score_list:
        m_ = jnp.max(s, axis=-1, keepdims=True)
        mx = m_ if mx is None else jnp.maximum(mx, m_)
    den, acc = None, None
    for s, v in zip(score_list, value_list):
        p = jnp.exp(s - mx)
        l_ = jnp.sum(p, axis=-1, keepdims=True)
        o_ = jnp.dot(p.astype(BF16), v, preferred_element_type=F32)
        den = l_ if den is None else den + l_
        acc = o_ if acc is None else acc + o_
    return acc / den


def _qk(q, k):
    return lax.dot_general(q, k, (((1,), (1,)), ((), ())), preferred_element_type=F32)


def _attn_kernel(*refs, has_cache, seqs, has_side):
    if has_side:
        c_ref, mw_ref, mb_ref, mo_ref = refs[-5], refs[-4], refs[-3], refs[-1]
        mo_ref[...] = _mod_tile(c_ref, mw_ref, mb_ref)
        refs = refs[:-5] + (refs[-2],)
    if has_cache:
        qm_ref, qg_ref, km_ref, v_ref, kg_ref, vg_ref, kmc_ref, vc_ref, kgc_ref, vgc_ref, o_ref = refs
    else:
        qm_ref, qg_ref, km_ref, v_ref, kg_ref, vg_ref, o_ref = refs
    rows_q, rows_k = qm_ref.shape[0] // seqs, km_ref.shape[0] // seqs
    base = MLA_HEADS * MLA_V
    for sq in range(seqs):
        rq = slice(sq * rows_q, (sq + 1) * rows_q)
        rk = slice(sq * rows_k, (sq + 1) * rows_k)
        for hd in range(MLA_HEADS):
            qs = slice(hd * MLA_QK_PAD, (hd + 1) * MLA_QK_PAD)
            vs = slice(hd * MLA_V, (hd + 1) * MLA_V)
            q = qm_ref[rq, qs]
            scores = [_qk(q, km_ref[rk, qs])]
            values = [v_ref[rk, vs]]
            if has_cache:
                scores.append(_qk(q, kmc_ref[:, qs]))
                values.append(vc_ref[:, vs])
            o_ref[rq, vs] = _softmax_pv(scores, values).astype(o_ref.dtype)
        for kvh in range(GQA_KV_HEADS):
            own = pl.ds(sq * rows_k * GQA_KV_HEADS + kvh, rows_k, stride=GQA_KV_HEADS)
            k = kg_ref[own, :].astype(BF16)
            v = vg_ref[own, :].astype(BF16)
            if has_cache:
                past = pl.ds(kvh, kgc_ref.shape[0] // GQA_KV_HEADS, stride=GQA_KV_HEADS)
                kc = kgc_ref[past, :].astype(BF16)
                vc = vgc_ref[past, :].astype(BF16)
            for g in range(GQA_GROUP):
                hd = kvh * GQA_GROUP + g
                hs = slice(hd * HEAD_DIM, (hd + 1) * HEAD_DIM)
                q = qg_ref[rq, hs]
                scores, values = [_qk(q, k)], [v]
                if has_cache:
                    scores.append(_qk(q, kc))
                    values.append(vc)
                o_ref[rq, base + hd * HEAD_DIM:base + (hd + 1) * HEAD_DIM] = (
                    _softmax_pv(scores, values).astype(o_ref.dtype))


def _attention(qm, qg, km, v, kg, vg, seq, tq, cache=None, seqs=1, side=None):
    m = qm.shape[0]
    nq = seq // tq
    assert seqs == 1 or (cache is None and nq == 1)
    qrow = lambda w_: pl.BlockSpec((seqs * tq, w_), lambda i: (i, 0))
    krow = lambda w_: pl.BlockSpec((seqs * seq, w_), lambda i: (i // nq, 0))
    grow = lambda rows: pl.BlockSpec((rows * GQA_KV_HEADS, HEAD_DIM), lambda i: (i // nq, 0))
    in_specs = [qrow(qm.shape[1]), qrow(qg.shape[1]), krow(km.shape[1]), krow(v.shape[1]),
                grow(seqs * seq), grow(seqs * seq)]
    args = [qm, qg, km, v, kg, vg]
    if cache is not None:
        past = cache[0].shape[0] // (m // seq)
        crow = lambda w_: pl.BlockSpec((past, w_), lambda i: (i // nq, 0))
        in_specs += [crow(cache[0].shape[1]), crow(cache[1].shape[1]), grow(past), grow(past)]
        args += list(cache)
    width = MLA_HEADS * MLA_V + GQA_HEADS * HEAD_DIM
    steps = m // (seqs * tq)
    out_shape = [jax.ShapeDtypeStruct((m, width), BF16)]
    out_specs = [qrow(width)]
    if side is not None:
        cvec, mod_w, mod_b, layer = side
        _, d, n = mod_w.shape
        tn = n // steps
        assert tn * steps == n and tn % LANES == 0
        in_specs += [_const_spec(cvec.shape), pl.BlockSpec((None, d, tn), lambda i: (layer, 0, i)),
                     pl.BlockSpec((None, 1, tn), lambda i: (layer, 0, i))]
        args += [cvec, mod_w, mod_b.reshape(mod_b.shape[0], 1, n)]
        out_shape.append(jax.ShapeDtypeStruct((cvec.shape[0], n), F32))
        out_specs.append(pl.BlockSpec((cvec.shape[0], tn), lambda i: (0, i)))
    res = pl.pallas_call(
        functools.partial(_attn_kernel, has_cache=cache is not None, seqs=seqs, has_side=side is not None),
        out_shape=out_shape,
        grid=(steps,),
        in_specs=in_specs,
        out_specs=out_specs,
        compiler_params=_cparams("parallel"),
        name="attention",
    )(*args)
    return res if side is not None else res[0]


def _odd_proj_kernel(x_hbm, sh_ref, sc_ref, w_ref, o_ref, x_buf, h_ref, x_sem):
    i, j = pl.program_id(0), pl.program_id(1)
    tm = x_buf.shape[0]

    def x_copy(tile):
        return pltpu.make_async_copy(x_hbm.at[pl.ds(tile * tm, tm), :], x_buf, x_sem)

    def project(h):
        return jnp.dot(h, w_ref[...].astype(BF16), preferred_element_type=F32).astype(o_ref.dtype)

    @pl.when(j == 0)
    def _():
        @pl.when(i == 0)
        def _():
            x_copy(0).start()

        x_copy(i).wait()
        h = _modulated(x_buf[...], sh_ref, sc_ref)
        h_ref[...] = h
        o_ref[...] = project(h)

    @pl.when(j > 0)
    def _():
        @pl.when(jnp.logical_and(j == pl.num_programs(1) // 2, i + 1 < pl.num_programs(0)))
        def _():
            x_copy(i + 1).start()

        o_ref[...] = project(h_ref[...])


def _odd_proj(x, mod_tab, latent, w_in, layer):
    m, d = x.shape
    n = w_in.shape[-1]
    tm, tn = (TM_ODD_LAT, TN_ODD_LAT) if latent else (TM_ODD_CTX, TN_ODD_CTX)
    assert n // tn >= 2
    return pl.pallas_call(
        _odd_proj_kernel,
        out_shape=jax.ShapeDtypeStruct((m, n), BF16),
        grid=(m // tm, n // tn),
        in_specs=[pl.BlockSpec(memory_space=pl.ANY)]
        + _mod_specs(1, latent, tm, extra_axes=1)[:2]
        + [pl.BlockSpec((None, d, tn), lambda i, j: (layer, 0, j))],
        out_specs=pl.BlockSpec((tm, tn), lambda i, j: (i, j)),
        scratch_shapes=[pltpu.VMEM((tm, d), F32), pltpu.VMEM((tm, d), BF16), pltpu.SemaphoreType.DMA(())],
        compiler_params=_cparams("arbitrary", "arbitrary"),
        name="odd_proj",
    )(x, mod_tab, mod_tab, w_in)


def _log_sigmoid(x):
    return jnp.minimum(x, 0.0) - jnp.log1p(jnp.exp(-jnp.abs(x)))


def _retention_kernel(*refs, heads, has_init, seqs):
    if has_init:
        logit_ref, q_ref, k_ref, v_ref, g_ref, rg_ref, s0_ref, o_ref, dec_ref, vec_ref = refs
    else:
        logit_ref, q_ref, k_ref, v_ref, g_ref, rg_ref, o_ref, st_ref, dec_ref, vec_ref = refs
    length = q_ref.shape[0] // seqs
    h0 = pl.program_id(0) * heads
    scale = RET_DK ** -0.5
    log_g = _log_sigmoid(logit_ref[...])
    lane_h = lax.broadcasted_iota(jnp.int32, (2, RET_HEADS), 1)

    def head_log_decays(hh):
        lg = jnp.sum(jnp.where(lane_h == h0 + hh, log_g, 0.0), axis=1, keepdims=True)
        return lg[0:1, :], lg[1:2, :]

    @pl.when(pl.program_id(1) == 0)
    def _():
        n_i = lax.broadcasted_iota(jnp.int32, (length, length), 0)
        m_i = lax.broadcasted_iota(jnp.int32, (length, length), 1)
        diff = (n_i - m_i).astype(F32)
        pos = lax.broadcasted_iota(jnp.int32, (length, RET_DK), 0).astype(F32)
        for hh in range(heads):
            lg_f, lg_b = head_log_decays(hh)
            dec_ref[hh] = scale * (jnp.where(diff >= 0, jnp.exp(jnp.maximum(diff, 0.0) * lg_f), 0.0)
                                   + jnp.where(diff <= 0, jnp.exp(jnp.maximum(-diff, 0.0) * lg_b), 0.0))
            if has_init:
                vec_ref[4 * hh + 2] = jnp.exp((pos + 1.0) * lg_f)
                vec_ref[4 * hh + 3] = jnp.exp((length - pos) * lg_b)
            else:
                vec_ref[4 * hh + 0] = jnp.exp((length - 1.0 - pos) * lg_f) * scale
                vec_ref[4 * hh + 1] = jnp.exp(pos * lg_b) * scale

    tn = (((0,), (0,)), ((), ()))
    for sq, hh in [(a, b) for a in range(seqs) for b in range(heads)]:
        cs = slice(hh * RET_DK, (hh + 1) * RET_DK)
        rs = slice(sq * length, (sq + 1) * length)
        q, k, v = q_ref[rs, cs], k_ref[rs, cs], v_ref[rs, cs]
        att = _qk(q, k) * dec_ref[hh]
        o = jnp.dot(att.astype(BF16), v, preferred_element_type=F32)
        if has_init:
            s0_f, s0_b = s0_ref[sq, 0, hh].astype(BF16), s0_ref[sq, 1, hh].astype(BF16)
            o = o + jnp.dot(q, s0_f, preferred_element_type=F32) * vec_ref[4 * hh + 2]
            o = o + jnp.dot(q, s0_b, preferred_element_type=F32) * vec_ref[4 * hh + 3]
        else:
            kf = k.astype(F32)
            k_f, k_b = (kf * vec_ref[4 * hh + 0]).astype(BF16), (kf * vec_ref[4 * hh + 1]).astype(BF16)
            st_ref[sq, 0, hh] = lax.dot_general(k_f, v, tn, preferred_element_type=F32)
            st_ref[sq, 1, hh] = lax.dot_general(k_b, v, tn, preferred_element_type=F32)
        gate = g_ref[rs, cs].astype(F32)
        o_ref[rs, cs] = (_rms_rows(o) * rg_ref[:, cs] * _silu(gate)).astype(o_ref.dtype)


def _retention(u, decay_logit, ret_g, seq, heads, s0=None, seqs=1):
    m = u.shape[0]
    b = m // seq
    hw = heads * RET_DK
    nh = RET_HEADS // heads
    blocks_per_part = RET_HEADS * RET_DK // hw
    col = lambda part: pl.BlockSpec((seqs * seq, hw), lambda j, i, part=part: (i, part * blocks_per_part + j))
    st_spec = pl.BlockSpec((seqs, 2, heads, RET_DK, RET_DV), lambda j, i: (i, 0, j, 0, 0))
    in_specs = [_const_spec(decay_logit.shape), col(0), col(1), col(2), col(3),
                pl.BlockSpec((1, hw), lambda j, i: (0, j))]
    args = [decay_logit, u, u, u, u, ret_g]
    if s0 is not None:
        in_specs.append(st_spec)
        args.append(s0)
    out_shape = [jax.ShapeDtypeStruct((m, RET_HEADS * RET_DV), BF16)]
    out_specs = [pl.BlockSpec((seqs * seq, hw), lambda j, i: (i, j))]
    if s0 is None:
        out_shape.append(jax.ShapeDtypeStruct((b, 2, RET_HEADS, RET_DK, RET_DV), F32))
        out_specs.append(st_spec)
    return pl.pallas_call(
        functools.partial(_retention_kernel, heads=heads, has_init=s0 is not None, seqs=seqs),
        out_shape=out_shape,
        grid=(nh, b // seqs),
        in_specs=in_specs,
        out_specs=out_specs,
        scratch_shapes=[pltpu.VMEM((heads, seq, seq), F32), pltpu.VMEM((4 * heads, seq, RET_DK), F32)],
        compiler_params=_cparams("parallel", "arbitrary"),
        name="retention",
    )(*args)


@functools.lru_cache(maxsize=None)
def _dft_matrices(length):
    n = 2 * length
    f = np.arange(length, dtype=np.float64)[:, None]
    t = np.arange(length, dtype=np.float64)[None, :]
    ang = 2.0 * np.pi * f * t / n
    fwd_a = np.cos(ang)
    fwd_b = np.sin(ang)
    fwd_b[0, :] = np.cos(np.pi * t[0])
    fwd = np.concatenate([fwd_a, fwd_b], axis=0)
    tt = (np.arange(length, dtype=np.float64) + length // 2)[:, None]
    ff = np.arange(length, dtype=np.float64)[None, :]
    ang_i = 2.0 * np.pi * ff * tt / n
    inv_a = 2.0 * np.cos(ang_i) / n
    inv_b = 2.0 * np.sin(ang_i) / n
    inv_a[:, 0] = 1.0 / n
    inv_b[:, 0] = np.cos(np.pi * tt[:, 0]) / n
    inv = np.concatenate([inv_a, inv_b], axis=1)
    return fwd.astype(np.float32), inv.astype(np.float32)


@functools.lru_cache(maxsize=None)
def _filter_features(length):
    t = np.arange(length, dtype=np.float64)
    tn = t / length
    bands = np.arange(1, HY_BANDS + 1, dtype=np.float64)
    ang = 2.0 * np.pi * tn[:, None] * bands[None, :]
    feat = np.concatenate([tn[:, None], np.sin(ang), np.cos(ang)], axis=-1)
    feat = np.pad(feat, ((0, 0), (0, LANES - feat.shape[1])))
    r = (np.abs(t - length // 2) / (length / 2))[:, None]
    return feat.astype(np.float32), r.astype(np.float32)


def _filter_spec_kernel(feat_ref, r_ref, w1_ref, b1_ref, w2_ref, b2_ref, fr_ref, w3_ref, dec_ref, fwd_ref, o_ref,
                        z_ref):
    hp = lax.Precision.HIGHEST

    @pl.when(pl.program_id(0) == 0)
    def _():
        z = jnp.sin(fr_ref[0:1, :] * (jnp.dot(feat_ref[...], w1_ref[...], precision=hp, preferred_element_type=F32)
                                      + b1_ref[...]))
        z_ref[...] = jnp.sin(fr_ref[1:2, :] * (jnp.dot(z, w2_ref[...], precision=hp, preferred_element_type=F32)
                                               + b2_ref[...]))

    filt = jnp.dot(z_ref[...], w3_ref[...], precision=hp, preferred_element_type=F32)
    filt = filt * jnp.exp(-r_ref[...] * jnp.abs(dec_ref[...]))
    spec = jnp.dot(fwd_ref[...], filt.astype(BF16), preferred_element_type=F32)
    length = spec.shape[0] // 2
    sa, sb = spec[:length], spec[length:]
    first = lax.broadcasted_iota(jnp.int32, sa.shape, 0) == 0
    o_ref[0:length, :] = sa
    o_ref[length:2 * length, :] = jnp.where(first, 0.0, sb)
    o_ref[2 * length:, :] = jnp.where(first, sb, sa)


def _filter_spectra(length, w1, b1, w2, b2, freq, w3, decay, fwd_bf16, tc):
    feat, r = _filter_features(length)
    nch = w3.shape[1]
    w1p = jnp.pad(w1, ((0, LANES - w1.shape[0]), (0, 0)))
    return pl.pallas_call(
        _filter_spec_kernel,
        out_shape=jax.ShapeDtypeStruct((3 * length, nch), F32),
        grid=(nch // tc,),
        in_specs=[_const_spec(feat.shape), _const_spec(r.shape), _const_spec(w1p.shape), _const_spec((1, HY_FHID)),
                  _const_spec(w2.shape), _const_spec((1, HY_FHID)), _const_spec(freq.shape),
                  pl.BlockSpec((HY_FHID, tc), lambda j: (0, j)), pl.BlockSpec((1, tc), lambda j: (0, j)),
                  _const_spec(fwd_bf16.shape)],
        out_specs=pl.BlockSpec((3 * length, tc), lambda j: (0, j)),
        scratch_shapes=[pltpu.VMEM((length, HY_FHID), F32)],
        compiler_params=_cparams("arbitrary"),
        name="hyena_filter_spectra",
    )(jnp.asarray(feat), jnp.asarray(r), w1p, b1.reshape(1, -1), w2, b2.reshape(1, -1), freq, w3,
      decay.reshape(1, -1), fwd_bf16)


def _hyena_kernel(v_ref, x1_ref, x2_ref, cwv_ref, cw1_ref, cw2_ref, cbv_ref, cb1_ref, cb2_ref,
                  h1_ref, h2_ref, sk1_ref, sk2_ref, fwd_ref, inv_ref, o_ref, *, sub, seqs):
    length, tc = v_ref.shape[0] // seqs, v_ref.shape[1]
    row = lax.broadcasted_iota(jnp.int32, (length, sub), 0)
    first, last = row == 0, row == length - 1

    for sq, c in [(a, b) for a in range(seqs) for b in range(tc // sub)]:
        cs = slice(c * sub, (c + 1) * sub)
        rs = slice(sq * length, (sq + 1) * length)

        def short_conv(x_ref, w_ref, b_ref):
            x = x_ref[rs, cs].astype(F32)
            prev = jnp.where(first, 0.0, pltpu.roll(x, 1, axis=0))
            nxt = jnp.where(last, 0.0, pltpu.roll(x, length - 1, axis=0))
            return prev * w_ref[0:1, cs] + x * w_ref[1:2, cs] + nxt * w_ref[2:3, cs] + b_ref[:, cs]

        def long_conv(z, h_ref):
            zs = jnp.dot(fwd_ref[...], z.astype(BF16), preferred_element_type=F32)
            za, zb = zs[:length], zs[length:]
            ha, hb, hc = h_ref[0:length, cs], h_ref[length:2 * length, cs], h_ref[2 * length:, cs]
            y = jnp.concatenate([za * ha - zb * hb, za * hb + zb * hc], axis=0).astype(BF16)
            return jnp.dot(inv_ref[...], y, preferred_element_type=F32)

        v = short_conv(v_ref, cwv_ref, cbv_ref)
        x1 = short_conv(x1_ref, cw1_ref, cb1_ref)
        x2 = short_conv(x2_ref, cw2_ref, cb2_ref)
        z = x1 * (long_conv(v, h1_ref) + sk1_ref[:, cs] * v)
        o_ref[rs, cs] = (x2 * (long_conv(z, h2_ref) + sk2_ref[:, cs] * z)).astype(o_ref.dtype)


def _hyena(u, seq, tc, sub, conv_w, conv_b, spectra, skip, fwd_bf16, inv_bf16, seqs=1):
    m = u.shape[0]
    b = m // (seqs * seq)
    nc = HY_CH // tc
    hy0 = 4 * RET_HEADS * RET_DK // tc
    ucol = lambda part: pl.BlockSpec((seqs * seq, tc), lambda i, j, part=part: (i, hy0 + part * nc + j))
    wcol = lambda rows, part: pl.BlockSpec((rows, tc), lambda i, j, part=part: (0, part * nc + j))
    return pl.pallas_call(
        functools.partial(_hyena_kernel, sub=sub, seqs=seqs),
        out_shape=jax.ShapeDtypeStruct((m, HY_CH), BF16),
        grid=(b, nc),
        in_specs=[ucol(0), ucol(1), ucol(2), wcol(3, 0), wcol(3, 1), wcol(3, 2), wcol(1, 0), wcol(1, 1), wcol(1, 2),
                  wcol(3 * seq, 0), wcol(3 * seq, 1), wcol(1, 0), wcol(1, 1),
                  _const_spec(fwd_bf16.shape), _const_spec(inv_bf16.shape)],
        out_specs=pl.BlockSpec((seqs * seq, tc), lambda i, j: (i, j)),
        compiler_params=_cparams("parallel", "parallel"),
        name="hyena",
    )(u, u, u, conv_w, conv_w, conv_w, conv_b, conv_b, conv_b, spectra, spectra, skip, skip, fwd_bf16, inv_bf16)


@functools.lru_cache(maxsize=None)
def _rope_tables(length):
    rows = np.repeat(np.arange(length // GRID_W, dtype=np.float64), GRID_W)
    cols = np.tile(np.arange(GRID_W, dtype=np.float64), length // GRID_W)

    def pack(dim):
        half = dim // 2
        freq = ROPE_THETA ** (-np.arange(0, half, 2, dtype=np.float64) / half)
        ang = np.concatenate([rows[:, None] * freq[None], cols[:, None] * freq[None]], axis=-1)
        pad = ((0, 0), (0, LANES // 2 - half))
        cos = np.pad(np.cos(ang), pad, constant_values=1.0)
        sin = np.pad(np.sin(ang), pad)
        return (np.concatenate([cos, cos], axis=-1).astype(np.float32),
                np.concatenate([-sin, sin], axis=-1).astype(np.float32))

    cos_m, sin_m = pack(MLA_ROPE)
    cos_g, sin_g = pack(HEAD_DIM)
    return cos_m, sin_m, cos_g, sin_g


def _spread_rope(a):
    half = MLA_ROPE // 2
    z = jnp.zeros(a.shape[:-1] + (LANES // 2 - half,), a.dtype)
    return jnp.concatenate([a[..., :half], z, a[..., half:], z], axis=-1)


def _unspread_rope(a):
    half = MLA_ROPE // 2
    return jnp.concatenate([a[..., :half], a[..., LANES // 2:LANES // 2 + half]], axis=-1)


def _even_weights(layer, w_in, q_norm, w_qb, kv_norm, w_kvb, nope_g, rope_g, gqa_g, w_out):
    wq = w_qb.reshape(MLA_Q_RANK, MLA_HEADS, MLA_NOPE + MLA_ROPE)
    wq = jnp.concatenate([wq[..., :MLA_NOPE], _spread_rope(wq[..., MLA_NOPE:])], axis=-1)
    return {
        "layer": layer,
        "w_in": jnp.swapaxes(w_in, 1, 2),
        "w_qb": wq.reshape(MLA_Q_RANK, MLA_HEADS * MLA_QK_PAD).astype(BF16),
        "w_kvb": w_kvb.astype(BF16),
        "w_out": w_out,
        "q_norm": q_norm.reshape(1, -1),
        "kv_norm": kv_norm.reshape(1, -1),
        "nope_g": nope_g,
        "rope_g": _spread_rope(rope_g),
        "gqa_g": gqa_g,
    }


def kernel(x_prompt, x_sample, cache_mla_ckv, cache_mla_krope, cache_gqa_k, cache_gqa_v, state_ret, c, c_ctx, mod_w, mod_b, ffn_w_gate, ffn_w_up, ffn_w_down, ev_w_in, mla_q_norm, mla_w_qb, mla_kv_norm, mla_w_kvb, mla_nope_norm, mla_rope_norm, gqa_qk_norm, ev_w_out, od_w_in, ret_decay_logit, ret_norm, hy_conv_w, hy_conv_b, hy_filt_w1, hy_filt_b1, hy_filt_w2, hy_filt_b2, hy_filt_freq, hy_filt_w3, hy_decay, hy_skip, od_w_out):
    xc = x_prompt.reshape(BATCH * SEQ, D_MODEL)
    xs = x_sample.reshape(DEC_BATCH * DEC_SEQ, D_MODEL)

    cvec = jnp.concatenate([c_ctx[None, :], c, jnp.zeros((8 - 1 - DEC_BATCH, D_MODEL), F32)], axis=0)
    assert DEPTH == 2
    mod_rows = {0: _mod_vectors(cvec, mod_w, mod_b, 1)[0]}

    tables = tuple(jnp.asarray(t) for t in _rope_tables(DEC_SEQ))

    outs = {}
    for l in range(DEPTH):
        mod_tab = mod_rows[l][:1 + DEC_BATCH].reshape((1 + DEC_BATCH) * N_MOD, 1, D_MODEL)
        xc = _ffn(xc, mod_tab, False, 0, ffn_w_gate, ffn_w_up, ffn_w_down, l, 0)
        xs = _ffn(xs, mod_tab, True, 0, ffn_w_gate, ffn_w_up, ffn_w_down, l, 0)
        if l % 2 == 0:
            e = l // 2
            w = _even_weights(e, ev_w_in, mla_q_norm[e], mla_w_qb[e], mla_kv_norm[e], mla_w_kvb[e],
                              mla_nope_norm[e], mla_rope_norm[e], gqa_qk_norm[e], ev_w_out[e])
            qm, ckv, kr, qg, kg, vg = _even_proj(xc, mod_tab, w)
            km, v = _kv_expand(ckv, kr, w["w_kvb"], w["nope_g"], 2 * TM_PROJ)
            att, mod_rows[l + 1] = _attention(qm, qg, km, v, kg, vg, SEQ, SEQ, seqs=SEQS_PER_STEP,
                                              side=(cvec, mod_w, mod_b, l + 1))
            xc = _out_proj(xc, mod_tab, False, [att], w["w_out"])
            outs["ckv"], outs["krope"], outs["k"], outs["v"] = ckv, _unspread_rope(kr), kg, vg
            qm, ckv, kr, qg, kg, vg = _even_proj(xs, mod_tab, w, tables)
            km, v = _kv_expand(ckv, kr, w["w_kvb"], w["nope_g"], TM_PROJ)
            c_ckv = cache_mla_ckv[:, e].reshape(DEC_BATCH * PAST_LEN, MLA_KV_RANK)
            c_kr = _spread_rope(cache_mla_krope[:, e].reshape(DEC_BATCH * PAST_LEN, MLA_ROPE))
            kmc, vc = _kv_expand(c_ckv, c_kr, w["w_kvb"], w["nope_g"], PAST_LEN)
            kgc = cache_gqa_k[:, e].reshape(DEC_BATCH * PAST_LEN * GQA_KV_HEADS, HEAD_DIM)
            vgc = cache_gqa_v[:, e].reshape(DEC_BATCH * PAST_LEN * GQA_KV_HEADS, HEAD_DIM)
            att = _attention(qm, qg, km, v, kg, vg, DEC_SEQ, TQ_ATT, cache=(kmc, vc, kgc, vgc))
            xs = _out_proj(xs, mod_tab, True, [att], w["w_out"])
        else:
            o = l // 2
            w_out = od_w_out[o]
            logit = ret_decay_logit[o]
            ret_g = ret_norm[o].reshape(1, -1)
            skip = hy_skip[o].reshape(1, -1)
            conv_b = hy_conv_b[o].reshape(1, -1)
            filt_args = (hy_filt_w1[o], hy_filt_b1[o], hy_filt_w2[o], hy_filt_b2[o], hy_filt_freq[o],
                         hy_filt_w3[o], hy_decay[o])
            for stream in ("ctx", "smp"):
                x, seq, latent = (xc, SEQ, False) if stream == "ctx" else (xs, DEC_SEQ, True)
                fwd, inv = _dft_matrices(seq)
                fwd_b, inv_b = jnp.asarray(fwd).astype(BF16), jnp.asarray(inv).astype(BF16)
                spectra = _filter_spectra(seq, *filt_args, fwd_b, TC_HY)
                u = _odd_proj(x, mod_tab, latent, od_w_in, o)
                if stream == "ctx":
                    o_ret, st = _retention(u, logit, ret_g, seq, RET_HEADS, seqs=SEQS_PER_STEP)
                    outs["ret"] = st
                    o_hy = _hyena(u, seq, HY_CH, HY_CH, hy_conv_w[o], conv_b, spectra, skip, fwd_b, inv_b, seqs=SEQS_PER_STEP)
                else:
                    (o_ret,) = _retention(u, logit, ret_g, seq, RET_HEADS_LAT, s0=state_ret[:, o])
                    o_hy = _hyena(u, seq, TC_HY, HY_SUB_LAT, hy_conv_w[o], conv_b, spectra, skip, fwd_b, inv_b)
                x = _out_proj(x, mod_tab, latent, [o_ret, o_hy], w_out)
                if stream == "ctx":
                    xc = x
                else:
                    xs = x
        xc = _ffn(xc, mod_tab, False, 2, ffn_w_gate, ffn_w_up, ffn_w_down, l, 1)
        xs = _ffn(xs, mod_tab, True, 2, ffn_w_gate, ffn_w_up, ffn_w_down, l, 1)

    return (
        xc.reshape(BATCH, SEQ, D_MODEL),
        xs.reshape(DEC_BATCH, DEC_SEQ, D_MODEL),
        outs["ckv"].reshape(BATCH, 1, SEQ, MLA_KV_RANK),
        outs["krope"].reshape(BATCH, 1, SEQ, MLA_ROPE),
        outs["k"].reshape(BATCH, 1, SEQ, GQA_KV_HEADS, HEAD_DIM),
        outs["v"].reshape(BATCH, 1, SEQ, GQA_KV_HEADS, HEAD_DIM),
        outs["ret"].reshape(BATCH, 1, 2, RET_HEADS, RET_DK, RET_DV),
    )
```

```python
import functools

import numpy as np
import jax
import jax.numpy as jnp
from jax import lax
from jax.experimental import pallas as pl
from jax.experimental.pallas import tpu as pltpu

F32 = jnp.float32
BF16 = jnp.bfloat16

D_MODEL = 2048
BATCH = 32
SEQ = 256
DEPTH = 2
DEC_BATCH = 2
DEC_SEQ = 1024
PAST_LEN = 256
GRID_W = 64
N_MOD = 9
ROPE_THETA = 10000.0
EPS = 1e-6

MLA_HEADS = 8
MLA_Q_RANK = 512
MLA_KV_RANK = 512
MLA_NOPE = 128
MLA_ROPE = 64
MLA_V = 128
GQA_HEADS = 8
GQA_KV_HEADS = 2
GQA_GROUP = GQA_HEADS // GQA_KV_HEADS
HEAD_DIM = 128
MLA_SCALE = (MLA_NOPE + MLA_ROPE) ** -0.5
GQA_SCALE = HEAD_DIM ** -0.5
MLA_QK_PAD = 256

RET_HEADS = 8
RET_DK = 128
RET_DV = 128
HY_CH = 1024
HY_BANDS = 16
HY_FHID = 64

V7X_VMEM_BYTES = 64 * 1024 * 1024
LANES = 128
VMEM_LIMIT = V7X_VMEM_BYTES - 3 * 1024 * 1024

TM_FFN = 1024
TF_FFN = 512
TF_SUB = 256
TM_PROJ = 512
TM_ODD_CTX, TN_ODD_CTX = 2048, 512
TM_ODD_LAT, TN_ODD_LAT = 1024, 1024
TN_MOD = 1024
TQ_ATT = 256
TC_HY = 512
HY_SUB_LAT = 256
RET_HEADS_LAT = 2
SEQS_PER_STEP = 2


def _cparams(*sem):
    return pltpu.CompilerParams(dimension_semantics=sem, vmem_limit_bytes=VMEM_LIMIT)


def _const_spec(shape):
    nd = len(shape)
    return pl.BlockSpec(shape, lambda *_: (0,) * nd)


def _silu(x):
    return x * jax.nn.sigmoid(x)


def _rms_rows(x, width=None):
    ss = jnp.sum(x * x, axis=-1, keepdims=True)
    n = x.shape[-1] if width is None else width
    return x * lax.rsqrt(ss * (1.0 / n) + EPS)


def _mod_tile(c_ref, w_ref, b_ref):
    s = _silu(c_ref[...]).astype(BF16)
    return jnp.dot(s, w_ref[...].astype(BF16), preferred_element_type=F32) + b_ref[...]


def _mod_kernel(c_ref, w_ref, b_ref, o_ref):
    o_ref[0] = _mod_tile(c_ref, w_ref.at[0], b_ref.at[0])


def _mod_vectors(cvec, mod_w, mod_b, depth):
    _, d, n = mod_w.shape
    rows = cvec.shape[0]
    return pl.pallas_call(
        _mod_kernel,
        out_shape=jax.ShapeDtypeStruct((depth, rows, n), F32),
        grid=(depth, n // TN_MOD),
        in_specs=[
            pl.BlockSpec((rows, d), lambda l, j: (0, 0)),
            pl.BlockSpec((1, d, TN_MOD), lambda l, j: (l, 0, j)),
            pl.BlockSpec((1, 1, TN_MOD), lambda l, j: (l, 0, j)),
        ],
        out_specs=pl.BlockSpec((1, rows, TN_MOD), lambda l, j: (l, 0, j)),
        compiler_params=_cparams("parallel", "parallel"),
        name="mod_vectors",
    )(cvec, mod_w, mod_b.reshape(mod_b.shape[0], 1, n))


def _mod_specs(sub, latent, tile, extra_axes=0):
    group_of_tile = (lambda i: 1 + (i * tile) // DEC_SEQ) if latent else (lambda i: 0)

    def spec(k):
        if extra_axes:
            return pl.BlockSpec((1, 1, D_MODEL), lambda i, j: (group_of_tile(i) * N_MOD + 3 * sub + k, 0, 0))
        return pl.BlockSpec((1, 1, D_MODEL), lambda i: (group_of_tile(i) * N_MOD + 3 * sub + k, 0, 0))
    return [spec(0), spec(1), spec(2)]


def _modulated(x, sh_ref, sc_ref):
    return (_rms_rows(x) * (1.0 + sc_ref[0]) + sh_ref[0]).astype(BF16)


def _ffn_kernel(x_hbm, sh_ref, sc_ref, gt_ref, wg_ref, wu_ref, wd_ref, o_ref, x_buf, h_ref, x_sem):
    i, f = pl.program_id(0), pl.program_id(1)
    tm = x_buf.shape[0]
    half_gate = 0.5 * gt_ref[0]

    def x_copy(tile):
        return pltpu.make_async_copy(x_hbm.at[pl.ds(tile * tm, tm), :], x_buf, x_sem)

    def partial_out(h):
        out = None
        for c in range(TF_FFN // TF_SUB):
            cs = slice(c * TF_SUB, (c + 1) * TF_SUB)
            g = jnp.dot(h, wg_ref[:, cs].astype(BF16), preferred_element_type=F32)
            u = jnp.dot(h, wu_ref[:, cs].astype(BF16), preferred_element_type=F32)
            a = (_silu(g) * u).astype(BF16)
            t = jnp.dot(a, wd_ref[cs, :].astype(BF16), preferred_element_type=F32)
            out = t if out is None else out + t
        return half_gate * out

    @pl.when(f == 0)
    def _():
        @pl.when(i == 0)
        def _():
            x_copy(0).start()

        x_copy(i).wait()
        x = x_buf[...]
        h = _modulated(x, sh_ref, sc_ref)
        h_ref[...] = h
        o_ref[...] = x + partial_out(h)

    @pl.when(f > 0)
    def _():
        @pl.when(jnp.logical_and(f == pl.num_programs(1) // 2, i + 1 < pl.num_programs(0)))
        def _():
            x_copy(i + 1).start()

        o_ref[...] += partial_out(h_ref[...])


def _ffn(x, mod_tab, latent, sub, w_gate, w_up, w_down, layer, j):
    m, d = x.shape
    dff = w_gate.shape[-1]
    assert dff // TF_FFN >= 2
    return pl.pallas_call(
        _ffn_kernel,
        out_shape=jax.ShapeDtypeStruct((m, d), F32),
        grid=(m // TM_FFN, dff // TF_FFN),
        in_specs=[pl.BlockSpec(memory_space=pl.ANY)]
        + _mod_specs(sub, latent, TM_FFN, extra_axes=1)
        + [
            pl.BlockSpec((None, None, d, TF_FFN), lambda i, f: (layer, j, 0, f)),
            pl.BlockSpec((None, None, d, TF_FFN), lambda i, f: (layer, j, 0, f)),
            pl.BlockSpec((None, None, TF_FFN, d), lambda i, f: (layer, j, f, 0)),
        ],
        out_specs=pl.BlockSpec((TM_FFN, d), lambda i, f: (i, 0)),
        scratch_shapes=[pltpu.VMEM((TM_FFN, d), F32), pltpu.VMEM((TM_FFN, d), BF16), pltpu.SemaphoreType.DMA(())],
        compiler_params=_cparams("arbitrary", "arbitrary"),
        name="ffn",
    )(x, mod_tab, mod_tab, mod_tab, w_gate, w_up, w_down)


def _out_proj_kernel(*refs, n_parts):
    x_ref, gt_ref, w_ref = refs[0], refs[1], refs[2]
    part_refs = refs[3:3 + n_parts]
    o_ref, wb_ref = refs[3 + n_parts], refs[4 + n_parts]

    @pl.when(pl.program_id(0) == 0)
    def _():
        wb_ref[...] = w_ref[...].astype(BF16)

    acc, off = None, 0
    for p_ref in part_refs:
        width = p_ref.shape[1]
        t = jnp.dot(p_ref[...], wb_ref[off:off + width, :], preferred_element_type=F32)
        acc = t if acc is None else acc + t
        off += width
    o_ref[...] = x_ref[...] + gt_ref[0] * acc


def _out_proj(x, mod_tab, latent, parts, w_out):
    m, d = x.shape
    return pl.pallas_call(
        functools.partial(_out_proj_kernel, n_parts=len(parts)),
        out_shape=jax.ShapeDtypeStruct((m, d), F32),
        grid=(m // TM_PROJ,),
        in_specs=[pl.BlockSpec((TM_PROJ, d), lambda i: (i, 0)), _mod_specs(1, latent, TM_PROJ)[2],
                  pl.BlockSpec(w_out.shape, lambda i: (0, 0), pipeline_mode=pl.Buffered(1))]
        + [pl.BlockSpec((TM_PROJ, p.shape[1]), lambda i: (i, 0)) for p in parts],
        out_specs=pl.BlockSpec((TM_PROJ, d), lambda i: (i, 0)),
        scratch_shapes=[pltpu.VMEM(w_out.shape, BF16)],
        compiler_params=_cparams("arbitrary"),
        name="mixer_out_proj",
    )(x, mod_tab, w_out, *parts)


EV_QA, EV_KVA, EV_QG, EV_KG, EV_VG, EV_KR, EV_END = 0, 512, 1024, 2048, 2304, 2560, 2688


def _rotate_half(x, cos, sin):
    return x * cos + pltpu.roll(x, LANES // 2, axis=1) * sin


def _even_proj_kernel(*refs, rotary):
    (x_ref, sh_ref, sc_ref, w_in_ref, w_qb_ref, qn_ref, kvn_ref, ng_ref, rg_ref, gg_ref) = refs[:10]
    if rotary:
        cm_ref, sm_ref, cg_ref, sg_ref = refs[10:14]
        rot_m = lambda t: _rotate_half(t, cm_ref[...], sm_ref[...])
        rot_g = lambda t: _rotate_half(t, cg_ref[...], sg_ref[...])
    else:
        rot_m = rot_g = lambda t: t
    qm_ref, ckv_ref, kr_ref, qg_ref, kg_ref, vg_ref, wb_ref = refs[-7:]

    @pl.when(pl.program_id(0) == 0)
    def _():
        chunk = 128
        half = MLA_ROPE // 2

        def copy_rows(dst0, src0, n_chunks):
            def body(c, carry):
                dst = pl.ds(pl.multiple_of(dst0 + c * chunk, 16), chunk)
                src = pl.ds(pl.multiple_of(src0 + c * chunk, 8), chunk)
                wb_ref[dst, :] = w_in_ref[src, :].astype(BF16)
                return carry
            lax.fori_loop(0, n_chunks, body, 0)

        copy_rows(EV_QA, 0, EV_QG // chunk)
        copy_rows(EV_QG, EV_QG + MLA_ROPE, (EV_KR - EV_QG) // chunk)
        wb_ref[EV_KR:EV_END, :] = jnp.zeros((EV_END - EV_KR, wb_ref.shape[1]), BF16)
        wb_ref[EV_KR:EV_KR + half, :] = w_in_ref[EV_QG:EV_QG + half, :].astype(BF16)
        wb_ref[EV_KR + LANES // 2:EV_KR + LANES // 2 + half, :] = w_in_ref[EV_QG + half:EV_QG + MLA_ROPE, :].astype(BF16)

    h = _modulated(x_ref[...], sh_ref, sc_ref)
    u = _qk(h, wb_ref[...])

    nope_gain = ng_ref[0:1, :] * MLA_SCALE
    rope_gain = rg_ref[0:1, :] * MLA_SCALE
    gqa_gain = gg_ref[0:1, :] * GQA_SCALE

    qa = (_rms_rows(u[:, EV_QA:EV_KVA]) * qn_ref[...]).astype(BF16)
    q = jnp.dot(qa, w_qb_ref[...], preferred_element_type=F32)
    for hd in range(MLA_HEADS):
        c0 = hd * MLA_QK_PAD
        nope = _rms_rows(q[:, c0:c0 + MLA_NOPE]) * nope_gain
        rope = _rms_rows(q[:, c0 + MLA_NOPE:c0 + MLA_QK_PAD], MLA_ROPE) * rope_gain
        qm_ref[:, c0:c0 + MLA_NOPE] = nope.astype(BF16)
        qm_ref[:, c0 + MLA_NOPE:c0 + MLA_QK_PAD] = rot_m(rope).astype(BF16)

    ckv_ref[...] = _rms_rows(u[:, EV_KVA:EV_QG]) * kvn_ref[...]
    kr_ref[...] = rot_m(_rms_rows(u[:, EV_KR:EV_END], MLA_ROPE) * rg_ref[1:2, :])

    for hd in range(GQA_HEADS):
        c0 = EV_QG + hd * HEAD_DIM
        qh = _rms_rows(u[:, c0:c0 + HEAD_DIM]) * gqa_gain
        qg_ref[:, hd * HEAD_DIM:(hd + 1) * HEAD_DIM] = rot_g(qh).astype(BF16)
    for hd in range(GQA_KV_HEADS):
        c0 = EV_KG + hd * HEAD_DIM
        kh = _rms_rows(u[:, c0:c0 + HEAD_DIM]) * gg_ref[1:2, :]
        kg_ref[pl.ds(hd, kh.shape[0], stride=GQA_KV_HEADS), :] = rot_g(kh)
        vg_ref[pl.ds(hd, kh.shape[0], stride=GQA_KV_HEADS), :] = u[:, EV_VG + hd * HEAD_DIM:EV_VG + (hd + 1) * HEAD_DIM]


def _even_proj(x, mod_tab, w, tables=None):
    m, d = x.shape
    tm = TM_PROJ
    row = lambda w_: pl.BlockSpec((tm, w_), lambda i: (i, 0))
    tab_specs, tab_args = [], []
    if tables is not None:
        tiles_per_seq = tables[0].shape[0] // tm
        tab_specs = [pl.BlockSpec((tm, LANES), lambda i: (i % tiles_per_seq, 0))] * 4
        tab_args = list(tables)
    return pl.pallas_call(
        functools.partial(_even_proj_kernel, rotary=tables is not None),
        out_shape=[
            jax.ShapeDtypeStruct((m, MLA_HEADS * MLA_QK_PAD), BF16),
            jax.ShapeDtypeStruct((m, MLA_KV_RANK), F32),
            jax.ShapeDtypeStruct((m, LANES), F32),
            jax.ShapeDtypeStruct((m, GQA_HEADS * HEAD_DIM), BF16),
            jax.ShapeDtypeStruct((m * GQA_KV_HEADS, HEAD_DIM), F32),
            jax.ShapeDtypeStruct((m * GQA_KV_HEADS, HEAD_DIM), F32),
        ],
        grid=(m // tm,),
        in_specs=[row(d)] + _mod_specs(1, tables is not None, tm)[:2]
        + [pl.BlockSpec((None,) + w["w_in"].shape[1:], lambda i: (w["layer"], 0, 0)),
           _const_spec(w["w_qb"].shape), _const_spec(w["q_norm"].shape),
           _const_spec(w["kv_norm"].shape), _const_spec(w["nope_g"].shape), _const_spec(w["rope_g"].shape),
           _const_spec(w["gqa_g"].shape)] + tab_specs,
        out_specs=[row(MLA_HEADS * MLA_QK_PAD), row(MLA_KV_RANK), row(LANES), row(GQA_HEADS * HEAD_DIM),
                   pl.BlockSpec((tm * GQA_KV_HEADS, HEAD_DIM), lambda i: (i, 0)),
                   pl.BlockSpec((tm * GQA_KV_HEADS, HEAD_DIM), lambda i: (i, 0))],
        scratch_shapes=[pltpu.VMEM((EV_END, d), BF16)],
        compiler_params=_cparams("arbitrary"),
        name="even_proj",
    )(x, mod_tab, mod_tab, w["w_in"], w["w_qb"], w["q_norm"], w["kv_norm"], w["nope_g"], w["rope_g"], w["gqa_g"],
      *tab_args)


def _kv_expand_kernel(ckv_ref, kr_ref, w_ref, ng_ref, km_ref, v_ref):
    kv = jnp.dot(ckv_ref[...].astype(BF16), w_ref[...], preferred_element_type=F32)
    kr = kr_ref[...].astype(BF16)
    for hd in range(MLA_HEADS):
        c0 = hd * (MLA_NOPE + MLA_V)
        km_ref[:, hd * MLA_QK_PAD:hd * MLA_QK_PAD + MLA_NOPE] = (
            _rms_rows(kv[:, c0:c0 + MLA_NOPE]) * ng_ref[1:2, :]).astype(BF16)
        km_ref[:, hd * MLA_QK_PAD + MLA_NOPE:(hd + 1) * MLA_QK_PAD] = kr
        v_ref[:, hd * MLA_V:(hd + 1) * MLA_V] = kv[:, c0 + MLA_NOPE:c0 + MLA_NOPE + MLA_V].astype(BF16)


def _kv_expand(ckv, kr, w_kvb_bf16, nope_g, tr):
    r = ckv.shape[0]
    row = lambda w_: pl.BlockSpec((tr, w_), lambda i: (i, 0))
    return pl.pallas_call(
        _kv_expand_kernel,
        out_shape=[jax.ShapeDtypeStruct((r, MLA_HEADS * MLA_QK_PAD), BF16),
                   jax.ShapeDtypeStruct((r, MLA_HEADS * MLA_V), BF16)],
        grid=(r // tr,),
        in_specs=[row(MLA_KV_RANK), row(LANES), _const_spec(w_kvb_bf16.shape), _const_spec(nope_g.shape)],
        out_specs=[row(MLA_HEADS * MLA_QK_PAD), row(MLA_HEADS * MLA_V)],
        compiler_params=_cparams("parallel"),
        name="kv_expand",
    )(ckv, kr, w_kvb_bf16, nope_g)


def _softmax_pv(score_list, value_list):
    mx = None
    for s in score_list:
        m_ = jnp.max(s, axis=-1, keepdims=True)
        mx = m_ if mx is None else jnp.maximum(mx, m_)
    den, acc = None, None
    for s, v in zip(score_list, value_list):
        p = jnp.exp(s - mx)
        l_ = jnp.sum(p, axis=-1, keepdims=True)
        o_ = jnp.dot(p.astype(BF16), v, preferred_element_type=F32)
        den = l_ if den is None else den + l_
        acc = o_ if acc is None else acc + o_
    return acc / den


def _qk(q, k):
    return lax.dot_general(q, k, (((1,), (1,)), ((), ())), preferred_element_type=F32)


def _attn_kernel(*refs, has_cache, seqs, has_side):
    if has_side:
        c_ref, mw_ref, mb_ref, mo_ref = refs[-5], refs[-4], refs[-3], refs[-1]
        mo_ref[...] = _mod_tile(c_ref, mw_ref, mb_ref)
        refs = refs[:-5] + (refs[-2],)
    if has_cache:
        qm_ref, qg_ref, km_ref, v_ref, kg_ref, vg_ref, kmc_ref, vc_ref, kgc_ref, vgc_ref, o_ref = refs
    else:
        qm_ref, qg_ref, km_ref, v_ref, kg_ref, vg_ref, o_ref = refs
    rows_q, rows_k = qm_ref.shape[0] // seqs, km_ref.shape[0] // seqs
    base = MLA_HEADS * MLA_V
    for sq in range(seqs):
        rq = slice(sq * rows_q, (sq + 1) * rows_q)
        rk = slice(sq * rows_k, (sq + 1) * rows_k)
        for hd in range(MLA_HEADS):
            qs = slice(hd * MLA_QK_PAD, (hd + 1) * MLA_QK_PAD)
            vs = slice(hd * MLA_V, (hd + 1) * MLA_V)
            q = qm_ref[rq, qs]
            scores = [_qk(q, km_ref[rk, qs])]
            values = [v_ref[rk, vs]]
            if has_cache:
                scores.append(_qk(q, kmc_ref[:, qs]))
                values.append(vc_ref[:, vs])
            o_ref[rq, vs] = _softmax_pv(scores, values).astype(o_ref.dtype)
        for kvh in range(GQA_KV_HEADS):
            own = pl.ds(sq * rows_k * GQA_KV_HEADS + kvh, rows_k, stride=GQA_KV_HEADS)
            k = kg_ref[own, :].astype(BF16)
            v = vg_ref[own, :].astype(BF16)
            if has_cache:
                past = pl.ds(kvh, kgc_ref.shape[0] // GQA_KV_HEADS, stride=GQA_KV_HEADS)
                kc = kgc_ref[past, :].astype(BF16)
                vc = vgc_ref[past, :].astype(BF16)
            for g in range(GQA_GROUP):
                hd = kvh * GQA_GROUP + g
                hs = slice(hd * HEAD_DIM, (hd + 1) * HEAD_DIM)
                q = qg_ref[rq, hs]
                scores, values = [_qk(q, k)], [v]
                if has_cache:
                    scores.append(_qk(q, kc))
                    values.append(vc)
                o_ref[rq, base + hd * HEAD_DIM:base + (hd + 1) * HEAD_DIM] = (
                    _softmax_pv(scores, values).astype(o_ref.dtype))


def _attention(qm, qg, km, v, kg, vg, seq, tq, cache=None, seqs=1, side=None):
    m = qm.shape[0]
    nq = seq // tq
    assert seqs == 1 or (cache is None and nq == 1)
    qrow = lambda w_: pl.BlockSpec((seqs * tq, w_), lambda i: (i, 0))
    krow = lambda w_: pl.BlockSpec((seqs * seq, w_), lambda i: (i // nq, 0))
    grow = lambda rows: pl.BlockSpec((rows * GQA_KV_HEADS, HEAD_DIM), lambda i: (i // nq, 0))
    in_specs = [qrow(qm.shape[1]), qrow(qg.shape[1]), krow(km.shape[1]), krow(v.shape[1]),
                grow(seqs * seq), grow(seqs * seq)]
    args = [qm, qg, km, v, kg, vg]
    if cache is not None:
        past = cache[0].shape[0] // (m // seq)
        crow = lambda w_: pl.BlockSpec((past, w_), lambda i: (i // nq, 0))
        in_specs += [crow(cache[0].shape[1]), crow(cache[1].shape[1]), grow(past), grow(past)]
        args += list(cache)
    width = MLA_HEADS * MLA_V + GQA_HEADS * HEAD_DIM
    steps = m // (seqs * tq)
    out_shape = [jax.ShapeDtypeStruct((m, width), BF16)]
    out_specs = [qrow(width)]
    if side is not None:
        cvec, mod_w, mod_b, layer = side
        _, d, n = mod_w.shape
        tn = n // steps
        assert tn * steps == n and tn % LANES == 0
        in_specs += [_const_spec(cvec.shape), pl.BlockSpec((None, d, tn), lambda i: (layer, 0, i)),
                     pl.BlockSpec((None, 1, tn), lambda i: (layer, 0, i))]
        args += [cvec, mod_w, mod_b.reshape(mod_b.shape[0], 1, n)]
        out_shape.append(jax.ShapeDtypeStruct((cvec.shape[0], n), F32))
        out_specs.append(pl.BlockSpec((cvec.shape[0], tn), lambda i: (0, i)))
    res = pl.pallas_call(
        functools.partial(_attn_kernel, has_cache=cache is not None, seqs=seqs, has_side=side is not None),
        out_shape=out_shape,
        grid=(steps,),
        in_specs=in_specs,
        out_specs=out_specs,
        compiler_params=_cparams("parallel"),
        name="attention",
    )(*args)
    return res if side is not None else res[0]


def _odd_proj_kernel(x_hbm, sh_ref, sc_ref, w_ref, o_ref, x_buf, h_ref, x_sem):
    i, j = pl.program_id(0), pl.program_id(1)
    tm = x_buf.shape[0]

    def x_copy(tile):
        return pltpu.make_async_copy(x_hbm.at[pl.ds(tile * tm, tm), :], x_buf, x_sem)

    def project(h):
        return jnp.dot(h, w_ref[...].astype(BF16), preferred_element_type=F32).astype(o_ref.dtype)

    @pl.when(j == 0)
    def _():
        @pl.when(i == 0)
        def _():
            x_copy(0).start()

        x_copy(i).wait()
        h = _modulated(x_buf[...], sh_ref, sc_ref)
        h_ref[...] = h
        o_ref[...] = project(h)

    @pl.when(j > 0)
    def _():
        @pl.when(jnp.logical_and(j == pl.num_programs(1) // 2, i + 1 < pl.num_programs(0)))
        def _():
            x_copy(i + 1).start()

        o_ref[...] = project(h_ref[...])


def _odd_proj(x, mod_tab, latent, w_in, layer):
    m, d = x.shape
    n = w_in.shape[-1]
    tm, tn = (TM_ODD_LAT, TN_ODD_LAT) if latent else (TM_ODD_CTX, TN_ODD_CTX)
    assert n // tn >= 2
    return pl.pallas_call(
        _odd_proj_kernel,
        out_shape=jax.ShapeDtypeStruct((m, n), BF16),
        grid=(m // tm, n // tn),
        in_specs=[pl.BlockSpec(memory_space=pl.ANY)]
        + _mod_specs(1, latent, tm, extra_axes=1)[:2]
        + [pl.BlockSpec((None, d, tn), lambda i, j: (layer, 0, j))],
        out_specs=pl.BlockSpec((tm, tn), lambda i, j: (i, j)),
        scratch_shapes=[pltpu.VMEM((tm, d), F32), pltpu.VMEM((tm, d), BF16), pltpu.SemaphoreType.DMA(())],
        compiler_params=_cparams("arbitrary", "arbitrary"),
        name="odd_proj",
    )(x, mod_tab, mod_tab, w_in)


def _log_sigmoid(x):
    return jnp.minimum(x, 0.0) - jnp.log1p(jnp.exp(-jnp.abs(x)))


def _retention_kernel(*refs, heads, has_init, seqs):
    if has_init:
        logit_ref, q_ref, k_ref, v_ref, g_ref, rg_ref, s0_ref, o_ref, dec_ref, vec_ref = refs
    else:
        logit_ref, q_ref, k_ref, v_ref, g_ref, rg_ref, o_ref, st_ref, dec_ref, vec_ref = refs
    length = q_ref.shape[0] // seqs
    h0 = pl.program_id(0) * heads
    scale = RET_DK ** -0.5
    log_g = _log_sigmoid(logit_ref[...])
    lane_h = lax.broadcasted_iota(jnp.int32, (2, RET_HEADS), 1)

    def head_log_decays(hh):
        lg = jnp.sum(jnp.where(lane_h == h0 + hh, log_g, 0.0), axis=1, keepdims=True)
        return lg[0:1, :], lg[1:2, :]

    @pl.when(pl.program_id(1) == 0)
    def _():
        n_i = lax.broadcasted_iota(jnp.int32, (length, length), 0)
        m_i = lax.broadcasted_iota(jnp.int32, (length, length), 1)
        diff = (n_i - m_i).astype(F32)
        pos = lax.broadcasted_iota(jnp.int32, (length, RET_DK), 0).astype(F32)
        for hh in range(heads):
            lg_f, lg_b = head_log_decays(hh)
            dec_ref[hh] = scale * (jnp.where(diff >= 0, jnp.exp(jnp.maximum(diff, 0.0) * lg_f), 0.0)
                                   + jnp.where(diff <= 0, jnp.exp(jnp.maximum(-diff, 0.0) * lg_b), 0.0))
            if has_init:
                vec_ref[4 * hh + 2] = jnp.exp((pos + 1.0) * lg_f)
                vec_ref[4 * hh + 3] = jnp.exp((length - pos) * lg_b)
            else:
                vec_ref[4 * hh + 0] = jnp.exp((length - 1.0 - pos) * lg_f) * scale
                vec_ref[4 * hh + 1] = jnp.exp(pos * lg_b) * scale

    tn = (((0,), (0,)), ((), ()))
    for sq, hh in [(a, b) for a in range(seqs) for b in range(heads)]:
        cs = slice(hh * RET_DK, (hh + 1) * RET_DK)
        rs = slice(sq * length, (sq + 1) * length)
        q, k, v = q_ref[rs, cs], k_ref[rs, cs], v_ref[rs, cs]
        att = _qk(q, k) * dec_ref[hh]
        o = jnp.dot(att.astype(BF16), v, preferred_element_type=F32)
        if has_init:
            s0_f, s0_b = s0_ref[sq, 0, hh].astype(BF16), s0_ref[sq, 1, hh].astype(BF16)
            o = o + jnp.dot(q, s0_f, preferred_element_type=F32) * vec_ref[4 * hh + 2]
            o = o + jnp.dot(q, s0_b, preferred_element_type=F32) * vec_ref[4 * hh + 3]
        else:
            kf = k.astype(F32)
            k_f, k_b = (kf * vec_ref[4 * hh + 0]).astype(BF16), (kf * vec_ref[4 * hh + 1]).astype(BF16)
            st_ref[sq, 0, hh] = lax.dot_general(k_f, v, tn, preferred_element_type=F32)
            st_ref[sq, 1, hh] = lax.dot_general(k_b, v, tn, preferred_element_type=F32)
        gate = g_ref[rs, cs].astype(F32)
        o_ref[rs, cs] = (_rms_rows(o) * rg_ref[:, cs] * _silu(gate)).astype(o_ref.dtype)


def _retention(u, decay_logit, ret_g, seq, heads, s0=None, seqs=1):
    m = u.shape[0]
    b = m // seq
    hw = heads * RET_DK
    nh = RET_HEADS // heads
    blocks_per_part = RET_HEADS * RET_DK // hw
    col = lambda part: pl.BlockSpec((seqs * seq, hw), lambda j, i, part=part: (i, part * blocks_per_part + j))
    st_spec = pl.BlockSpec((seqs, 2, heads, RET_DK, RET_DV), lambda j, i: (i, 0, j, 0, 0))
    in_specs = [_const_spec(decay_logit.shape), col(0), col(1), col(2), col(3),
                pl.BlockSpec((1, hw), lambda j, i: (0, j))]
    args = [decay_logit, u, u, u, u, ret_g]
    if s0 is not None:
        in_specs.append(st_spec)
        args.append(s0)
    out_shape = [jax.ShapeDtypeStruct((m, RET_HEADS * RET_DV), BF16)]
    out_specs = [pl.BlockSpec((seqs * seq, hw), lambda j, i: (i, j))]
    if s0 is None:
        out_shape.append(jax.ShapeDtypeStruct((b, 2, RET_HEADS, RET_DK, RET_DV), F32))
        out_specs.append(st_spec)
    return pl.pallas_call(
        functools.partial(_retention_kernel, heads=heads, has_init=s0 is not None, seqs=seqs),
        out_shape=out_shape,
        grid=(nh, b // seqs),
        in_specs=in_specs,
        out_specs=out_specs,
        scratch_shapes=[pltpu.VMEM((heads, seq, seq), F32), pltpu.VMEM((4 * heads, seq, RET_DK), F32)],
        compiler_params=_cparams("parallel", "arbitrary"),
        name="retention",
    )(*args)


@functools.lru_cache(maxsize=None)
def _dft_matrices(length):
    n = 2 * length
    f = np.arange(length, dtype=np.float64)[:, None]
    t = np.arange(length, dtype=np.float64)[None, :]
    ang = 2.0 * np.pi * f * t / n
    fwd_a = np.cos(ang)
    fwd_b = np.sin(ang)
    fwd_b[0, :] = np.cos(np.pi * t[0])
    fwd = np.concatenate([fwd_a, fwd_b], axis=0)
    tt = (np.arange(length, dtype=np.float64) + length // 2)[:, None]
    ff = np.arange(length, dtype=np.float64)[None, :]
    ang_i = 2.0 * np.pi * ff * tt / n
    inv_a = 2.0 * np.cos(ang_i) / n
    inv_b = 2.0 * np.sin(ang_i) / n
    inv_a[:, 0] = 1.0 / n
    inv_b[:, 0] = np.cos(np.pi * tt[:, 0]) / n
    inv = np.concatenate([inv_a, inv_b], axis=1)
    return fwd.astype(np.float32), inv.astype(np.float32)


@functools.lru_cache(maxsize=None)
def _filter_features(length):
    t = np.arange(length, dtype=np.float64)
    tn = t / length
    bands = np.arange(1, HY_BANDS + 1, dtype=np.float64)
    ang = 2.0 * np.pi * tn[:, None] * bands[None, :]
    feat = np.concatenate([tn[:, None], np.sin(ang), np.cos(ang)], axis=-1)
    feat = np.pad(feat, ((0, 0), (0, LANES - feat.shape[1])))
    r = (np.abs(t - length // 2) / (length / 2))[:, None]
    return feat.astype(np.float32), r.astype(np.float32)


def _filter_spec_kernel(feat_ref, r_ref, w1_ref, b1_ref, w2_ref, b2_ref, fr_ref, w3_ref, dec_ref, fwd_ref, o_ref,
                        z_ref):
    hp = lax.Precision.HIGHEST

    @pl.when(pl.program_id(0) == 0)
    def _():
        z = jnp.sin(fr_ref[0:1, :] * (jnp.dot(feat_ref[...], w1_ref[...], precision=hp, preferred_element_type=F32)
                                      + b1_ref[...]))
        z_ref[...] = jnp.sin(fr_ref[1:2, :] * (jnp.dot(z, w2_ref[...], precision=hp, preferred_element_type=F32)
                                               + b2_ref[...]))

    filt = jnp.dot(z_ref[...], w3_ref[...], precision=hp, preferred_element_type=F32)
    filt = filt * jnp.exp(-r_ref[...] * jnp.abs(dec_ref[...]))
    spec = jnp.dot(fwd_ref[...], filt.astype(BF16), preferred_element_type=F32)
    length = spec.shape[0] // 2
    sa, sb = spec[:length], spec[length:]
    first = lax.broadcasted_iota(jnp.int32, sa.shape, 0) == 0
    o_ref[0:length, :] = sa
    o_ref[length:2 * length, :] = jnp.where(first, 0.0, sb)
    o_ref[2 * length:, :] = jnp.where(first, sb, sa)


def _filter_spectra(length, w1, b1, w2, b2, freq, w3, decay, fwd_bf16, tc):
    feat, r = _filter_features(length)
    nch = w3.shape[1]
    w1p = jnp.pad(w1, ((0, LANES - w1.shape[0]), (0, 0)))
    return pl.pallas_call(
        _filter_spec_kernel,
        out_shape=jax.ShapeDtypeStruct((3 * length, nch), F32),
        grid=(nch // tc,),
        in_specs=[_const_spec(feat.shape), _const_spec(r.shape), _const_spec(w1p.shape), _const_spec((1, HY_FHID)),
                  _const_spec(w2.shape), _const_spec((1, HY_FHID)), _const_spec(freq.shape),
                  pl.BlockSpec((HY_FHID, tc), lambda j: (0, j)), pl.BlockSpec((1, tc), lambda j: (0, j)),
                  _const_spec(fwd_bf16.shape)],
        out_specs=pl.BlockSpec((3 * length, tc), lambda j: (0, j)),
        scratch_shapes=[pltpu.VMEM((length, HY_FHID), F32)],
        compiler_params=_cparams("arbitrary"),
        name="hyena_filter_spectra",
    )(jnp.asarray(feat), jnp.asarray(r), w1p, b1.reshape(1, -1), w2, b2.reshape(1, -1), freq, w3,
      decay.reshape(1, -1), fwd_bf16)


def _hyena_kernel(v_ref, x1_ref, x2_ref, cwv_ref, cw1_ref, cw2_ref, cbv_ref, cb1_ref, cb2_ref,
                  h1_ref, h2_ref, sk1_ref, sk2_ref, fwd_ref, inv_ref, o_ref, *, sub, seqs):
    length, tc = v_ref.shape[0] // seqs, v_ref.shape[1]
    row = lax.broadcasted_iota(jnp.int32, (length, sub), 0)
    first, last = row == 0, row == length - 1

    for sq, c in [(a, b) for a in range(seqs) for b in range(tc // sub)]:
        cs = slice(c * sub, (c + 1) * sub)
        rs = slice(sq * length, (sq + 1) * length)

        def short_conv(x_ref, w_ref, b_ref):
            x = x_ref[rs, cs].astype(F32)
            prev = jnp.where(first, 0.0, pltpu.roll(x, 1, axis=0))
            nxt = jnp.where(last, 0.0, pltpu.roll(x, length - 1, axis=0))
            return prev * w_ref[0:1, cs] + x * w_ref[1:2, cs] + nxt * w_ref[2:3, cs] + b_ref[:, cs]

        def long_conv(z, h_ref):
            zs = jnp.dot(fwd_ref[...], z.astype(BF16), preferred_element_type=F32)
            za, zb = zs[:length], zs[length:]
            ha, hb, hc = h_ref[0:length, cs], h_ref[length:2 * length, cs], h_ref[2 * length:, cs]
            y = jnp.concatenate([za * ha - zb * hb, za * hb + zb * hc], axis=0).astype(BF16)
            return jnp.dot(inv_ref[...], y, preferred_element_type=F32)

        v = short_conv(v_ref, cwv_ref, cbv_ref)
        x1 = short_conv(x1_ref, cw1_ref, cb1_ref)
        x2 = short_conv(x2_ref, cw2_ref, cb2_ref)
        z = x1 * (long_conv(v, h1_ref) + sk1_ref[:, cs] * v)
        o_ref[rs, cs] = (x2 * (long_conv(z, h2_ref) + sk2_ref[:, cs] * z)).astype(o_ref.dtype)


def _hyena(u, seq, tc, sub, conv_w, conv_b, spectra, skip, fwd_bf16, inv_bf16, seqs=1):
    m = u.shape[0]
    b = m // (seqs * seq)
    nc = HY_CH // tc
    hy0 = 4 * RET_HEADS * RET_DK // tc
    ucol = lambda part: pl.BlockSpec((seqs * seq, tc), lambda i, j, part=part: (i, hy0 + part * nc + j))
    wcol = lambda rows, part: pl.BlockSpec((rows, tc), lambda i, j, part=part: (0, part * nc + j))
    return pl.pallas_call(
        functools.partial(_hyena_kernel, sub=sub, seqs=seqs),
        out_shape=jax.ShapeDtypeStruct((m, HY_CH), BF16),
        grid=(b, nc),
        in_specs=[ucol(0), ucol(1), ucol(2), wcol(3, 0), wcol(3, 1), wcol(3, 2), wcol(1, 0), wcol(1, 1), wcol(1, 2),
                  wcol(3 * seq, 0), wcol(3 * seq, 1), wcol(1, 0), wcol(1, 1),
                  _const_spec(fwd_bf16.shape), _const_spec(inv_bf16.shape)],
        out_specs=pl.BlockSpec((seqs * seq, tc), lambda i, j: (i, j)),
        compiler_params=_cparams("parallel", "parallel"),
        name="hyena",
    )(u, u, u, conv_w, conv_w, conv_w, conv_b, conv_b, conv_b, spectra, spectra, skip, skip, fwd_bf16, inv_bf16)


@functools.lru_cache(maxsize=None)
def _rope_tables(length):
    rows = np.repeat(np.arange(length // GRID_W, dtype=np.float64), GRID_W)
    cols = np.tile(np.arange(GRID_W, dtype=np.float64), length // GRID_W)

    def pack(dim):
        half = dim // 2
        freq = ROPE_THETA ** (-np.arange(0, half, 2, dtype=np.float64) / half)
        ang = np.concatenate([rows[:, None] * freq[None], cols[:, None] * freq[None]], axis=-1)
        pad = ((0, 0), (0, LANES // 2 - half))
        cos = np.pad(np.cos(ang), pad, constant_values=1.0)
        sin = np.pad(np.sin(ang), pad)
        return (np.concatenate([cos, cos], axis=-1).astype(np.float32),
                np.concatenate([-sin, sin], axis=-1).astype(np.float32))

    cos_m, sin_m = pack(MLA_ROPE)
    cos_g, sin_g = pack(HEAD_DIM)
    return cos_m, sin_m, cos_g, sin_g


def _spread_rope(a):
    half = MLA_ROPE // 2
    z = jnp.zeros(a.shape[:-1] + (LANES // 2 - half,), a.dtype)
    return jnp.concatenate([a[..., :half], z, a[..., half:], z], axis=-1)


def _unspread_rope(a):
    half = MLA_ROPE // 2
    return jnp.concatenate([a[..., :half], a[..., LANES // 2:LANES // 2 + half]], axis=-1)


def _even_weights(layer, w_in, q_norm, w_qb, kv_norm, w_kvb, nope_g, rope_g, gqa_g, w_out):
    wq = w_qb.reshape(MLA_Q_RANK, MLA_HEADS, MLA_NOPE + MLA_ROPE)
    wq = jnp.concatenate([wq[..., :MLA_NOPE], _spread_rope(wq[..., MLA_NOPE:])], axis=-1)
    return {
        "layer": layer,
        "w_in": jnp.swapaxes(w_in, 1, 2),
        "w_qb": wq.reshape(MLA_Q_RANK, MLA_HEADS * MLA_QK_PAD).astype(BF16),
        "w_kvb": w_kvb.astype(BF16),
        "w_out": w_out,
        "q_norm": q_norm.reshape(1, -1),
        "kv_norm": kv_norm.reshape(1, -1),
        "nope_g": nope_g,
        "rope_g": _spread_rope(rope_g),
        "gqa_g": gqa_g,
    }


def kernel(x_prompt, x_sample, cache_mla_ckv, cache_mla_krope, cache_gqa_k, cache_gqa_v, state_ret, c, c_ctx, mod_w, mod_b, ffn_w_gate, ffn_w_up, ffn_w_down, ev_w_in, mla_q_norm, mla_w_qb, mla_kv_norm, mla_w_kvb, mla_nope_norm, mla_rope_norm, gqa_qk_norm, ev_w_out, od_w_in, ret_decay_logit, ret_norm, hy_conv_w, hy_conv_b, hy_filt_w1, hy_filt_b1, hy_filt_w2, hy_filt_b2, hy_filt_freq, hy_filt_w3, hy_decay, hy_skip, od_w_out):
    xc = x_prompt.reshape(BATCH * SEQ, D_MODEL)
    xs = x_sample.reshape(DEC_BATCH * DEC_SEQ, D_MODEL)

    cvec = jnp.concatenate([c_ctx[None, :], c, jnp.zeros((8 - 1 - DEC_BATCH, D_MODEL), F32)], axis=0)
    assert DEPTH == 2
    mod_rows = {0: _mod_vectors(cvec, mod_w, mod_b, 1)[0]}

    tables = tuple(jnp.asarray(t) for t in _rope_tables(DEC_SEQ))

    outs = {}
    for l in range(DEPTH):
        mod_tab = mod_rows[l][:1 + DEC_BATCH].reshape((1 + DEC_BATCH) * N_MOD, 1, D_MODEL)
        xc = _ffn(xc, mod_tab, False, 0, ffn_w_gate, ffn_w_up, ffn_w_down, l, 0)
        xs = _ffn(xs, mod_tab, True, 0, ffn_w_gate, ffn_w_up, ffn_w_down, l, 0)
        if l % 2 == 0:
            e = l // 2
            w = _even_weights(e, ev_w_in, mla_q_norm[e], mla_w_qb[e], mla_kv_norm[e], mla_w_kvb[e],
                              mla_nope_norm[e], mla_rope_norm[e], gqa_qk_norm[e], ev_w_out[e])
            qm, ckv, kr, qg, kg, vg = _even_proj(xc, mod_tab, w)
            km, v = _kv_expand(ckv, kr, w["w_kvb"], w["nope_g"], 2 * TM_PROJ)
            att, mod_rows[l + 1] = _attention(qm, qg, km, v, kg, vg, SEQ, SEQ, seqs=SEQS_PER_STEP,
                                              side=(cvec, mod_w, mod_b, l + 1))
            xc = _out_proj(xc, mod_tab, False, [att], w["w_out"])
            outs["ckv"], outs["krope"], outs["k"], outs["v"] = ckv, _unspread_rope(kr), kg, vg
            qm, ckv, kr, qg, kg, vg = _even_proj(xs, mod_tab, w, tables)
            km, v = _kv_expand(ckv, kr, w["w_kvb"], w["nope_g"], TM_PROJ)
            c_ckv = cache_mla_ckv[:, e].reshape(DEC_BATCH * PAST_LEN, MLA_KV_RANK)
            c_kr = _spread_rope(cache_mla_krope[:, e].reshape(DEC_BATCH * PAST_LEN, MLA_ROPE))
            kmc, vc = _kv_expand(c_ckv, c_kr, w["w_kvb"], w["nope_g"], PAST_LEN)
            kgc = cache_gqa_k[:, e].reshape(DEC_BATCH * PAST_LEN * GQA_KV_HEADS, HEAD_DIM)
            vgc = cache_gqa_v[:, e].reshape(DEC_BATCH * PAST_LEN * GQA_KV_HEADS, HEAD_DIM)
            att = _attention(qm, qg, km, v, kg, vg, DEC_SEQ, TQ_ATT, cache=(kmc, vc, kgc, vgc))
            xs = _out_proj(xs, mod_tab, True, [att], w["w_out"])
        else:
            o = l // 2
            w_out = od_w_out[o]
            logit = ret_decay_logit[o]
            ret_g = ret_norm[o].reshape(1, -1)
            skip = hy_skip[o].reshape(1, -1)
            conv_b = hy_conv_b[o].reshape(1, -1)
            filt_args = (hy_filt_w1[o], hy_filt_b1[o], hy_filt_w2[o], hy_filt_b2[o], hy_filt_freq[o],
                         hy_filt_w3[o], hy_decay[o])
            for stream in ("ctx", "smp"):
                x, seq, latent = (xc, SEQ, False) if stream == "ctx" else (xs, DEC_SEQ, True)
                fwd, inv = _dft_matrices(seq)
                fwd_b, inv_b = jnp.asarray(fwd).astype(BF16), jnp.asarray(inv).astype(BF16)
                spectra = _filter_spectra(seq, *filt_args, fwd_b, TC_HY)
                u = _odd_proj(x, mod_tab, latent, od_w_in, o)
                if stream == "ctx":
                    o_ret, st = _retention(u, logit, ret_g, seq, RET_HEADS, seqs=SEQS_PER_STEP)
                    outs["ret"] = st
                    o_hy = _hyena(u, seq, HY_CH, HY_CH, hy_conv_w[o], conv_b, spectra, skip, fwd_b, inv_b, seqs=SEQS_PER_STEP)
                else:
                    (o_ret,) = _retention(u, logit, ret_g, seq, RET_HEADS_LAT, s0=state_ret[:, o])
                    o_hy = _hyena(u, seq, TC_HY, HY_SUB_LAT, hy_conv_w[o], conv_b, spectra, skip, fwd_b, inv_b)
                x = _out_proj(x, mod_tab, latent, [o_ret, o_hy], w_out)
                if stream == "ctx":
                    xc = x
                else:
                    xs = x
        xc = _ffn(xc, mod_tab, False, 2, ffn_w_gate, ffn_w_up, ffn_w_down, l, 1)
        xs = _ffn(xs, mod_tab, True, 2, ffn_w_gate, ffn_w_up, ffn_w_down, l, 1)

    return (
        xc.reshape(BATCH, SEQ, D_MODEL),
        xs.reshape(DEC_BATCH, DEC_SEQ, D_MODEL),
        outs["ckv"].reshape(BATCH, 1, SEQ, MLA_KV_RANK),
        outs["krope"].reshape(BATCH, 1, SEQ, MLA_ROPE),
        outs["k"].reshape(BATCH, 1, SEQ, GQA_KV_HEADS, HEAD_DIM),
        outs["v"].reshape(BATCH, 1, SEQ, GQA_KV_HEADS, HEAD_DIM),
        outs["ret"].reshape(BATCH, 1, 2, RET_HEADS, RET_DK, RET_DV),
    )
```
